```python
import jax, jax.numpy as jnp
from jax import lax
import numpy as np

D_MODEL = 1024
BATCH = 32
SEQ = 2048
DEPTH = 4

N_NSA_HEADS = 8
N_NSA_KV = 2
NSA_GROUP = N_NSA_HEADS // N_NSA_KV
N_SB_HEADS = 4
N_FOX_HEADS = 4
HEAD_DIM = D_MODEL // (N_NSA_HEADS + N_SB_HEADS + N_FOX_HEADS)
MIX_WIDTH = (N_NSA_HEADS + N_SB_HEADS + N_FOX_HEADS) * HEAD_DIM
ROPE_DIM = HEAD_DIM // 4
ROPE_THETA = 500000.0
CMP_BLOCK = 32
CMP_STRIDE = 16
CMP_HIDDEN = HEAD_DIM
SEL_BLOCK = 64
SEL_TOPK = 16
WINDOW = 512
Q_BLOCK = 128
SEL_Q_BLOCK = 32
MEM_LEN = 256
N_MEM_HEADS = 4
MEM_HEAD_DIM = 64
D_FF = 2816
CONV_WIDTH = 3
EPS = 1e-6

NSA_Q = N_NSA_HEADS * HEAD_DIM
NSA_KV = N_NSA_KV * HEAD_DIM
NSA_GATES = 3 * N_NSA_HEADS
SB_W = N_SB_HEADS * HEAD_DIM
FOX_W = N_FOX_HEADS * HEAD_DIM
IN_COLS = NSA_Q + 6 * NSA_KV + NSA_GATES + 3 * SB_W + 3 * FOX_W + N_FOX_HEADS

kernel_name = "hybrid_nsa_stickbreak_fox_block"


def rms_norm(x, g):
    xf = x.astype(jnp.float32)
    y = xf * lax.rsqrt(jnp.mean(xf * xf, axis=-1, keepdims=True) + EPS)
    return (y * g.astype(jnp.float32)).astype(x.dtype)


def partial_rope(x, pos):
    half = ROPE_DIM // 2
    inv = ROPE_THETA ** (-jnp.arange(half, dtype=jnp.float32) / half)
    ang = pos.astype(jnp.float32)[:, None] * inv[None, :]
    cos = jnp.cos(ang)[None, :, None, :]
    sin = jnp.sin(ang)[None, :, None, :]
    xf = x.astype(jnp.float32)
    x1, x2, rest = xf[..., :half], xf[..., half:ROPE_DIM], xf[..., ROPE_DIM:]
    out = jnp.concatenate([x1 * cos - x2 * sin, x2 * cos + x1 * sin, rest], axis=-1)
    return out.astype(x.dtype)


def masked_softmax(logits, mask):
    logits = jnp.where(mask, logits, -jnp.inf)
    m = jnp.max(logits, axis=-1, keepdims=True)
    m = jnp.where(jnp.isfinite(m), m, 0.0)
    p = jnp.exp(logits - m)
    return p / jnp.maximum(jnp.sum(p, axis=-1, keepdims=True), 1e-30)


def split_columns(proj):
    sizes = [NSA_Q] + [NSA_KV] * 6 + [NSA_GATES] + [SB_W] * 3 + [FOX_W] * 3 + [N_FOX_HEADS]
    out, off = [], 0
    for s in sizes:
        out.append(proj[..., off:off + s])
        off += s
    return out


def nsa_mixer(q, k_cmp, v_cmp, k_sel, v_sel, k_win, v_win, gates, pe_k, pe_v, wk1, wk2, wv1, wv2):
    B, S = q.shape[0], q.shape[1]
    G, Hg, Dh = N_NSA_KV, NSA_GROUP, HEAD_DIM
    f32 = jnp.float32
    scale = Dh ** -0.5
    qg = q.astype(f32).reshape(B, S, G, Hg, Dh).transpose(0, 2, 3, 1, 4)
    t = jnp.arange(S)

    n_cmp = (S - CMP_BLOCK) // CMP_STRIDE + 1
    starts = np.arange(n_cmp) * CMP_STRIDE
    blk = starts[:, None] + np.arange(CMP_BLOCK)[None, :]

    def compress(kv, pe, w1, w2):
        kb = kv.astype(f32)[:, blk] + pe.astype(f32)[None, None, :, None, :]
        hid = jax.nn.silu(jnp.einsum('bnlgd,ldc->bngc', kb, w1.astype(f32)))
        return jnp.einsum('bngc,ce->bgne', hid, w2.astype(f32))

    kc = compress(k_cmp, pe_k, wk1, wk2)
    vc = compress(v_cmp, pe_v, wv1, wv2)
    cmp_mask = (starts + CMP_BLOCK - 1)[None, :] <= t[:, None]
    p_cmp = masked_softmax(jnp.einsum('bghsd,bgnd->bghsn', qg, kc) * scale, cmp_mask)
    o_cmp = jnp.einsum('bghsn,bgnd->bghsd', p_cmp, vc)

    n_sel = S // SEL_BLOCK
    top_n = min(SEL_TOPK, n_sel)
    sel_starts = np.arange(n_sel) * SEL_BLOCK
    overlap = ((starts[:, None] < sel_starts[None, :] + SEL_BLOCK)
               & (starts[:, None] + CMP_BLOCK > sel_starts[None, :])).astype(np.float32)
    imp = jnp.einsum('bghsn,nj->bgsj', p_cmp, jnp.asarray(overlap))
    cur = (t // SEL_BLOCK)[:, None]
    bid = jnp.arange(n_sel)[None, :]
    forced = (bid == 0) | (bid == cur) | (bid == cur - 1)
    imp = jnp.where(bid <= cur, jnp.where(forced, jnp.inf, imp), -jnp.inf)
    top_val, top_idx = lax.top_k(imp, top_n)
    top_ok = top_val > -jnp.inf

    ks_b = k_sel.astype(f32).transpose(0, 2, 1, 3).reshape(B, G, n_sel, SEL_BLOCK, Dh)
    vs_b = v_sel.astype(f32).transpose(0, 2, 1, 3).reshape(B, G, n_sel, SEL_BLOCK, Dh)
    bi = jnp.arange(B)[:, None, None, None]
    gi = jnp.arange(G)[None, :, None, None]
    n_qc = S // SEL_Q_BLOCK

    def sel_chunk(args):
        q_c, idx_c, ok_c, t_c = args
        k_g = ks_b[bi, gi, idx_c]
        v_g = vs_b[bi, gi, idx_c]
        logits = jnp.einsum('bghqd,bgqnld->bghqnl', q_c, k_g) * scale
        tok = idx_c[..., None] * SEL_BLOCK + jnp.arange(SEL_BLOCK)
        mask = (tok <= t_c[None, None, :, None, None]) & ok_c[..., None]
        shp = logits.shape
        p = masked_softmax(logits.reshape(shp[:4] + (-1,)),
                           mask.reshape(B, G, 1, SEL_Q_BLOCK, -1))
        return jnp.einsum('bghqnl,bgqnld->bghqd', p.reshape(shp), v_g)

    q_chunks = jnp.moveaxis(qg.reshape(B, G, Hg, n_qc, SEL_Q_BLOCK, Dh), 3, 0)
    idx_chunks = jnp.moveaxis(top_idx.reshape(B, G, n_qc, SEL_Q_BLOCK, top_n), 2, 0)
    ok_chunks = jnp.moveaxis(top_ok.reshape(B, G, n_qc, SEL_Q_BLOCK, top_n), 2, 0)
    o_slc = lax.map(sel_chunk, (q_chunks, idx_chunks, ok_chunks, t.reshape(n_qc, SEL_Q_BLOCK)))
    o_slc = jnp.moveaxis(o_slc, 0, 3).reshape(B, G, Hg, S, Dh)

    pad = ((0, 0), (0, 0), (WINDOW, 0), (0, 0))
    kw_p = jnp.pad(k_win.astype(f32).transpose(0, 2, 1, 3), pad)
    vw_p = jnp.pad(v_win.astype(f32).transpose(0, 2, 1, 3), pad)
    span = WINDOW + Q_BLOCK
    n_qb = S // Q_BLOCK

    def win_block(args):
        q_b, i = args
        start = i * Q_BLOCK
        k_b = lax.dynamic_slice_in_dim(kw_p, start, span, axis=2)
        v_b = lax.dynamic_slice_in_dim(vw_p, start, span, axis=2)
        tq = start + jnp.arange(Q_BLOCK)
        kpos = start - WINDOW + jnp.arange(span)
        diff = tq[:, None] - kpos[None, :]
        mask = (diff >= 0) & (diff < WINDOW) & (kpos[None, :] >= 0)
        p = masked_softmax(jnp.einsum('bghqd,bgkd->bghqk', q_b, k_b) * scale, mask)
        return jnp.einsum('bghqk,bgkd->bghqd', p, v_b)

    qb = jnp.moveaxis(qg.reshape(B, G, Hg, n_qb, Q_BLOCK, Dh), 3, 0)
    o_win = lax.map(win_block, (qb, jnp.arange(n_qb)))
    o_win = jnp.moveaxis(o_win, 0, 3).reshape(B, G, Hg, S, Dh)

    g = gates.astype(f32).reshape(B, S, G, Hg, 3).transpose(0, 2, 3, 1, 4)
    o = g[..., 0:1] * o_cmp + g[..., 1:2] * o_slc + g[..., 2:3] * o_win
    return o.transpose(0, 3, 1, 2, 4).reshape(B, S, N_NSA_HEADS, Dh)


def stick_breaking_attention(q, k, v):
    B, S, H, Dh = q.shape
    f32 = jnp.float32
    scale = Dh ** -0.5
    qf = q.astype(f32).transpose(0, 2, 1, 3)
    kf = k.astype(f32).transpose(0, 2, 1, 3)
    vf = v.astype(f32).transpose(0, 2, 1, 3)
    n_qb = S // Q_BLOCK
    s_pos = jnp.arange(S)

    def blk(args):
        q_b, i = args
        tq = i * Q_BLOCK + jnp.arange(Q_BLOCK)
        z = jnp.einsum('bhqd,bhkd->bhqk', q_b, kf) * scale
        strict = s_pos[None, :] < tq[:, None]
        log_beta = jax.nn.log_sigmoid(z)
        log_1m = jnp.where(strict, log_beta - z, 0.0)
        excl = lax.cumsum(log_1m, axis=3, reverse=True) - log_1m
        a = jnp.where(strict, jnp.exp(log_beta + excl), 0.0)
        return jnp.einsum('bhqk,bhkd->bhqd', a, vf)

    qb = jnp.moveaxis(qf.reshape(B, H, n_qb, Q_BLOCK, Dh), 2, 0)
    o = lax.map(blk, (qb, jnp.arange(n_qb)))
    return jnp.moveaxis(o, 0, 2).reshape(B, H, S, Dh).transpose(0, 2, 1, 3)


def forgetting_attention(q, k, v, f_logit):
    B, S, H, Dh = q.shape
    f32 = jnp.float32
    scale = Dh ** -0.5
    qf = q.astype(f32).transpose(0, 2, 1, 3)
    kf = k.astype(f32).transpose(0, 2, 1, 3)
    vf = v.astype(f32).transpose(0, 2, 1, 3)
    c = lax.cumsum(jax.nn.log_sigmoid(f_logit.astype(f32)), axis=1).transpose(0, 2, 1)
    n_qb = S // Q_BLOCK
    s_pos = jnp.arange(S)

    def blk(args):
        q_b, c_b, i = args
        tq = i * Q_BLOCK + jnp.arange(Q_BLOCK)
        logits = (jnp.einsum('bhqd,bhkd->bhqk', q_b, kf) * scale
                  + c_b[..., :, None] - c[:, :, None, :])
        p = masked_softmax(logits, s_pos[None, :] <= tq[:, None])
        return jnp.einsum('bhqk,bhkd->bhqd', p, vf)

    qb = jnp.moveaxis(qf.reshape(B, H, n_qb, Q_BLOCK, Dh), 2, 0)
    cb = jnp.moveaxis(c.reshape(B, H, n_qb, Q_BLOCK), 2, 0)
    o = lax.map(blk, (qb, cb, jnp.arange(n_qb)))
    return jnp.moveaxis(o, 0, 2).reshape(B, H, S, Dh).transpose(0, 2, 1, 3)


def memory_cross_attention(h, m, wq, wk, wv, wo):
    B, S, _ = h.shape
    M = m.shape[1]
    f32 = jnp.float32
    q = (h @ wq).astype(f32).reshape(B, S, N_MEM_HEADS, MEM_HEAD_DIM)
    k = (m @ wk).astype(f32).reshape(B, M, N_MEM_HEADS, MEM_HEAD_DIM)
    v = (m @ wv).astype(f32).reshape(B, M, N_MEM_HEADS, MEM_HEAD_DIM)
    p = jax.nn.softmax(jnp.einsum('bshd,bmhd->bhsm', q, k) * MEM_HEAD_DIM ** -0.5, axis=-1)
    o = jnp.einsum('bhsm,bmhd->bshd', p, v).reshape(B, S, N_MEM_HEADS * MEM_HEAD_DIM)
    return o.astype(h.dtype) @ wo


def conv_gated_mlp(h, w_up, conv_w, conv_b, w_down):
    u = h @ w_up
    C = u.shape[-1]
    u = lax.conv_general_dilated(
        u, conv_w.astype(u.dtype).reshape(CONV_WIDTH, 1, C),
        window_strides=(1,), padding=[(CONV_WIDTH - 1, 0)],
        dimension_numbers=('NWC', 'WIO', 'NWC'), feature_group_count=C) + conv_b
    gate, val = u[..., :D_FF], u[..., D_FF:]
    return (jax.nn.silu(gate) * val) @ w_down


def setup_inputs(seed: int = 0) -> dict:
    key = jax.random.key(seed)
    ks = jax.random.split(key, 24)
    f32 = jnp.float32

    def nrm(k, shape, fan_in):
        return jax.random.normal(k, shape, f32) * fan_in ** -0.5

    def gain(k, shape):
        return 1.0 + 0.02 * jax.random.normal(k, shape, f32)

    L, Dh = CMP_BLOCK, HEAD_DIM
    MW = N_MEM_HEADS * MEM_HEAD_DIM
    return {
        "x": jax.random.normal(ks[0], (BATCH, SEQ, D_MODEL), f32),
        "mem": jax.random.normal(ks[1], (BATCH, MEM_LEN, D_MODEL), f32),
        "norm_mix": gain(ks[2], (DEPTH, D_MODEL)),
        "w_in": nrm(ks[3], (DEPTH, D_MODEL, IN_COLS), D_MODEL),
        "b_forget": 3.0 + 0.1 * jax.random.normal(ks[4], (DEPTH, N_FOX_HEADS), f32),
        "cmp_pe_k": 0.1 * jax.random.normal(ks[5], (DEPTH, L, Dh), f32),
        "cmp_pe_v": 0.1 * jax.random.normal(ks[6], (DEPTH, L, Dh), f32),
        "cmp_wk1": nrm(ks[7], (DEPTH, L, Dh, CMP_HIDDEN), L * Dh),
        "cmp_wk2": nrm(ks[8], (DEPTH, CMP_HIDDEN, Dh), CMP_HIDDEN),
        "cmp_wv1": nrm(ks[9], (DEPTH, L, Dh, CMP_HIDDEN), L * Dh),
        "cmp_wv2": nrm(ks[10], (DEPTH, CMP_HIDDEN, Dh), CMP_HIDDEN),
        "w_out": nrm(ks[11], (DEPTH, MIX_WIDTH, D_MODEL), MIX_WIDTH),
        "norm_cross": gain(ks[12], (DEPTH, D_MODEL)),
        "norm_mem": gain(ks[13], (DEPTH, D_MODEL)),
        "w_mq": nrm(ks[14], (DEPTH, D_MODEL, MW), D_MODEL),
        "w_mk": nrm(ks[15], (DEPTH, D_MODEL, MW), D_MODEL),
        "w_mv": nrm(ks[16], (DEPTH, D_MODEL, MW), D_MODEL),
        "w_mo": nrm(ks[17], (DEPTH, MW, D_MODEL), MW),
        "norm_ffn": gain(ks[18], (DEPTH, D_MODEL)),
        "w_up": nrm(ks[19], (DEPTH, D_MODEL, 2 * D_FF), D_MODEL),
        "conv_w": nrm(ks[20], (DEPTH, CONV_WIDTH, 2 * D_FF), CONV_WIDTH),
        "conv_b": 0.01 * jax.random.normal(ks[21], (DEPTH, 2 * D_FF), f32),
        "w_down": nrm(ks[22], (DEPTH, D_FF, D_MODEL), D_FF),
        "norm_final": gain(ks[23], (D_MODEL,)),
    }


def reference(x, mem, norm_mix, w_in, b_forget, cmp_pe_k, cmp_pe_v, cmp_wk1, cmp_wk2,
              cmp_wv1, cmp_wv2, w_out, norm_cross, norm_mem, w_mq, w_mk, w_mv, w_mo,
              norm_ffn, w_up, conv_w, conv_b, w_down, norm_final):
    B, S, _ = x.shape
    pos = jnp.arange(S)

    def heads(a, n):
        return a.reshape(B, S, n, HEAD_DIM)

    for i in range(DEPTH):
        h = rms_norm(x, norm_mix[i])
        (q_n, kc, vc, ks_, vs_, kw, vw, g_n,
         q_s, k_s, v_s, q_f, k_f, v_f, f_l) = split_columns(h @ w_in[i])
        q_n = partial_rope(heads(q_n, N_NSA_HEADS), pos)
        kc = partial_rope(heads(kc, N_NSA_KV), pos)
        ks_ = partial_rope(heads(ks_, N_NSA_KV), pos)
        kw = partial_rope(heads(kw, N_NSA_KV), pos)
        gates = jax.nn.sigmoid(g_n.astype(jnp.float32)).reshape(B, S, N_NSA_HEADS, 3)
        o_nsa = nsa_mixer(q_n, kc, heads(vc, N_NSA_KV), ks_, heads(vs_, N_NSA_KV),
                          kw, heads(vw, N_NSA_KV), gates,
                          cmp_pe_k[i], cmp_pe_v[i], cmp_wk1[i], cmp_wk2[i], cmp_wv1[i], cmp_wv2[i])
        o_sb = stick_breaking_attention(heads(q_s, N_SB_HEADS), heads(k_s, N_SB_HEADS),
                                        heads(v_s, N_SB_HEADS))
        o_fox = forgetting_attention(heads(q_f, N_FOX_HEADS), heads(k_f, N_FOX_HEADS),
                                     heads(v_f, N_FOX_HEADS), f_l + b_forget[i])
        o = jnp.concatenate([o_nsa, o_sb, o_fox], axis=2).reshape(B, S, MIX_WIDTH).astype(x.dtype)
        x = x + o @ w_out[i]
        m = rms_norm(mem, norm_mem[i])
        x = x + memory_cross_attention(rms_norm(x, norm_cross[i]), m,
                                       w_mq[i], w_mk[i], w_mv[i], w_mo[i])
        x = x + conv_gated_mlp(rms_norm(x, norm_ffn[i]), w_up[i], conv_w[i], conv_b[i], w_down[i])
    return rms_norm(x, norm_final)
```

```python
import functools

import numpy as np
import jax
import jax.numpy as jnp
from jax import lax
from jax.experimental import pallas as pl
from jax.experimental.pallas import tpu as pltpu

N_NSA_HEADS = 8
N_NSA_KV = 2
N_SB_HEADS = 4
N_FOX_HEADS = 4
HEAD_DIM = 64
ROPE_DIM = 16
ROPE_THETA = 500000.0
CMP_BLOCK = 32
CMP_STRIDE = 16
SEL_BLOCK = 64
SEL_TOPK = 16
WINDOW = 512
MEM_HEAD_DIM = 64
CONV_WIDTH = 3
EPS = 1e-6

LANES = 128
KEY_CHUNK = 128
NEG_MASK = -1e30
SEL_BIAS = -(2.0 ** 30)
VMEM_LIMIT = 48 * 1024 * 1024

F32 = jnp.float32
BF16 = jnp.bfloat16

T_QN, T_KS, T_VS, T_KW, T_VW = 0, 4, 5, 6, 7
T_QS, T_KSB, T_VSB = 8, 10, 12
T_QF, T_KF, T_VF = 14, 16, 18
N_MAIN_TILES = 20
ROPE_MAIN_TILES = (0, 1, 2, 3, T_KS, T_KW)

_NT = (((1,), (1,)), ((), ()))


def _params(n_grid):
    return pltpu.CompilerParams(dimension_semantics=("arbitrary",) * n_grid,
                                vmem_limit_bytes=VMEM_LIMIT)


def _rms(xf, g):
    return xf * lax.rsqrt(jnp.mean(xf * xf, axis=-1, keepdims=True) + EPS) * g


def _sigmoid(x):
    return 1.0 / (1.0 + jnp.exp(-x))


def _log_sigmoid(x):
    return jnp.minimum(x, 0.0) - jnp.log(1.0 + jnp.exp(-jnp.abs(x)))


def _dot(a, b):
    return jnp.dot(a, b, preferred_element_type=F32)


def _dot_nt(a, b):
    return lax.dot_general(a, b, _NT, preferred_element_type=F32)


def _split2(x):
    hi = x.astype(BF16)
    lo = (x - hi.astype(F32)).astype(BF16)
    return hi, lo


def _proj_kernel(x_ref, g_ref, w_ref, c_ref, s1_ref, s2_ref, main_ref, kcv_ref, small_ref):
    h = _rms(x_ref[...], g_ref[...]).astype(BF16)
    cos = c_ref[...]
    sin_lo = s1_ref[...]
    sin_hi = s2_ref[...]

    def rope(a):
        return a * cos + pltpu.roll(a, LANES - 8, 1) * sin_lo + pltpu.roll(a, 8, 1) * sin_hi

    for c in range(N_MAIN_TILES // 2):
        acc = _dot(h, w_ref[:, 2 * c * LANES:(2 * c + 2) * LANES])
        for k in range(2):
            t = 2 * c + k
            a = acc[:, k * LANES:(k + 1) * LANES]
            if t in ROPE_MAIN_TILES:
                a = rope(a)
            main_ref[:, t * LANES:(t + 1) * LANES] = a.astype(BF16)
    base = N_MAIN_TILES * LANES
    acc = _dot(h, w_ref[:, base:base + 2 * LANES])
    kcv_ref[:, :LANES] = rope(acc[:, :LANES])
    kcv_ref[:, LANES:] = acc[:, LANES:]
    small_ref[...] = _dot(h, w_ref[:, base + 2 * LANES:base + 3 * LANES])


def _proj(x2d, g, w, ctab, s1tab, s2tab, seq, tm):
    t = x2d.shape[0]
    d = x2d.shape[1]
    ncol = w.shape[1]
    spt = seq // tm
    tab = pl.BlockSpec((tm, LANES), lambda i: (i % spt, 0))
    return pl.pallas_call(
        _proj_kernel,
        grid=(t // tm,),
        in_specs=[pl.BlockSpec((tm, d), lambda i: (i, 0)),
                  pl.BlockSpec((1, d), lambda i: (0, 0)),
                  pl.BlockSpec((d, ncol), lambda i: (0, 0)),
                  tab, tab, tab],
        out_specs=[pl.BlockSpec((tm, N_MAIN_TILES * LANES), lambda i: (i, 0)),
                   pl.BlockSpec((tm, 2 * LANES), lambda i: (i, 0)),
                   pl.BlockSpec((tm, LANES), lambda i: (i, 0))],
        out_shape=[jax.ShapeDtypeStruct((t, N_MAIN_TILES * LANES), BF16),
                   jax.ShapeDtypeStruct((t, 2 * LANES), F32),
                   jax.ShapeDtypeStruct((t, LANES), F32)],
        compiler_params=_params(1),
        name="proj",
    )(x2d, g, w, ctab, s1tab, s2tab)


def _gates_kernel(gt_ref, bf_ref, sig_ref, c_ref):
    x = gt_ref[...]
    sig_ref[...] = _sigmoid(x[:24])
    lf = _log_sigmoid(x[24:32] + bf_ref[...])
    seq = lf.shape[1]
    lane = lax.broadcasted_iota(jnp.int32, lf.shape, 1)
    sh = 1
    while sh < seq:
        lf = lf + jnp.where(lane >= sh, pltpu.roll(lf, sh, 1), 0.0)
        sh *= 2
    c_ref[...] = lf


def _gates(gt, bf):
    b, _, seq = gt.shape
    return pl.pallas_call(
        _gates_kernel,
        grid=(b,),
        in_specs=[pl.BlockSpec((None, 32, seq), lambda i: (i, 0, 0)),
                  pl.BlockSpec((8, 1), lambda i: (0, 0))],
        out_specs=[pl.BlockSpec((None, 24, seq), lambda i: (i, 0, 0)),
                   pl.BlockSpec((None, 8, seq), lambda i: (i, 0, 0))],
        out_shape=[jax.ShapeDtypeStruct((b, 24, seq), F32),
                   jax.ShapeDtypeStruct((b, 8, seq), F32)],
        compiler_params=_params(1),
        name="gates",
    )(gt, bf)


def _compress_kernel(r_ref, pea_ref, peb_ref, wa_ref, wb_ref, w2k_ref, w2vt_ref, kc_ref, vct_ref):
    r = r_ref[...]
    a = _dot((r + pea_ref[...]).astype(BF16), wa_ref[...])
    b = _dot((r + peb_ref[...]).astype(BF16), wb_ref[...])
    nrow = r.shape[0]
    hp = a + pltpu.roll(b, nrow - 1, 0)
    hid = (hp * _sigmoid(hp)).astype(BF16)
    kc_ref[...] = _dot(hid[:, :LANES], w2k_ref[...]).astype(BF16)
    vct_ref[...] = _dot_nt(w2vt_ref[...], hid[:, LANES:]).astype(BF16)


def _compress(r3, pea, peb, wa, wb, w2k, w2vt):
    b, nrow, width = r3.shape
    const = lambda i: (0, 0)
    return pl.pallas_call(
        _compress_kernel,
        grid=(b,),
        in_specs=[pl.BlockSpec((None, nrow, width), lambda i: (i, 0, 0)),
                  pl.BlockSpec((1, width), const), pl.BlockSpec((1, width), const),
                  pl.BlockSpec((width, 2 * LANES), const), pl.BlockSpec((width, 2 * LANES), const),
                  pl.BlockSpec((LANES, LANES), const), pl.BlockSpec((LANES, LANES), const)],
        out_specs=[pl.BlockSpec((None, nrow, LANES), lambda i: (i, 0, 0)),
                   pl.BlockSpec((None, LANES, nrow), lambda i: (i, 0, 0))],
        out_shape=[jax.ShapeDtypeStruct((b, nrow, LANES), BF16),
                   jax.ShapeDtypeStruct((b, LANES, nrow), BF16)],
        compiler_params=_params(1),
        name="compress",
    )(r3, pea, peb, wa, wb, w2k, w2vt)


def _nsa_cmp_kernel(q_ref, kc_ref, vct_ref, ovt_ref, ocmp_ref, nm_ref, *, qb, n_cmp, n_sel):
    qi = pl.program_id(1)
    ncp = kc_ref.shape[0]
    tq = qi * qb + lax.broadcasted_iota(jnp.int32, (ncp, qb), 1)
    nblk = lax.broadcasted_iota(jnp.int32, (ncp, qb), 0)
    cmask = (nblk * CMP_STRIDE + (CMP_BLOCK - 1) <= tq) & (nblk < n_cmp)
    row = lax.broadcasted_iota(jnp.int32, (LANES, qb), 0)
    lane = lax.broadcasted_iota(jnp.int32, (qb, LANES), 1)
    kc = kc_ref[...]
    vct = vct_ref[...]
    psum = [jnp.zeros((ncp, qb), F32), jnp.zeros((ncp, qb), F32)]
    for j in range(N_NSA_HEADS // 2):
        qt = q_ref[:, j * LANES:(j + 1) * LANES].astype(F32) * (HEAD_DIM ** -0.5)
        outs = []
        for half in range(2):
            qm = jnp.where(lane < HEAD_DIM if half == 0 else lane >= HEAD_DIM, qt, 0.0).astype(BF16)
            lt = _dot_nt(kc, qm)
            m = jnp.max(jnp.where(cmask, lt, NEG_MASK), axis=0, keepdims=True)
            p = jnp.where(cmask, jnp.exp(lt - m), 0.0)
            p = p / jnp.maximum(jnp.sum(p, axis=0, keepdims=True), 1e-30)
            psum[half] = psum[half] + p
            outs.append(_dot(vct, p.astype(BF16)))
        ot = jnp.where(row < HEAD_DIM, outs[0], outs[1])
        for s in range(qb // LANES):
            ocmp_ref[s * LANES:(s + 1) * LANES, j * LANES:(j + 1) * LANES] = (
                ot[:, s * LANES:(s + 1) * LANES].T)

    jrow = lax.broadcasted_iota(jnp.int32, (n_sel, qb), 0)
    tsel = qi * qb + lax.broadcasted_iota(jnp.int32, (n_sel, qb), 1)
    cur = tsel // SEL_BLOCK
    forced = (jrow == 0) | (jrow == cur) | (jrow == cur - 1)
    ovt = ovt_ref[...]
    for g in range(N_NSA_KV):
        hi, lo = _split2(psum[g])
        imp = _dot(ovt, hi) + _dot(ovt, lo)
        imp = jnp.where(jrow <= cur, jnp.where(forced, jnp.inf, imp), -jnp.inf)
        before = jnp.zeros((n_sel, qb), F32)
        for i in range(n_sel):
            ri = imp[i:i + 1, :]
            ahead = (ri > imp) | ((ri == imp) & (jrow > i))
            before = before + jnp.where(ahead, 1.0, 0.0)
        member = (before < min(SEL_TOPK, n_sel)) & (imp > -jnp.inf)
        not_member = jnp.where(member, 0.0, 1.0)
        padded = jnp.concatenate([not_member, jnp.zeros((LANES - n_sel, qb), F32)], axis=0)
        for s in range(qb // LANES):
            nm_ref[g, s * LANES:(s + 1) * LANES, :] = (
                padded[:, s * LANES:(s + 1) * LANES].T.astype(BF16))


def _nsa_cmp(main, kc, vct, ovt, bsz, seq, qb, n_cmp, n_sel):
    t = main.shape[0]
    nq = seq // qb
    ncp = kc.shape[1]
    kern = functools.partial(_nsa_cmp_kernel, qb=qb, n_cmp=n_cmp, n_sel=n_sel)
    return pl.pallas_call(
        kern,
        grid=(bsz, nq),
        in_specs=[pl.BlockSpec((qb, 4 * LANES), lambda b, i: (b * nq + i, 0)),
                  pl.BlockSpec((None, ncp, LANES), lambda b, i: (b, 0, 0)),
                  pl.BlockSpec((None, LANES, ncp), lambda b, i: (b, 0, 0)),
                  pl.BlockSpec((n_sel, ncp), lambda b, i: (0, 0))],
        out_specs=[pl.BlockSpec((qb, 4 * LANES), lambda b, i: (b * nq + i, 0)),
                   pl.BlockSpec((None, N_NSA_KV, qb, LANES), lambda b, i: (b, 0, i, 0))],
        out_shape=[jax.ShapeDtypeStruct((t, 4 * LANES), F32),
                   jax.ShapeDtypeStruct((bsz, N_NSA_KV, seq, LANES), BF16)],
        compiler_params=_params(2),
        name="nsa_cmp",
    )(main, kc, vct, ovt)


def _softmax_step(s, mask, v, carry):
    m, l, acc = carry
    if mask is not None:
        s = jnp.where(mask, s, NEG_MASK)
    m_new = jnp.maximum(m, jnp.max(s, axis=-1, keepdims=True))
    alpha = jnp.exp(m - m_new)
    p = jnp.exp(s - m_new)
    if mask is not None:
        p = jnp.where(mask, p, 0.0)
    l = alpha * l + jnp.sum(p, axis=-1, keepdims=True)
    acc = alpha * acc + _dot(p.astype(BF16), v)
    return m_new, l, acc


def _softmax_init(qb):
    return (jnp.full((qb, 1), NEG_MASK, F32), jnp.zeros((qb, 1), F32), jnp.zeros((qb, LANES), F32))


def _pair_tile(is_a, a, b):
    return jnp.where(is_a, a, b)


def _nsa_main_kernel(q_ref, ks_ref, vs_ref, kw_ref, vw_ref, nm_ref, eneg_ref, gate_ref, ocmp_ref,
                     o_ref, *, qb):
    qi = pl.program_id(2)
    lane = lax.broadcasted_iota(jnp.int32, (qb, LANES), 1)
    rowi = lax.broadcasted_iota(jnp.int32, (qb, KEY_CHUNK), 0)
    coli = lax.broadcasted_iota(jnp.int32, (qb, KEY_CHUNK), 1)
    is_a = lane < HEAD_DIM
    qt = q_ref[...].astype(F32) * (HEAD_DIM ** -0.5)
    qh = [jnp.where(is_a, qt, 0.0).astype(BF16), jnp.where(is_a, 0.0, qt).astype(BF16)]
    qsel = [jnp.concatenate([qh[h], nm_ref[h]], axis=1) for h in range(2)]

    def sel_chunk(c, carry, mask):
        off = pl.multiple_of(c * KEY_CHUNK, KEY_CHUNK)
        k = jnp.concatenate([ks_ref[pl.ds(off, KEY_CHUNK), :], eneg_ref[pl.ds(off, KEY_CHUNK), :]], axis=1)
        v = vs_ref[pl.ds(off, KEY_CHUNK), :]
        return tuple(_softmax_step(_dot_nt(qsel[h], k), mask, v, carry[h]) for h in range(2))

    init = (_softmax_init(qb), _softmax_init(qb))
    sel = lax.fori_loop(0, qi, lambda c, carry: sel_chunk(c, carry, None), init)
    sel = sel_chunk(qi, sel, coli <= rowi)

    win = init
    nwin = WINDOW // KEY_CHUNK
    for kk in range(nwin + 1):
        c = qi - nwin + kk
        cc = jnp.maximum(c, 0)
        off = pl.multiple_of(cc * KEY_CHUNK, KEY_CHUNK)
        base = jnp.where(c >= 0, (qi - cc) * KEY_CHUNK, -KEY_CHUNK)
        diff = base + rowi - coli
        mask = (diff >= 0) & (diff < WINDOW)
        k = kw_ref[pl.ds(off, KEY_CHUNK), :]
        v = vw_ref[pl.ds(off, KEY_CHUNK), :]
        win = tuple(_softmax_step(_dot_nt(qh[h], k), mask, v, win[h]) for h in range(2))

    def finish(st):
        return _pair_tile(is_a, *[st[h][2] / jnp.maximum(st[h][1], 1e-30) for h in range(2)])

    def gate(r):
        return _pair_tile(is_a, gate_ref[0, r], gate_ref[1, r])

    out = gate(0) * ocmp_ref[...] + gate(1) * finish(sel) + gate(2) * finish(win)
    o_ref[...] = out.astype(BF16)


def _nsa_main(main, nm, eneg, gates6, ocmp, bsz, seq, qb):
    t = main.shape[0]
    nq = seq // qb
    npair = N_NSA_HEADS // 2
    kv = lambda tile: pl.BlockSpec((seq, LANES), lambda b, j, i: (b, tile))
    return pl.pallas_call(
        functools.partial(_nsa_main_kernel, qb=qb),
        grid=(bsz, npair, nq),
        in_specs=[pl.BlockSpec((qb, LANES), lambda b, j, i: (b * nq + i, T_QN + j)),
                  kv(T_KS), kv(T_VS), kv(T_KW), kv(T_VW),
                  pl.BlockSpec((None, N_NSA_KV, qb, LANES), lambda b, j, i: (b, 0, i, 0)),
                  pl.BlockSpec((seq, LANES), lambda b, j, i: (0, 0)),
                  pl.BlockSpec((None, None, 2, 3, qb, 1), lambda b, j, i: (b, j, 0, 0, i, 0)),
                  pl.BlockSpec((qb, LANES), lambda b, j, i: (b * nq + i, j))],
        out_specs=pl.BlockSpec((qb, LANES), lambda b, j, i: (b * nq + i, j)),
        out_shape=jax.ShapeDtypeStruct((t, npair * LANES), BF16),
        compiler_params=_params(3),
        name="nsa_main",
    )(main, main, main, main, main, nm, eneg, gates6, ocmp)


def _sb_kernel(q_ref, k_ref, v_ref, u_ref, o_ref, *, qb):
    qi = pl.program_id(2)
    lane = lax.broadcasted_iota(jnp.int32, (qb, LANES), 1)
    rowi = lax.broadcasted_iota(jnp.int32, (qb, KEY_CHUNK), 0)
    coli = lax.broadcasted_iota(jnp.int32, (qb, KEY_CHUNK), 1)
    is_a = lane < HEAD_DIM
    qt = q_ref[...].astype(F32) * (HEAD_DIM ** -0.5)
    qh = [jnp.where(is_a, qt, 0.0).astype(BF16), jnp.where(is_a, 0.0, qt).astype(BF16)]
    later = u_ref[...]

    def chunk(c, carry, strict):
        off = pl.multiple_of(c * KEY_CHUNK, KEY_CHUNK)
        k = k_ref[pl.ds(off, KEY_CHUNK), :]
        v = v_ref[pl.ds(off, KEY_CHUNK), :]
        new = []
        for h in range(2):
            tail, acc = carry[h]
            z = _dot_nt(qh[h], k)
            log_beta = _log_sigmoid(z)
            log_1m = log_beta - z
            if strict is not None:
                log_1m = jnp.where(strict, log_1m, 0.0)
            hi, lo = _split2(log_1m)
            excl = _dot(hi, later) + _dot(lo, later) + tail
            a = jnp.exp(log_beta + excl)
            if strict is not None:
                a = jnp.where(strict, a, 0.0)
            acc = acc + _dot(a.astype(BF16), v)
            tail = tail + jnp.sum(log_1m, axis=-1, keepdims=True)
            new.append((tail, acc))
        return tuple(new)

    init = tuple((jnp.zeros((qb, 1), F32), jnp.zeros((qb, LANES), F32)) for _ in range(2))
    st = chunk(qi, init, coli < rowi)
    st = lax.fori_loop(0, qi, lambda i, carry: chunk(qi - 1 - i, carry, None), st)
    o_ref[...] = _pair_tile(is_a, st[0][1], st[1][1]).astype(BF16)


def _sb(main, later, bsz, seq, qb):
    t = main.shape[0]
    nq = seq // qb
    npair = N_SB_HEADS // 2
    return pl.pallas_call(
        functools.partial(_sb_kernel, qb=qb),
        grid=(bsz, npair, nq),
        in_specs=[pl.BlockSpec((qb, LANES), lambda b, p, i: (b * nq + i, T_QS + p)),
                  pl.BlockSpec((seq, LANES), lambda b, p, i: (b, T_KSB + p)),
                  pl.BlockSpec((seq, LANES), lambda b, p, i: (b, T_VSB + p)),
                  pl.BlockSpec((KEY_CHUNK, KEY_CHUNK), lambda b, p, i: (0, 0))],
        out_specs=pl.BlockSpec((qb, LANES), lambda b, p, i: (b * nq + i, p)),
        out_shape=jax.ShapeDtypeStruct((t, npair * LANES), BF16),
        compiler_params=_params(3),
        name="sb",
    )(main, main, main, later)


def _fox_kernel(q_ref, k_ref, v_ref, crow_ref, ccol_ref, o_ref, *, qb):
    qi = pl.program_id(2)
    lane = lax.broadcasted_iota(jnp.int32, (qb, LANES), 1)
    rowi = lax.broadcasted_iota(jnp.int32, (qb, KEY_CHUNK), 0)
    coli = lax.broadcasted_iota(jnp.int32, (qb, KEY_CHUNK), 1)
    is_a = lane < HEAD_DIM
    qt = q_ref[...].astype(F32) * (HEAD_DIM ** -0.5)
    qh = [jnp.where(is_a, qt, 0.0).astype(BF16), jnp.where(is_a, 0.0, qt).astype(BF16)]
    ct = [ccol_ref[h] for h in range(2)]

    def chunk(c, carry, mask):
        off = pl.multiple_of(c * KEY_CHUNK, KEY_CHUNK)
        k = k_ref[pl.ds(off, KEY_CHUNK), :]
        v = v_ref[pl.ds(off, KEY_CHUNK), :]
        new = []
        for h in range(2):
            cs = crow_ref[h:h + 1, pl.ds(off, KEY_CHUNK)]
            s = _dot_nt(qh[h], k) + ct[h] - cs
            new.append(_softmax_step(s, mask, v, carry[h]))
        return tuple(new)

    init = (_softmax_init(qb), _softmax_init(qb))
    st = lax.fori_loop(0, qi, lambda c, carry: chunk(c, carry, None), init)
    st = chunk(qi, st, coli <= rowi)
    out = _pair_tile(is_a, *[st[h][2] / jnp.maximum(st[h][1], 1e-30) for h in range(2)])
    o_ref[...] = out.astype(BF16)


def _fox(main, crow, ccol, bsz, seq, qb):
    t = main.shape[0]
    nq = seq // qb
    npair = N_FOX_HEADS // 2
    return pl.pallas_call(
        functools.partial(_fox_kernel, qb=qb),
        grid=(bsz, npair, nq),
        in_specs=[pl.BlockSpec((qb, LANES), lambda b, p, i: (b * nq + i, T_QF + p)),
                  pl.BlockSpec((seq, LANES), lambda b, p, i: (b, T_KF + p)),
                  pl.BlockSpec((seq, LANES), lambda b, p, i: (b, T_VF + p)),
                  pl.BlockSpec((None, None, 2, seq), lambda b, p, i: (b, p, 0, 0)),
                  pl.BlockSpec((None, None, 2, qb, 1), lambda b, p, i: (b, p, 0, i, 0))],
        out_specs=pl.BlockSpec((qb, LANES), lambda b, p, i: (b * nq + i, p)),
        out_shape=jax.ShapeDtypeStruct((t, npair * LANES), BF16),
        compiler_params=_params(3),
        name="fox",
    )(main, main, main, crow, ccol)


def _mem_kv_kernel(m_ref, g_ref, wk_ref, wv_ref, k_ref, v_ref):
    h = _rms(m_ref[...], g_ref[...]).astype(BF16)
    k_ref[...] = _dot(h, wk_ref[...]).astype(BF16)
    v_ref[...] = _dot(h, wv_ref[...]).astype(BF16)


def _mem_kv(mem, g, wk, wv):
    b, m, d = mem.shape
    mw = wk.shape[1]
    const = lambda i: (0, 0)
    return pl.pallas_call(
        _mem_kv_kernel,
        grid=(b,),
        in_specs=[pl.BlockSpec((None, m, d), lambda i: (i, 0, 0)),
                  pl.BlockSpec((1, d), const), pl.BlockSpec((d, mw), const), pl.BlockSpec((d, mw), const)],
        out_specs=[pl.BlockSpec((None, m, mw), lambda i: (i, 0, 0)),
                   pl.BlockSpec((None, m, mw), lambda i: (i, 0, 0))],
        out_shape=[jax.ShapeDtypeStruct((b, m, mw), BF16), jax.ShapeDtypeStruct((b, m, mw), BF16)],
        compiler_params=_params(1),
        name="mem_kv",
    )(mem, g, wk, wv)


def _cross_kernel(x_ref, on_ref, os_ref, of_ref, won_ref, wos_ref, wof_ref, g_ref, wq_ref,
                  km_ref, vm_ref, wmo_ref, o_ref):
    x1 = (x_ref[...] + _dot(on_ref[...], won_ref[...]) + _dot(os_ref[...], wos_ref[...])
          + _dot(of_ref[...], wof_ref[...]))
    h = _rms(x1, g_ref[...]).astype(BF16)
    q = _dot(h, wq_ref[...]) * (MEM_HEAD_DIM ** -0.5)
    tm = q.shape[0]
    lane = lax.broadcasted_iota(jnp.int32, (tm, LANES), 1)
    is_a = lane < MEM_HEAD_DIM
    tiles = []
    for p in range(q.shape[1] // LANES):
        qt = q[:, p * LANES:(p + 1) * LANES]
        k = km_ref[:, p * LANES:(p + 1) * LANES]
        v = vm_ref[:, p * LANES:(p + 1) * LANES]
        outs = []
        for half in range(2):
            qm = jnp.where(is_a if half == 0 else jnp.logical_not(is_a), qt, 0.0).astype(BF16)
            s = _dot_nt(qm, k)
            e = jnp.exp(s - jnp.max(s, axis=-1, keepdims=True))
            pr = e / jnp.sum(e, axis=-1, keepdims=True)
            outs.append(_dot(pr.astype(BF16), v))
        tiles.append(_pair_tile(is_a, outs[0], outs[1]).astype(BF16))
    attn = jnp.concatenate(tiles, axis=1)
    o_ref[...] = x1 + _dot(attn, wmo_ref[...])


def _cross(x2d, on, osb, ofx, won, wos, wof, g, wq, km, vm, wmo, seq, tm):
    t, d = x2d.shape
    mw = wq.shape[1]
    m = km.shape[1]
    spt = seq // tm
    const = lambda i: (0, 0)
    row = lambda w: pl.BlockSpec((tm, w), lambda i: (i, 0))
    return pl.pallas_call(
        _cross_kernel,
        grid=(t // tm,),
        in_specs=[row(d), row(on.shape[1]), row(osb.shape[1]), row(ofx.shape[1]),
                  pl.BlockSpec(won.shape, const), pl.BlockSpec(wos.shape, const),
                  pl.BlockSpec(wof.shape, const), pl.BlockSpec((1, d), const),
                  pl.BlockSpec((d, mw), const),
                  pl.BlockSpec((None, m, mw), lambda i: (i // spt, 0, 0)),
                  pl.BlockSpec((None, m, mw), lambda i: (i // spt, 0, 0)),
                  pl.BlockSpec((mw, d), const)],
        out_specs=row(d),
        out_shape=jax.ShapeDtypeStruct((t, d), F32),
        compiler_params=_params(1),
        name="cross",
    )(x2d, on, osb, ofx, won, wos, wof, g, wq, km, vm, wmo)


def _ffn_kernel(x_ref, g_ref, wug_ref, wuv_ref, cwg_ref, cwv_ref, cbg_ref, cbv_ref, wd_ref, o_ref,
                h_ref, acc_ref, tail_ref, *, tiles_per_seq):
    i = pl.program_id(0)
    f = pl.program_id(1)
    nf = pl.num_programs(1)

    @pl.when(f == 0)
    def _():
        h_ref[...] = _rms(x_ref[...], g_ref[...]).astype(BF16)
        acc_ref[...] = jnp.zeros_like(acc_ref)

    h = h_ref[...]
    tm = h.shape[0]
    first = (i % tiles_per_seq) == 0
    rowi = lax.broadcasted_iota(jnp.int32, (tm, wug_ref.shape[1]), 0)

    @pl.when(first)
    def _():
        tail_ref[f] = jnp.zeros(tail_ref.shape[1:], F32)

    def conv(u, cw_ref, cb_ref, kind):
        prev = tail_ref[f, kind]
        u1 = jnp.where(rowi == 0, prev[7:8], pltpu.roll(u, 1, 0))
        u2 = jnp.where(rowi == 0, prev[6:7], jnp.where(rowi == 1, prev[7:8], pltpu.roll(u, 2, 0)))
        tail_ref[f, kind] = u[tm - 8:, :]
        return cw_ref[2:3, :] * u + cw_ref[1:2, :] * u1 + cw_ref[0:1, :] * u2 + cb_ref[...]

    gate = conv(_dot(h, wug_ref[...]), cwg_ref, cbg_ref, 0)
    val = conv(_dot(h, wuv_ref[...]), cwv_ref, cbv_ref, 1)
    act = (gate * _sigmoid(gate) * val).astype(BF16)
    acc_ref[...] += _dot(act, wd_ref[...])

    @pl.when(f == nf - 1)
    def _():
        o_ref[...] = x_ref[...] + acc_ref[...]


def _ffn(x2d, g, wup, cw, cb, wdown, seq, tm, tf):
    t, d = x2d.shape
    dff = wdown.shape[0]
    nf = dff // tf
    return pl.pallas_call(
        functools.partial(_ffn_kernel, tiles_per_seq=seq // tm),
        grid=(t // tm, nf),
        in_specs=[pl.BlockSpec((tm, d), lambda i, f: (i, 0)),
                  pl.BlockSpec((1, d), lambda i, f: (0, 0)),
                  pl.BlockSpec((d, tf), lambda i, f: (0, f)),
                  pl.BlockSpec((d, tf), lambda i, f: (0, f + nf)),
                  pl.BlockSpec((CONV_WIDTH, tf), lambda i, f: (0, f)),
                  pl.BlockSpec((CONV_WIDTH, tf), lambda i, f: (0, f + nf)),
                  pl.BlockSpec((1, tf), lambda i, f: (0, f)),
                  pl.BlockSpec((1, tf), lambda i, f: (0, f + nf)),
                  pl.BlockSpec((tf, d), lambda i, f: (f, 0))],
        out_specs=pl.BlockSpec((tm, d), lambda i, f: (i, 0)),
        out_shape=jax.ShapeDtypeStruct((t, d), F32),
        scratch_shapes=[pltpu.VMEM((tm, d), BF16), pltpu.VMEM((tm, d), F32),
                        pltpu.VMEM((nf, 2, 8, tf), F32)],
        compiler_params=_params(2),
        name="ffn",
    )(x2d, g, wup, wup, cw, cw, cb, cb, wdown)


def _norm_kernel(x_ref, g_ref, o_ref):
    o_ref[...] = _rms(x_ref[...], g_ref[...])


def _final_norm(x2d, g, tm):
    t, d = x2d.shape
    return pl.pallas_call(
        _norm_kernel,
        grid=(t // tm,),
        in_specs=[pl.BlockSpec((tm, d), lambda i: (i, 0)), pl.BlockSpec((1, d), lambda i: (0, 0))],
        out_specs=pl.BlockSpec((tm, d), lambda i: (i, 0)),
        out_shape=jax.ShapeDtypeStruct((t, d), F32),
        compiler_params=_params(1),
        name="final_norm",
    )(x2d, g)


def _in_column_order():
    nq = N_NSA_HEADS * HEAD_DIM
    nkv = N_NSA_KV * HEAD_DIM
    ngate = 3 * N_NSA_HEADS
    sbw = N_SB_HEADS * HEAD_DIM
    fxw = N_FOX_HEADS * HEAD_DIM
    off = {}
    pos = 0
    for name, size in [("qn", nq), ("kc", nkv), ("vc", nkv), ("ks", nkv), ("vs", nkv), ("kw", nkv),
                       ("vw", nkv), ("gn", ngate), ("qs", sbw), ("ksb", sbw), ("vsb", sbw),
                       ("qf", fxw), ("kf", fxw), ("vf", fxw), ("fl", N_FOX_HEADS)]:
        off[name] = (pos, size)
        pos += size
    rng = lambda name: list(range(off[name][0], off[name][0] + off[name][1]))
    group = N_NSA_HEADS // N_NSA_KV
    cols = []
    for j in range(group):
        for g in range(N_NSA_KV):
            h = g * group + j
            cols += list(range(off["qn"][0] + h * HEAD_DIM, off["qn"][0] + (h + 1) * HEAD_DIM))
    for name in ("ks", "vs", "kw", "vw", "qs", "ksb", "vsb", "qf", "kf", "vf", "kc", "vc", "gn", "fl"):
        cols += rng(name)
    cols += [-1] * (LANES - ngate - N_FOX_HEADS)
    return np.asarray(cols, np.int32), pos


def _nsa_out_rows():
    group = N_NSA_HEADS // N_NSA_KV
    rows = []
    for j in range(group):
        for g in range(N_NSA_KV):
            h = g * group + j
            rows += list(range(h * HEAD_DIM, (h + 1) * HEAD_DIM))
    return np.asarray(rows, np.int32)


def _rope_tables(seq):
    half = ROPE_DIM // 2
    inv = ROPE_THETA ** (-jnp.arange(half, dtype=F32) / half)
    ang = jnp.arange(seq).astype(F32)[:, None] * inv[None, :]
    cos, sin = jnp.cos(ang), jnp.sin(ang)
    ones = jnp.ones((seq, HEAD_DIM - ROPE_DIM), F32)
    zeros = jnp.zeros((seq, HEAD_DIM - half), F32)
    ctab = jnp.concatenate([cos, cos, ones], axis=1)
    s1 = jnp.concatenate([-sin, zeros], axis=1)
    s2 = jnp.concatenate([jnp.zeros((seq, half), F32), sin, jnp.zeros((seq, HEAD_DIM - ROPE_DIM), F32)], axis=1)
    rep = LANES // HEAD_DIM
    return tuple(jnp.tile(a, (1, rep)) for a in (ctab, s1, s2))


def _overlap_t(seq, ncp):
    n_cmp = (seq - CMP_BLOCK) // CMP_STRIDE + 1
    n_sel = seq // SEL_BLOCK
    starts = np.arange(n_cmp) * CMP_STRIDE
    sel_starts = np.arange(n_sel) * SEL_BLOCK
    ov = ((starts[:, None] < sel_starts[None, :] + SEL_BLOCK)
          & (starts[:, None] + CMP_BLOCK > sel_starts[None, :])).astype(np.float32)
    out = np.zeros((n_sel, ncp), np.float32)
    out[:, :n_cmp] = ov.T
    return out, n_cmp, n_sel


def _sel_bias_table(seq, n_sel):
    tab = np.zeros((seq, LANES), np.float32)
    tab[np.arange(seq), np.arange(seq) // SEL_BLOCK] = SEL_BIAS
    return tab


def _blockdiag(blocks):
    n = len(blocks)
    r, c = blocks[0].shape
    out = jnp.zeros((n * r, n * c), blocks[0].dtype)
    for i, blk in enumerate(blocks):
        out = out.at[i * r:(i + 1) * r, i * c:(i + 1) * c].set(blk)
    return out


def _compress_weights(pe_k, pe_v, wk1, wk2, wv1, wv2):
    hop = CMP_STRIDE
    lblk = CMP_BLOCK

    nblk = 2 * N_NSA_KV
    stacked = jnp.stack([wk1] * N_NSA_KV + [wv1] * N_NSA_KV, axis=1)
    bd = jnp.einsum('lkdc,kj->lkdjc', stacked, jnp.eye(nblk, dtype=stacked.dtype))
    bd = bd.reshape(lblk, nblk * HEAD_DIM, nblk * HEAD_DIM).astype(BF16)
    pe = jnp.concatenate([pe_k] * N_NSA_KV + [pe_v] * N_NSA_KV, axis=1).astype(F32)

    def first_layer(lo):
        return bd[lo:lo + hop].reshape(hop * nblk * HEAD_DIM, nblk * HEAD_DIM)

    def pe_row(lo):
        return pe[lo:lo + hop].reshape(1, hop * nblk * HEAD_DIM)

    assert lblk == 2 * hop
    w2k = _blockdiag([wk2] * N_NSA_KV).astype(BF16)
    w2vt = _blockdiag([wv2.T] * N_NSA_KV).astype(BF16)
    return pe_row(0), pe_row(hop), first_layer(0), first_layer(hop), w2k, w2vt


def kernel(x, mem, norm_mix, w_in, b_forget, cmp_pe_k, cmp_pe_v, cmp_wk1, cmp_wk2, cmp_wv1, cmp_wv2, w_out, norm_cross, norm_mem, w_mq, w_mk, w_mv, w_mo, norm_ffn, w_up, conv_w, conv_b, w_down, norm_final):
    bsz, seq, d = x.shape
    depth = w_in.shape[0]
    t = bsz * seq
    dff = w_down.shape[1]
    assert seq % 512 == 0 and d % LANES == 0

    cols, n_in = _in_column_order()
    assert n_in == w_in.shape[2]
    col_ok = jnp.asarray(cols >= 0)[None, :]
    col_src = jnp.asarray(np.maximum(cols, 0))
    out_rows = jnp.asarray(_nsa_out_rows())
    ctab, s1tab, s2tab = _rope_tables(seq)
    ncp = seq // CMP_STRIDE
    ovt_np, n_cmp, n_sel = _overlap_t(seq, ncp)
    ovt = jnp.asarray(ovt_np, BF16)
    eneg = jnp.asarray(_sel_bias_table(seq, n_sel), BF16)
    later = jnp.asarray(np.tril(np.ones((KEY_CHUNK, KEY_CHUNK), np.float32), -1), BF16)
    nsa_w = N_NSA_HEADS * HEAD_DIM
    sb_w = N_SB_HEADS * HEAD_DIM
    bf_pad = jnp.zeros((depth, 8, 1), F32).at[:, :N_FOX_HEADS, 0].set(b_forget)

    xs = x.reshape(t, d)
    for i in range(depth):
        w = jnp.where(col_ok, jnp.take(w_in[i], col_src, axis=1), 0.0).astype(BF16)
        main, kcv, small = _proj(xs, norm_mix[i][None, :], w, ctab, s1tab, s2tab, seq, 512)

        gt = small.reshape(bsz, seq, LANES)[:, :, :32].transpose(0, 2, 1)
        sig, cum = _gates(gt, bf_pad[i])
        group = N_NSA_HEADS // N_NSA_KV
        gates6 = sig.reshape(bsz, N_NSA_KV, group, 3, seq).transpose(0, 2, 1, 3, 4)[..., None]
        crow = cum[:, :N_FOX_HEADS].reshape(bsz, N_FOX_HEADS // 2, 2, seq)
        ccol = crow[..., None]

        pea, peb, wa, wb, w2k, w2vt = _compress_weights(
            cmp_pe_k[i], cmp_pe_v[i], cmp_wk1[i], cmp_wk2[i], cmp_wv1[i], cmp_wv2[i])
        kc, vct = _compress(kcv.reshape(bsz, ncp, CMP_STRIDE * 2 * LANES), pea, peb, wa, wb, w2k, w2vt)

        ocmp, nm = _nsa_cmp(main, kc, vct, ovt, bsz, seq, 256, n_cmp, n_sel)
        o_nsa = _nsa_main(main, nm, eneg, gates6, ocmp, bsz, seq, 128)
        o_sb = _sb(main, later, bsz, seq, 128)
        o_fox = _fox(main, crow, ccol, bsz, seq, 128)

        km, vm = _mem_kv(mem, norm_mem[i][None, :], w_mk[i].astype(BF16), w_mv[i].astype(BF16))
        wo = w_out[i]
        xs = _cross(xs, o_nsa, o_sb, o_fox,
                    jnp.take(wo[:nsa_w], out_rows, axis=0).astype(BF16),
                    wo[nsa_w:nsa_w + sb_w].astype(BF16), wo[nsa_w + sb_w:].astype(BF16),
                    norm_cross[i][None, :], w_mq[i].astype(BF16), km, vm, w_mo[i].astype(BF16), seq, 512)
        xs = _ffn(xs, norm_ffn[i][None, :], w_up[i].astype(BF16), conv_w[i], conv_b[i][None, :],
                  w_down[i].astype(BF16), seq, 512, 256)
    return _final_norm(xs, norm_final[None, :], 1024).reshape(bsz, seq, d)
```

```python
import functools

import numpy as np
import jax
import jax.numpy as jnp
from jax import lax
from jax.experimental import pallas as pl
from jax.experimental.pallas import tpu as pltpu

N_NSA_HEADS = 8
N_NSA_KV = 2
N_SB_HEADS = 4
N_FOX_HEADS = 4
HEAD_DIM = 64
ROPE_DIM = 16
ROPE_THETA = 500000.0
CMP_BLOCK = 32
CMP_STRIDE = 16
SEL_BLOCK = 64
SEL_TOPK = 16
WINDOW = 512
MEM_HEAD_DIM = 64
CONV_WIDTH = 3
EPS = 1e-6

LANES = 128
Q_BLOCK = 512
KEY_BLOCK = 512
SUB = 128
FL_LANE = 3 * N_NSA_HEADS
NEG_MASK = -1e30
SEL_BIAS = -(2.0 ** 30)
VMEM_LIMIT = 48 * 1024 * 1024

F32 = jnp.float32
BF16 = jnp.bfloat16

T_QN, T_KS, T_VS, T_KW, T_VW = 0, 4, 5, 6, 7
T_QS, T_KSB, T_VSB = 8, 10, 12
T_QF, T_KF, T_VF = 14, 16, 18
N_MAIN_TILES = 20
ROPE_MAIN_TILES = (0, 1, 2, 3, T_KS, T_KW)

_NT = (((1,), (1,)), ((), ()))


def _params(n_grid):
    return pltpu.CompilerParams(dimension_semantics=("arbitrary",) * n_grid,
                                vmem_limit_bytes=VMEM_LIMIT)


def _rms(xf, g):
    return xf * lax.rsqrt(jnp.mean(xf * xf, axis=-1, keepdims=True) + EPS) * g


def _sigmoid(x):
    return 1.0 / (1.0 + jnp.exp(-x))


def _log_sigmoid(x):
    return jnp.minimum(x, 0.0) - jnp.log(1.0 + jnp.exp(-jnp.abs(x)))


def _dot(a, b):
    return jnp.dot(a, b, preferred_element_type=F32)


def _dot_nt(a, b):
    return lax.dot_general(a, b, _NT, preferred_element_type=F32)


def _split2(x):
    hi = x.astype(BF16)
    lo = (x - hi.astype(F32)).astype(BF16)
    return hi, lo


def _proj_kernel(x_ref, g_ref, w_ref, c_ref, s1_ref, s2_ref, main_ref, kcv_ref, small_ref):
    h = _rms(x_ref[...], g_ref[...]).astype(BF16)
    cos = c_ref[...]
    sin_lo = s1_ref[...]
    sin_hi = s2_ref[...]

    def rope(a):
        return a * cos + pltpu.roll(a, LANES - 8, 1) * sin_lo + pltpu.roll(a, 8, 1) * sin_hi

    for c in range(N_MAIN_TILES // 2):
        acc = _dot(h, w_ref[:, 2 * c * LANES:(2 * c + 2) * LANES])
        for k in range(2):
            t = 2 * c + k
            a = acc[:, k * LANES:(k + 1) * LANES]
            if t in ROPE_MAIN_TILES:
                a = rope(a)
            main_ref[:, t * LANES:(t + 1) * LANES] = a.astype(BF16)
    base = N_MAIN_TILES * LANES
    acc = _dot(h, w_ref[:, base:base + 2 * LANES])
    kcv_ref[:, :LANES] = rope(acc[:, :LANES])
    kcv_ref[:, LANES:] = acc[:, LANES:]
    small_ref[...] = _dot(h, w_ref[:, base + 2 * LANES:base + 3 * LANES])


def _proj(x2d, g, w, ctab, s1tab, s2tab, seq, tm):
    t = x2d.shape[0]
    d = x2d.shape[1]
    ncol = w.shape[1]
    spt = seq // tm
    tab = pl.BlockSpec((tm, LANES), lambda i: (i % spt, 0))
    return pl.pallas_call(
        _proj_kernel,
        grid=(t // tm,),
        in_specs=[pl.BlockSpec((tm, d), lambda i: (i, 0)),
                  pl.BlockSpec((1, d), lambda i: (0, 0)),
                  pl.BlockSpec((d, ncol), lambda i: (0, 0)),
                  tab, tab, tab],
        out_specs=[pl.BlockSpec((tm, N_MAIN_TILES * LANES), lambda i: (i, 0)),
                   pl.BlockSpec((tm, 2 * LANES), lambda i: (i, 0)),
                   pl.BlockSpec((tm, LANES), lambda i: (i, 0))],
        out_shape=[jax.ShapeDtypeStruct((t, N_MAIN_TILES * LANES), BF16),
                   jax.ShapeDtypeStruct((t, 2 * LANES), F32),
                   jax.ShapeDtypeStruct((t, LANES), F32)],
        compiler_params=_params(1),
        name="proj",
    )(x2d, g, w, ctab, s1tab, s2tab)


def _gates_kernel(s_ref, bf_ref, ccol_ref, crow_ref):
    lf = _log_sigmoid(s_ref[...] + bf_ref[...])
    seq = lf.shape[0]
    row = lax.broadcasted_iota(jnp.int32, lf.shape, 0)
    sh = 1
    while sh < seq:
        lf = lf + jnp.where(row >= sh, pltpu.roll(lf, sh, 0), 0.0)
        sh *= 2
    ccol_ref[...] = lf
    for s in range(seq // LANES):
        blk = lf[s * LANES:(s + 1) * LANES, :].T
        crow_ref[:, s * LANES:(s + 1) * LANES] = blk[FL_LANE:FL_LANE + 8, :]


def _gates(small3, bf):
    b, seq, _ = small3.shape
    return pl.pallas_call(
        _gates_kernel,
        grid=(b,),
        in_specs=[pl.BlockSpec((None, seq, LANES), lambda i: (i, 0, 0)),
                  pl.BlockSpec((1, LANES), lambda i: (0, 0))],
        out_specs=[pl.BlockSpec((None, seq, LANES), lambda i: (i, 0, 0)),
                   pl.BlockSpec((None, 8, seq), lambda i: (i, 0, 0))],
        out_shape=[jax.ShapeDtypeStruct((b, seq, LANES), F32),
                   jax.ShapeDtypeStruct((b, 8, seq), F32)],
        compiler_params=_params(1),
        name="gates",
    )(small3, bf)


def _compress_kernel(r_ref, pea_ref, peb_ref, wa_ref, wb_ref, w2k_ref, w2vt_ref, kc_ref, vct_ref):
    r = r_ref[...]
    a = _dot((r + pea_ref[...]).astype(BF16), wa_ref[...])
    b = _dot((r + peb_ref[...]).astype(BF16), wb_ref[...])
    nrow = r.shape[0]
    hp = a + pltpu.roll(b, nrow - 1, 0)
    hid = (hp * _sigmoid(hp)).astype(BF16)
    kc_ref[...] = _dot(hid[:, :LANES], w2k_ref[...]).astype(BF16)
    vct_ref[...] = _dot_nt(w2vt_ref[...], hid[:, LANES:]).astype(BF16)


def _compress(r3, pea, peb, wa, wb, w2k, w2vt):
    b, nrow, width = r3.shape
    const = lambda i: (0, 0)
    return pl.pallas_call(
        _compress_kernel,
        grid=(b,),
        in_specs=[pl.BlockSpec((None, nrow, width), lambda i: (i, 0, 0)),
                  pl.BlockSpec((1, width), const), pl.BlockSpec((1, width), const),
                  pl.BlockSpec((width, 2 * LANES), const), pl.BlockSpec((width, 2 * LANES), const),
                  pl.BlockSpec((LANES, LANES), const), pl.BlockSpec((LANES, LANES), const)],
        out_specs=[pl.BlockSpec((None, nrow, LANES), lambda i: (i, 0, 0)),
                   pl.BlockSpec((None, LANES, nrow), lambda i: (i, 0, 0))],
        out_shape=[jax.ShapeDtypeStruct((b, nrow, LANES), BF16),
                   jax.ShapeDtypeStruct((b, LANES, nrow), BF16)],
        compiler_params=_params(1),
        name="compress",
    )(r3, pea, peb, wa, wb, w2k, w2vt)


def _nsa_cmp_kernel(q_ref, kc_ref, vct_ref, ovt_ref, ocmp_ref, nm_ref, *, qb, n_cmp, n_sel):
    qi = pl.program_id(1)
    ncp = kc_ref.shape[0]
    tq = qi * qb + lax.broadcasted_iota(jnp.int32, (ncp, qb), 1)
    nblk = lax.broadcasted_iota(jnp.int32, (ncp, qb), 0)
    cmask = (nblk * CMP_STRIDE + (CMP_BLOCK - 1) <= tq) & (nblk < n_cmp)
    row = lax.broadcasted_iota(jnp.int32, (LANES, qb), 0)
    lane = lax.broadcasted_iota(jnp.int32, (qb, LANES), 1)
    kc = kc_ref[...]
    vct = vct_ref[...]
    psum = [jnp.zeros((ncp, qb), F32), jnp.zeros((ncp, qb), F32)]
    for j in range(N_NSA_HEADS // 2):
        qt = q_ref[:, j * LANES:(j + 1) * LANES].astype(F32) * (HEAD_DIM ** -0.5)
        outs = []
        for half in range(2):
            qm = jnp.where(lane < HEAD_DIM if half == 0 else lane >= HEAD_DIM, qt, 0.0).astype(BF16)
            lt = _dot_nt(kc, qm)
            m = jnp.max(jnp.where(cmask, lt, NEG_MASK), axis=0, keepdims=True)
            p = jnp.where(cmask, jnp.exp(lt - m), 0.0)
            p = p / jnp.maximum(jnp.sum(p, axis=0, keepdims=True), 1e-30)
            psum[half] = psum[half] + p
            outs.append(_dot(vct, p.astype(BF16)))
        ot = jnp.where(row < HEAD_DIM, outs[0], outs[1])
        for s in range(qb // LANES):
            ocmp_ref[s * LANES:(s + 1) * LANES, j * LANES:(j + 1) * LANES] = (
                ot[:, s * LANES:(s + 1) * LANES].T)

    jrow = lax.broadcasted_iota(jnp.int32, (n_sel, qb), 0)
    tsel = qi * qb + lax.broadcasted_iota(jnp.int32, (n_sel, qb), 1)
    cur = tsel // SEL_BLOCK
    forced = (jrow == 0) | (jrow == cur) | (jrow == cur - 1)
    ovt = ovt_ref[...]
    for g in range(N_NSA_KV):
        hi, lo = _split2(psum[g])
        imp = _dot(ovt, hi) + _dot(ovt, lo)
        imp = jnp.where(jrow <= cur, jnp.where(forced, jnp.inf, imp), -jnp.inf)
        before = jnp.zeros((n_sel, qb), F32)
        for i in range(n_sel):
            ri = imp[i:i + 1, :]
            ahead = (ri > imp) | ((ri == imp) & (jrow > i))
            before = before + jnp.where(ahead, 1.0, 0.0)
        member = (before < min(SEL_TOPK, n_sel)) & (imp > -jnp.inf)
        not_member = jnp.where(member, 0.0, 1.0)
        padded = jnp.concatenate([not_member, jnp.zeros((LANES - n_sel, qb), F32)], axis=0)
        for s in range(qb // LANES):
            nm_ref[g, s * LANES:(s + 1) * LANES, :] = (
                padded[:, s * LANES:(s + 1) * LANES].T.astype(BF16))


def _nsa_cmp(main, kc, vct, ovt, bsz, seq, qb, n_cmp, n_sel):
    t = main.shape[0]
    nq = seq // qb
    ncp = kc.shape[1]
    kern = functools.partial(_nsa_cmp_kernel, qb=qb, n_cmp=n_cmp, n_sel=n_sel)
    return pl.pallas_call(
        kern,
        grid=(bsz, nq),
        in_specs=[pl.BlockSpec((qb, 4 * LANES), lambda b, i: (b * nq + i, 0)),
                  pl.BlockSpec((None, ncp, LANES), lambda b, i: (b, 0, 0)),
                  pl.BlockSpec((None, LANES, ncp), lambda b, i: (b, 0, 0)),
                  pl.BlockSpec((n_sel, ncp), lambda b, i: (0, 0))],
        out_specs=[pl.BlockSpec((qb, 4 * LANES), lambda b, i: (b * nq + i, 0)),
                   pl.BlockSpec((None, N_NSA_KV, qb, LANES), lambda b, i: (b, 0, i, 0))],
        out_shape=[jax.ShapeDtypeStruct((t, 4 * LANES), F32),
                   jax.ShapeDtypeStruct((bsz, N_NSA_KV, seq, LANES), BF16)],
        compiler_params=_params(2),
        name="nsa_cmp",
    )(main, kc, vct, ovt)


def _softmax_step(s, mask, vaug, carry):
    m, acc = carry
    if mask is not None:
        s = jnp.where(mask, s, NEG_MASK)
    m_new = jnp.maximum(m, jnp.max(s, axis=-1, keepdims=True))
    p = jnp.exp(s - m_new)
    acc = jnp.exp(m - m_new) * acc + _dot(p.astype(BF16), vaug)
    return m_new, acc


def _softmax_init(qb):
    return (jnp.full((qb, 1), NEG_MASK, F32), jnp.zeros((qb, 2 * LANES), F32))


def _softmax_finish(acc):
    return acc[:, :LANES] / jnp.maximum(acc[:, LANES:], 1e-30)


def _pair_tile(is_a, a, b):
    return jnp.where(is_a, a, b)


def _attn_heads(q_ref):
    qb = q_ref.shape[0]
    is_a = lax.broadcasted_iota(jnp.int32, (qb, LANES), 1) < HEAD_DIM
    qt = q_ref[...].astype(F32) * (HEAD_DIM ** -0.5)
    return is_a, [jnp.where(is_a, qt, 0.0).astype(BF16), jnp.where(is_a, 0.0, qt).astype(BF16)]


def _expand(x, e):
    hi = x.astype(BF16)
    r1 = x - hi.astype(F32)
    mid = r1.astype(BF16)
    lo = (r1 - mid.astype(F32)).astype(BF16)
    return _dot(hi, e) + _dot(mid, e) + _dot(lo, e)


def _nsa_main_kernel(q_ref, ks_ref, vs_ref, kw_ref, vw_ref, nm_ref, eneg_ref, small_ref, egate_ref,
                     ocmp_ref, o_ref):
    qb = q_ref.shape[0]
    qi = pl.program_id(2)
    is_a, qh = _attn_heads(q_ref)
    rowi = lax.broadcasted_iota(jnp.int32, (qb, KEY_BLOCK), 0)
    coli = lax.broadcasted_iota(jnp.int32, (qb, KEY_BLOCK), 1)
    qsel = [jnp.concatenate([qh[h], nm_ref[h]], axis=1) for h in range(2)]
    ones_k = jnp.ones((KEY_BLOCK, LANES), BF16)

    def sel_chunk(c, carry, mask):
        off = pl.multiple_of(c * KEY_BLOCK, KEY_BLOCK)
        k = jnp.concatenate([ks_ref[pl.ds(off, KEY_BLOCK), :], eneg_ref[pl.ds(off, KEY_BLOCK), :]], axis=1)
        v = jnp.concatenate([vs_ref[pl.ds(off, KEY_BLOCK), :], ones_k], axis=1)
        return tuple(_softmax_step(_dot_nt(qsel[h], k), mask, v, carry[h]) for h in range(2))

    init = (_softmax_init(qb), _softmax_init(qb))
    sel = lax.fori_loop(0, qi, lambda c, carry: sel_chunk(c, carry, None), init)
    sel = sel_chunk(qi, sel, coli <= rowi)
    sel_t = _pair_tile(is_a, _softmax_finish(sel[0][1]), _softmax_finish(sel[1][1]))

    span = WINDOW + SUB
    rw = lax.broadcasted_iota(jnp.int32, (SUB, span), 0)
    cw = lax.broadcasted_iota(jnp.int32, (SUB, span), 1)
    ones_w = jnp.ones((span, LANES), BF16)
    wins = [[], []]
    for r in range(qb // SUB):
        t0 = qi * qb + r * SUB
        start = pl.multiple_of(jnp.maximum(t0 - WINDOW, 0), SUB)
        diff = (t0 - start) + rw - cw
        mask = (diff >= 0) & (diff < WINDOW)
        k = kw_ref[pl.ds(start, span), :]
        v = jnp.concatenate([vw_ref[pl.ds(start, span), :], ones_w], axis=1)
        for h in range(2):
            s = jnp.where(mask, _dot_nt(qh[h][r * SUB:(r + 1) * SUB], k), NEG_MASK)
            p = jnp.exp(s - jnp.max(s, axis=-1, keepdims=True))
            wins[h].append(_softmax_finish(_dot(p.astype(BF16), v)))
    win_t = _pair_tile(is_a, jnp.concatenate(wins[0], axis=0), jnp.concatenate(wins[1], axis=0))

    gates = _expand(_sigmoid(small_ref[...]), egate_ref[...])
    out = (gates[:, :LANES] * ocmp_ref[...] + gates[:, LANES:2 * LANES] * sel_t
           + gates[:, 2 * LANES:] * win_t)
    o_ref[...] = out.astype(BF16)


def _nsa_main(main, nm, eneg, small, egate, ocmp, bsz, seq):
    t = main.shape[0]
    qb = Q_BLOCK
    nq = seq // qb
    npair = N_NSA_HEADS // 2
    kv = lambda tile: pl.BlockSpec((seq, LANES), lambda b, j, i: (b, tile))
    return pl.pallas_call(
        _nsa_main_kernel,
        grid=(bsz, npair, nq),
        in_specs=[pl.BlockSpec((qb, LANES), lambda b, j, i: (b * nq + i, T_QN + j)),
                  kv(T_KS), kv(T_VS), kv(T_KW), kv(T_VW),
                  pl.BlockSpec((None, N_NSA_KV, qb, LANES), lambda b, j, i: (b, 0, i, 0)),
                  pl.BlockSpec((seq, LANES), lambda b, j, i: (0, 0)),
                  pl.BlockSpec((qb, LANES), lambda b, j, i: (b * nq + i, 0)),
                  pl.BlockSpec((None, LANES, 3 * LANES), lambda b, j, i: (j, 0, 0)),
                  pl.BlockSpec((qb, LANES), lambda b, j, i: (b * nq + i, j))],
        out_specs=pl.BlockSpec((qb, LANES), lambda b, j, i: (b * nq + i, j)),
        out_shape=jax.ShapeDtypeStruct((t, npair * LANES), BF16),
        compiler_params=_params(3),
        name="nsa_main",
    )(main, main, main, main, main, nm, eneg, small, egate, ocmp)


def _sb_kernel(q_ref, k_ref, v_ref, uj_ref, o_ref):
    qb = q_ref.shape[0]
    qi = pl.program_id(2)
    is_a, qh = _attn_heads(q_ref)
    rowi = lax.broadcasted_iota(jnp.int32, (qb, KEY_BLOCK), 0)
    coli = lax.broadcasted_iota(jnp.int32, (qb, KEY_BLOCK), 1)
    row_s = lax.broadcasted_iota(jnp.int32, (qb, SUB), 0)
    col_s = lax.broadcasted_iota(jnp.int32, (qb, SUB), 1)
    uj = uj_ref[...]
    nsub = KEY_BLOCK // SUB

    def chunk(c, carry, diag):
        off = pl.multiple_of(c * KEY_BLOCK, KEY_BLOCK)
        k = k_ref[pl.ds(off, KEY_BLOCK), :]
        v = v_ref[pl.ds(off, KEY_BLOCK), :]
        new = []
        for h in range(2):
            tail, acc = carry[h]
            z = _dot_nt(qh[h], k)
            log_beta = _log_sigmoid(z)
            log_1m = log_beta - z
            if diag:
                log_1m = jnp.where(coli < rowi, log_1m, 0.0)
            parts = [None] * nsub
            for b in reversed(range(nsub)):
                lo_col, hi_col = b * SUB, (b + 1) * SUB
                hi, lo = _split2(log_1m[:, lo_col:hi_col])
                e = _dot(hi, uj) + _dot(lo, uj)
                a = jnp.exp(log_beta[:, lo_col:hi_col] + e[:, :SUB] + tail)
                if diag:
                    a = jnp.where(col_s + lo_col < row_s, a, 0.0)
                parts[b] = a.astype(BF16)
                tail = tail + e[:, SUB:]
            acc = acc + _dot(jnp.concatenate(parts, axis=1), v)
            new.append((tail, acc))
        return tuple(new)

    init = tuple((jnp.zeros((qb, LANES), F32), jnp.zeros((qb, LANES), F32)) for _ in range(2))
    st = chunk(qi, init, True)
    st = lax.fori_loop(0, qi, lambda i, carry: chunk(qi - 1 - i, carry, False), st)
    o_ref[...] = _pair_tile(is_a, st[0][1], st[1][1]).astype(BF16)


def _sb(main, uj, bsz, seq):
    t = main.shape[0]
    qb = Q_BLOCK
    nq = seq // qb
    npair = N_SB_HEADS // 2
    return pl.pallas_call(
        _sb_kernel,
        grid=(bsz, npair, nq),
        in_specs=[pl.BlockSpec((qb, LANES), lambda b, p, i: (b * nq + i, T_QS + p)),
                  pl.BlockSpec((seq, LANES), lambda b, p, i: (b, T_KSB + p)),
                  pl.BlockSpec((seq, LANES), lambda b, p, i: (b, T_VSB + p)),
                  pl.BlockSpec((SUB, 2 * SUB), lambda b, p, i: (0, 0))],
        out_specs=pl.BlockSpec((qb, LANES), lambda b, p, i: (b * nq + i, p)),
        out_shape=jax.ShapeDtypeStruct((t, npair * LANES), BF16),
        compiler_params=_params(3),
        name="sb",
    )(main, main, main, uj)


def _fox_kernel(q_ref, k_ref, v_ref, crow_ref, ccol_ref, esel_ref, o_ref):
    qb = q_ref.shape[0]
    qi = pl.program_id(2)
    is_a, qh = _attn_heads(q_ref)
    rowi = lax.broadcasted_iota(jnp.int32, (qb, KEY_BLOCK), 0)
    coli = lax.broadcasted_iota(jnp.int32, (qb, KEY_BLOCK), 1)
    ccol = ccol_ref[...]
    ct = [_expand(ccol, esel_ref[h]) for h in range(2)]
    ones_k = jnp.ones((KEY_BLOCK, LANES), BF16)

    def chunk(c, carry, mask):
        off = pl.multiple_of(c * KEY_BLOCK, KEY_BLOCK)
        k = k_ref[pl.ds(off, KEY_BLOCK), :]
        v = jnp.concatenate([v_ref[pl.ds(off, KEY_BLOCK), :], ones_k], axis=1)
        new = []
        for h in range(2):
            cs = crow_ref[h:h + 1, pl.ds(off, KEY_BLOCK)]
            s = _dot_nt(qh[h], k) + ct[h] - cs
            new.append(_softmax_step(s, mask, v, carry[h]))
        return tuple(new)

    init = (_softmax_init(qb), _softmax_init(qb))
    st = lax.fori_loop(0, qi, lambda c, carry: chunk(c, carry, None), init)
    st = chunk(qi, st, coli <= rowi)
    out = _pair_tile(is_a, _softmax_finish(st[0][1]), _softmax_finish(st[1][1]))
    o_ref[...] = out.astype(BF16)


def _fox(main, crow, ccol, esel, bsz, seq):
    t = main.shape[0]
    qb = Q_BLOCK
    nq = seq // qb
    npair = N_FOX_HEADS // 2
    return pl.pallas_call(
        _fox_kernel,
        grid=(bsz, npair, nq),
        in_specs=[pl.BlockSpec((qb, LANES), lambda b, p, i: (b * nq + i, T_QF + p)),
                  pl.BlockSpec((seq, LANES), lambda b, p, i: (b, T_KF + p)),
                  pl.BlockSpec((seq, LANES), lambda b, p, i: (b, T_VF + p)),
                  pl.BlockSpec((None, None, 2, seq), lambda b, p, i: (b, p, 0, 0)),
                  pl.BlockSpec((None, qb, LANES), lambda b, p, i: (b, i, 0)),
                  pl.BlockSpec((None, 2, LANES, KEY_BLOCK), lambda b, p, i: (p, 0, 0, 0))],
        out_specs=pl.BlockSpec((qb, LANES), lambda b, p, i: (b * nq + i, p)),
        out_shape=jax.ShapeDtypeStruct((t, npair * LANES), BF16),
        compiler_params=_params(3),
        name="fox",
    )(main, main, main, crow, ccol, esel)


def _mem_kv_kernel(m_ref, g_ref, wk_ref, wv_ref, k_ref, v_ref):
    h = _rms(m_ref[...], g_ref[...]).astype(BF16)
    k_ref[...] = _dot(h, wk_ref[...]).astype(BF16)
    v_ref[...] = _dot(h, wv_ref[...]).astype(BF16)


def _mem_kv(mem, g, wk, wv):
    b, m, d = mem.shape
    mw = wk.shape[1]
    const = lambda i: (0, 0)
    return pl.pallas_call(
        _mem_kv_kernel,
        grid=(b,),
        in_specs=[pl.BlockSpec((None, m, d), lambda i: (i, 0, 0)),
                  pl.BlockSpec((1, d), const), pl.BlockSpec((d, mw), const), pl.BlockSpec((d, mw), const)],
        out_specs=[pl.BlockSpec((None, m, mw), lambda i: (i, 0, 0)),
                   pl.BlockSpec((None, m, mw), lambda i: (i, 0, 0))],
        out_shape=[jax.ShapeDtypeStruct((b, m, mw), BF16), jax.ShapeDtypeStruct((b, m, mw), BF16)],
        compiler_params=_params(1),
        name="mem_kv",
    )(mem, g, wk, wv)


def _cross_kernel(x_ref, on_ref, os_ref, of_ref, won_ref, wos_ref, wof_ref, g_ref, wq_ref,
                  km_ref, vm_ref, wmo_ref, o_ref):
    x1 = (x_ref[...] + _dot(on_ref[...], won_ref[...]) + _dot(os_ref[...], wos_ref[...])
          + _dot(of_ref[...], wof_ref[...]))
    h = _rms(x1, g_ref[...]).astype(BF16)
    q = _dot(h, wq_ref[...]) * (MEM_HEAD_DIM ** -0.5)
    tm = q.shape[0]
    lane = lax.broadcasted_iota(jnp.int32, (tm, LANES), 1)
    is_a = lane < MEM_HEAD_DIM
    tiles = []
    for p in range(q.shape[1] // LANES):
        qt = q[:, p * LANES:(p + 1) * LANES]
        k = km_ref[:, p * LANES:(p + 1) * LANES]
        v = vm_ref[:, p * LANES:(p + 1) * LANES]
        outs = []
        for half in range(2):
            qm = jnp.where(is_a if half == 0 else jnp.logical_not(is_a), qt, 0.0).astype(BF16)
            s = _dot_nt(qm, k)
            e = jnp.exp(s - jnp.max(s, axis=-1, keepdims=True))
            pr = e / jnp.sum(e, axis=-1, keepdims=True)
            outs.append(_dot(pr.astype(BF16), v))
        tiles.append(_pair_tile(is_a, outs[0], outs[1]).astype(BF16))
    attn = jnp.concatenate(tiles, axis=1)
    o_ref[...] = x1 + _dot(attn, wmo_ref[...])


def _cross(x2d, on, osb, ofx, won, wos, wof, g, wq, km, vm, wmo, seq, tm):
    t, d = x2d.shape
    mw = wq.shape[1]
    m = km.shape[1]
    spt = seq // tm
    const = lambda i: (0, 0)
    row = lambda w: pl.BlockSpec((tm, w), lambda i: (i, 0))
    return pl.pallas_call(
        _cross_kernel,
        grid=(t // tm,),
        in_specs=[row(d), row(on.shape[1]), row(osb.shape[1]), row(ofx.shape[1]),
                  pl.BlockSpec(won.shape, const), pl.BlockSpec(wos.shape, const),
                  pl.BlockSpec(wof.shape, const), pl.BlockSpec((1, d), const),
                  pl.BlockSpec((d, mw), const),
                  pl.BlockSpec((None, m, mw), lambda i: (i // spt, 0, 0)),
                  pl.BlockSpec((None, m, mw), lambda i: (i // spt, 0, 0)),
                  pl.BlockSpec((mw, d), const)],
        out_specs=row(d),
        out_shape=jax.ShapeDtypeStruct((t, d), F32),
        compiler_params=_params(1),
        name="cross",
    )(x2d, on, osb, ofx, won, wos, wof, g, wq, km, vm, wmo)


def _ffn_kernel(x_ref, g_ref, wug_ref, wuv_ref, cwg_ref, cwv_ref, cbg_ref, cbv_ref, wd_ref, o_ref,
                h_ref, acc_ref, tail_ref, *, tiles_per_seq):
    i = pl.program_id(0)
    f = pl.program_id(1)
    nf = pl.num_programs(1)

    @pl.when(f == 0)
    def _():
        h_ref[...] = _rms(x_ref[...], g_ref[...]).astype(BF16)
        acc_ref[...] = jnp.zeros_like(acc_ref)

    h = h_ref[...]
    tm = h.shape[0]
    first = (i % tiles_per_seq) == 0
    rowi = lax.broadcasted_iota(jnp.int32, (tm, wug_ref.shape[1]), 0)

    @pl.when(first)
    def _():
        tail_ref[f] = jnp.zeros(tail_ref.shape[1:], F32)

    def conv(u, cw_ref, cb_ref, kind):
        prev = tail_ref[f, kind]
        u1 = jnp.where(rowi == 0, prev[7:8], pltpu.roll(u, 1, 0))
        u2 = jnp.where(rowi == 0, prev[6:7], jnp.where(rowi == 1, prev[7:8], pltpu.roll(u, 2, 0)))
        tail_ref[f, kind] = u[tm - 8:, :]
        return cw_ref[2:3, :] * u + cw_ref[1:2, :] * u1 + cw_ref[0:1, :] * u2 + cb_ref[...]

    gate = conv(_dot(h, wug_ref[...]), cwg_ref, cbg_ref, 0)
    val = conv(_dot(h, wuv_ref[...]), cwv_ref, cbv_ref, 1)
    act = (gate * _sigmoid(gate) * val).astype(BF16)
    acc_ref[...] += _dot(act, wd_ref[...])

    @pl.when(f == nf - 1)
    def _():
        o_ref[...] = x_ref[...] + acc_ref[...]


def _ffn(x2d, g, wup, cw, cb, wdown, seq, tm, tf):
    t, d = x2d.shape
    dff = wdown.shape[0]
    nf = dff // tf
    return pl.pallas_call(
        functools.partial(_ffn_kernel, tiles_per_seq=seq // tm),
        grid=(t // tm, nf),
        in_specs=[pl.BlockSpec((tm, d), lambda i, f: (i, 0)),
                  pl.BlockSpec((1, d), lambda i, f: (0, 0)),
                  pl.BlockSpec((d, tf), lambda i, f: (0, f)),
                  pl.BlockSpec((d, tf), lambda i, f: (0, f + nf)),
                  pl.BlockSpec((CONV_WIDTH, tf), lambda i, f: (0, f)),
                  pl.BlockSpec((CONV_WIDTH, tf), lambda i, f: (0, f + nf)),
                  pl.BlockSpec((1, tf), lambda i, f: (0, f)),
                  pl.BlockSpec((1, tf), lambda i, f: (0, f + nf)),
                  pl.BlockSpec((tf, d), lambda i, f: (f, 0))],
        out_specs=pl.BlockSpec((tm, d), lambda i, f: (i, 0)),
        out_shape=jax.ShapeDtypeStruct((t, d), F32),
        scratch_shapes=[pltpu.VMEM((tm, d), BF16), pltpu.VMEM((tm, d), F32),
                        pltpu.VMEM((nf, 2, 8, tf), F32)],
        compiler_params=_params(2),
        name="ffn",
    )(x2d, g, wup, wup, cw, cw, cb, cb, wdown)


def _norm_kernel(x_ref, g_ref, o_ref):
    o_ref[...] = _rms(x_ref[...], g_ref[...])


def _final_norm(x2d, g, tm):
    t, d = x2d.shape
    return pl.pallas_call(
        _norm_kernel,
        grid=(t // tm,),
        in_specs=[pl.BlockSpec((tm, d), lambda i: (i, 0)), pl.BlockSpec((1, d), lambda i: (0, 0))],
        out_specs=pl.BlockSpec((tm, d), lambda i: (i, 0)),
        out_shape=jax.ShapeDtypeStruct((t, d), F32),
        compiler_params=_params(1),
        name="final_norm",
    )(x2d, g)


def _in_column_order():
    nq = N_NSA_HEADS * HEAD_DIM
    nkv = N_NSA_KV * HEAD_DIM
    ngate = 3 * N_NSA_HEADS
    sbw = N_SB_HEADS * HEAD_DIM
    fxw = N_FOX_HEADS * HEAD_DIM
    off = {}
    pos = 0
    for name, size in [("qn", nq), ("kc", nkv), ("vc", nkv), ("ks", nkv), ("vs", nkv), ("kw", nkv),
                       ("vw", nkv), ("gn", ngate), ("qs", sbw), ("ksb", sbw), ("vsb", sbw),
                       ("qf", fxw), ("kf", fxw), ("vf", fxw), ("fl", N_FOX_HEADS)]:
        off[name] = (pos, size)
        pos += size
    rng = lambda name: list(range(off[name][0], off[name][0] + off[name][1]))
    group = N_NSA_HEADS // N_NSA_KV
    cols = []
    for j in range(group):
        for g in range(N_NSA_KV):
            h = g * group + j
            cols += list(range(off["qn"][0] + h * HEAD_DIM, off["qn"][0] + (h + 1) * HEAD_DIM))
    for name in ("ks", "vs", "kw", "vw", "qs", "ksb", "vsb", "qf", "kf", "vf", "kc", "vc", "gn", "fl"):
        cols += rng(name)
    cols += [-1] * (LANES - ngate - N_FOX_HEADS)
    return np.asarray(cols, np.int32), pos


def _nsa_out_rows():
    group = N_NSA_HEADS // N_NSA_KV
    rows = []
    for j in range(group):
        for g in range(N_NSA_KV):
            h = g * group + j
            rows += list(range(h * HEAD_DIM, (h + 1) * HEAD_DIM))
    return np.asarray(rows, np.int32)


def _rope_tables(seq):
    half = ROPE_DIM // 2
    inv = ROPE_THETA ** (-jnp.arange(half, dtype=F32) / half)
    ang = jnp.arange(seq).astype(F32)[:, None] * inv[None, :]
    cos, sin = jnp.cos(ang), jnp.sin(ang)
    ones = jnp.ones((seq, HEAD_DIM - ROPE_DIM), F32)
    zeros = jnp.zeros((seq, HEAD_DIM - half), F32)
    ctab = jnp.concatenate([cos, cos, ones], axis=1)
    s1 = jnp.concatenate([-sin, zeros], axis=1)
    s2 = jnp.concatenate([jnp.zeros((seq, half), F32), sin, jnp.zeros((seq, HEAD_DIM - ROPE_DIM), F32)], axis=1)
    rep = LANES // HEAD_DIM
    return tuple(jnp.tile(a, (1, rep)) for a in (ctab, s1, s2))


def _overlap_t(seq, ncp):
    n_cmp = (seq - CMP_BLOCK) // CMP_STRIDE + 1
    n_sel = seq // SEL_BLOCK
    starts = np.arange(n_cmp) * CMP_STRIDE
    sel_starts = np.arange(n_sel) * SEL_BLOCK
    ov = ((starts[:, None] < sel_starts[None, :] + SEL_BLOCK)
          & (starts[:, None] + CMP_BLOCK > sel_starts[None, :])).astype(np.float32)
    out = np.zeros((n_sel, ncp), np.float32)
    out[:, :n_cmp] = ov.T
    return out, n_cmp, n_sel


def _sel_bias_table(seq, n_sel):
    tab = np.zeros((seq, LANES), np.float32)
    tab[np.arange(seq), np.arange(seq) // SEL_BLOCK] = SEL_BIAS
    return tab


def _gate_expanders():
    group = N_NSA_HEADS // N_NSA_KV
    e = np.zeros((group, LANES, 3 * LANES), np.float32)
    for j in range(group):
        for g in range(N_NSA_KV):
            h = g * group + j
            for r in range(3):
                lo = r * LANES + g * HEAD_DIM
                e[j, 3 * h + r, lo:lo + HEAD_DIM] = 1.0
    return e


def _forget_selectors():
    npair = N_FOX_HEADS // 2
    e = np.zeros((npair, 2, LANES, KEY_BLOCK), np.float32)
    for p in range(npair):
        for half in range(2):
            e[p, half, FL_LANE + 2 * p + half, :] = 1.0
    return e


def _blockdiag(blocks):
    n = len(blocks)
    r, c = blocks[0].shape
    out = jnp.zeros((n * r, n * c), blocks[0].dtype)
    for i, blk in enumerate(blocks):
        out = out.at[i * r:(i + 1) * r, i * c:(i + 1) * c].set(blk)
    return out


def _compress_weights(pe_k, pe_v, wk1, wk2, wv1, wv2):
    hop = CMP_STRIDE
    lblk = CMP_BLOCK

    nblk = 2 * N_NSA_KV
    stacked = jnp.stack([wk1] * N_NSA_KV + [wv1] * N_NSA_KV, axis=1)
    bd = jnp.einsum('lkdc,kj->lkdjc', stacked, jnp.eye(nblk, dtype=stacked.dtype))
    bd = bd.reshape(lblk, nblk * HEAD_DIM, nblk * HEAD_DIM).astype(BF16)
    pe = jnp.concatenate([pe_k] * N_NSA_KV + [pe_v] * N_NSA_KV, axis=1).astype(F32)

    def first_layer(lo):
        return bd[lo:lo + hop].reshape(hop * nblk * HEAD_DIM, nblk * HEAD_DIM)

    def pe_row(lo):
        return pe[lo:lo + hop].reshape(1, hop * nblk * HEAD_DIM)

    assert lblk == 2 * hop
    w2k = _blockdiag([wk2] * N_NSA_KV).astype(BF16)
    w2vt = _blockdiag([wv2.T] * N_NSA_KV).astype(BF16)
    return pe_row(0), pe_row(hop), first_layer(0), first_layer(hop), w2k, w2vt


def kernel(x, mem, norm_mix, w_in, b_forget, cmp_pe_k, cmp_pe_v, cmp_wk1, cmp_wk2, cmp_wv1, cmp_wv2, w_out, norm_cross, norm_mem, w_mq, w_mk, w_mv, w_mo, norm_ffn, w_up, conv_w, conv_b, w_down, norm_final):
    bsz, seq, d = x.shape
    depth = w_in.shape[0]
    t = bsz * seq
    dff = w_down.shape[1]
    assert seq % 512 == 0 and d % LANES == 0

    cols, n_in = _in_column_order()
    assert n_in == w_in.shape[2]
    col_ok = jnp.asarray(cols >= 0)[None, :]
    col_src = jnp.asarray(np.maximum(cols, 0))
    out_rows = jnp.asarray(_nsa_out_rows())
    ctab, s1tab, s2tab = _rope_tables(seq)
    ncp = seq // CMP_STRIDE
    ovt_np, n_cmp, n_sel = _overlap_t(seq, ncp)
    ovt = jnp.asarray(ovt_np, BF16)
    eneg = jnp.asarray(_sel_bias_table(seq, n_sel), BF16)
    later = np.tril(np.ones((SUB, SUB), np.float32), -1)
    uj = jnp.asarray(np.concatenate([later, np.ones((SUB, SUB), np.float32)], axis=1), BF16)
    egate = jnp.asarray(_gate_expanders(), BF16)
    esel = jnp.asarray(_forget_selectors(), BF16)
    nsa_w = N_NSA_HEADS * HEAD_DIM
    sb_w = N_SB_HEADS * HEAD_DIM
    bf_row = jnp.zeros((depth, 1, LANES), F32).at[:, 0, FL_LANE:FL_LANE + N_FOX_HEADS].set(b_forget)

    xs = x.reshape(t, d)
    for i in range(depth):
        w = jnp.where(col_ok, jnp.take(w_in[i], col_src, axis=1), 0.0).astype(BF16)
        main, kcv, small = _proj(xs, norm_mix[i][None, :], w, ctab, s1tab, s2tab, seq, 512)

        ccol, cum = _gates(small.reshape(bsz, seq, LANES), bf_row[i])
        crow = cum[:, :N_FOX_HEADS].reshape(bsz, N_FOX_HEADS // 2, 2, seq)

        pea, peb, wa, wb, w2k, w2vt = _compress_weights(
            cmp_pe_k[i], cmp_pe_v[i], cmp_wk1[i], cmp_wk2[i], cmp_wv1[i], cmp_wv2[i])
        kc, vct = _compress(kcv.reshape(bsz, ncp, CMP_STRIDE * 2 * LANES), pea, peb, wa, wb, w2k, w2vt)

        ocmp, nm = _nsa_cmp(main, kc, vct, ovt, bsz, seq, 256, n_cmp, n_sel)
        o_nsa = _nsa_main(main, nm, eneg, small, egate, ocmp, bsz, seq)
        o_sb = _sb(main, uj, bsz, seq)
        o_fox = _fox(main, crow, ccol, esel, bsz, seq)

        km, vm = _mem_kv(mem, norm_mem[i][None, :], w_mk[i].astype(BF16), w_mv[i].astype(BF16))
        wo = w_out[i]
        xs = _cross(xs, o_nsa, o_sb, o_fox,
                    jnp.take(wo[:nsa_w], out_rows, axis=0).astype(BF16),
                    wo[nsa_w:nsa_w + sb_w].astype(BF16), wo[nsa_w + sb_w:].astype(BF16),
                    norm_cross[i][None, :], w_mq[i].astype(BF16), km, vm, w_mo[i].astype(BF16), seq, 512)
        xs = _ffn(xs, norm_ffn[i][None, :], w_up[i].astype(BF16), conv_w[i], conv_b[i][None, :],
                  w_down[i].astype(BF16), seq, 512, 256)
    return _final_norm(xs, norm_final[None, :], 1024).reshape(bsz, seq, d)
```

```python
import functools

import numpy as np
import jax
import jax.numpy as jnp
from jax import lax
from jax.experimental import pallas as pl
from jax.experimental.pallas import tpu as pltpu

N_NSA_HEADS = 8
N_NSA_KV = 2
N_SB_HEADS = 4
N_FOX_HEADS = 4
HEAD_DIM = 64
ROPE_DIM = 16
ROPE_THETA = 500000.0
CMP_BLOCK = 32
CMP_STRIDE = 16
SEL_BLOCK = 64
SEL_TOPK = 16
WINDOW = 512
MEM_HEAD_DIM = 64
CONV_WIDTH = 3
EPS = 1e-6

LANES = 128
Q_BLOCK = 512
KEY_BLOCK = 512
SUB = 128
FL_LANE = 3 * N_NSA_HEADS
ROW_TILE = 512
FFN_TF = 1408
CMP_Q_BLOCK = 256
NEG_MASK = -1e30
SEL_BIAS = -(2.0 ** 30)
VMEM_LIMIT = 48 * 1024 * 1024

F32 = jnp.float32
BF16 = jnp.bfloat16

T_QN, T_KS, T_VS, T_KW, T_VW = 0, 4, 5, 6, 7
T_QS, T_KSB, T_VSB = 8, 10, 12
T_QF, T_KF, T_VF = 14, 16, 18
N_MAIN_TILES = 20
ROPE_MAIN_TILES = (0, 1, 2, 3, T_KS, T_KW)

_NT = (((1,), (1,)), ((), ()))


def _params(n_grid):
    return pltpu.CompilerParams(dimension_semantics=("arbitrary",) * n_grid,
                                vmem_limit_bytes=VMEM_LIMIT)


def _rms(xf, g):
    return xf * lax.rsqrt(jnp.mean(xf * xf, axis=-1, keepdims=True) + EPS) * g


def _sigmoid(x):
    return 1.0 / (1.0 + jnp.exp(-x))


def _log_sigmoid(x):
    return jnp.minimum(x, 0.0) - jnp.log(1.0 + jnp.exp(-jnp.abs(x)))


def _dot(a, b):
    return jnp.dot(a, b, preferred_element_type=F32)


def _dot_nt(a, b):
    return lax.dot_general(a, b, _NT, preferred_element_type=F32)


def _split2(x):
    hi = x.astype(BF16)
    lo = (x - hi.astype(F32)).astype(BF16)
    return hi, lo


def _proj_kernel(x_ref, g_ref, w_ref, c_ref, s1_ref, s2_ref, main_ref, kcv_ref, small_ref):
    h = _rms(x_ref[...], g_ref[...]).astype(BF16)
    cos = c_ref[...]
    sin_lo = s1_ref[...]
    sin_hi = s2_ref[...]

    def rope(a):
        return a * cos + pltpu.roll(a, LANES - 8, 1) * sin_lo + pltpu.roll(a, 8, 1) * sin_hi

    for c in range(N_MAIN_TILES // 2):
        acc = _dot(h, w_ref[:, 2 * c * LANES:(2 * c + 2) * LANES])
        for k in range(2):
            t = 2 * c + k
            a = acc[:, k * LANES:(k + 1) * LANES]
            if t in ROPE_MAIN_TILES:
                a = rope(a)
            main_ref[:, t * LANES:(t + 1) * LANES] = a.astype(BF16)
    base = N_MAIN_TILES * LANES
    acc = _dot(h, w_ref[:, base:base + 2 * LANES])
    kcv_ref[:, :LANES] = rope(acc[:, :LANES])
    kcv_ref[:, LANES:] = acc[:, LANES:]
    small_ref[...] = _dot(h, w_ref[:, base + 2 * LANES:base + 3 * LANES])


def _proj(x2d, g, w, ctab, s1tab, s2tab, seq, tm):
    t = x2d.shape[0]
    d = x2d.shape[1]
    ncol = w.shape[1]
    spt = seq // tm
    tab = pl.BlockSpec((tm, LANES), lambda i: (i % spt, 0))
    return pl.pallas_call(
        _proj_kernel,
        grid=(t // tm,),
        in_specs=[pl.BlockSpec((tm, d), lambda i: (i, 0)),
                  pl.BlockSpec((1, d), lambda i: (0, 0)),
                  pl.BlockSpec((d, ncol), lambda i: (0, 0)),
                  tab, tab, tab],
        out_specs=[pl.BlockSpec((tm, N_MAIN_TILES * LANES), lambda i: (i, 0)),
                   pl.BlockSpec((tm, 2 * LANES), lambda i: (i, 0)),
                   pl.BlockSpec((tm, LANES), lambda i: (i, 0))],
        out_shape=[jax.ShapeDtypeStruct((t, N_MAIN_TILES * LANES), BF16),
                   jax.ShapeDtypeStruct((t, 2 * LANES), F32),
                   jax.ShapeDtypeStruct((t, LANES), F32)],
        compiler_params=_params(1),
        name="proj",
    )(x2d, g, w, ctab, s1tab, s2tab)


def _gates_kernel(s_ref, bf_ref, ccol_ref, crow_ref):
    lf = _log_sigmoid(s_ref[...] + bf_ref[...])
    seq = lf.shape[0]
    row = lax.broadcasted_iota(jnp.int32, lf.shape, 0)
    sh = 1
    while sh < seq:
        lf = lf + jnp.where(row >= sh, pltpu.roll(lf, sh, 0), 0.0)
        sh *= 2
    ccol_ref[...] = lf
    for s in range(seq // LANES):
        blk = lf[s * LANES:(s + 1) * LANES, :].T
        crow_ref[:, s * LANES:(s + 1) * LANES] = blk[FL_LANE:FL_LANE + 8, :]


def _gates(small3, bf):
    b, seq, _ = small3.shape
    return pl.pallas_call(
        _gates_kernel,
        grid=(b,),
        in_specs=[pl.BlockSpec((None, seq, LANES), lambda i: (i, 0, 0)),
                  pl.BlockSpec((1, LANES), lambda i: (0, 0))],
        out_specs=[pl.BlockSpec((None, seq, LANES), lambda i: (i, 0, 0)),
                   pl.BlockSpec((None, 8, seq), lambda i: (i, 0, 0))],
        out_shape=[jax.ShapeDtypeStruct((b, seq, LANES), F32),
                   jax.ShapeDtypeStruct((b, 8, seq), F32)],
        compiler_params=_params(1),
        name="gates",
    )(small3, bf)


def _compress_kernel(r_ref, pea_ref, peb_ref, wa_ref, wb_ref, w2k_ref, w2vt_ref, kc_ref, vct_ref):
    r = r_ref[...]
    a = _dot((r + pea_ref[...]).astype(BF16), wa_ref[...])
    b = _dot((r + peb_ref[...]).astype(BF16), wb_ref[...])
    nrow = r.shape[0]
    hp = a + pltpu.roll(b, nrow - 1, 0)
    hid = (hp * _sigmoid(hp)).astype(BF16)
    kc_ref[...] = _dot(hid[:, :LANES], w2k_ref[...]).astype(BF16)
    vct_ref[...] = _dot_nt(w2vt_ref[...], hid[:, LANES:]).astype(BF16)


def _compress(r3, pea, peb, wa, wb, w2k, w2vt):
    b, nrow, width = r3.shape
    const = lambda i: (0, 0)
    return pl.pallas_call(
        _compress_kernel,
        grid=(b,),
        in_specs=[pl.BlockSpec((None, nrow, width), lambda i: (i, 0, 0)),
                  pl.BlockSpec((1, width), const), pl.BlockSpec((1, width), const),
                  pl.BlockSpec((width, 2 * LANES), const), pl.BlockSpec((width, 2 * LANES), const),
                  pl.BlockSpec((LANES, LANES), const), pl.BlockSpec((LANES, LANES), const)],
        out_specs=[pl.BlockSpec((None, nrow, LANES), lambda i: (i, 0, 0)),
                   pl.BlockSpec((None, LANES, nrow), lambda i: (i, 0, 0))],
        out_shape=[jax.ShapeDtypeStruct((b, nrow, LANES), BF16),
                   jax.ShapeDtypeStruct((b, LANES, nrow), BF16)],
        compiler_params=_params(1),
        name="compress",
    )(r3, pea, peb, wa, wb, w2k, w2vt)


def _nsa_cmp_kernel(q_ref, kc_ref, vct_ref, ovt_ref, ocmp_ref, nm_ref, *, qb, n_cmp, n_sel):
    qi = pl.program_id(1)
    ncp = kc_ref.shape[0]
    tq = qi * qb + lax.broadcasted_iota(jnp.int32, (ncp, qb), 1)
    nblk = lax.broadcasted_iota(jnp.int32, (ncp, qb), 0)
    cmask = (nblk * CMP_STRIDE + (CMP_BLOCK - 1) <= tq) & (nblk < n_cmp)
    row = lax.broadcasted_iota(jnp.int32, (LANES, qb), 0)
    lane = lax.broadcasted_iota(jnp.int32, (qb, LANES), 1)
    kc = kc_ref[...]
    vct = vct_ref[...]
    psum = [jnp.zeros((ncp, qb), F32), jnp.zeros((ncp, qb), F32)]
    for j in range(N_NSA_HEADS // 2):
        qt = q_ref[:, j * LANES:(j + 1) * LANES].astype(F32) * (HEAD_DIM ** -0.5)
        outs = []
        for half in range(2):
            qm = jnp.where(lane < HEAD_DIM if half == 0 else lane >= HEAD_DIM, qt, 0.0).astype(BF16)
            lt = _dot_nt(kc, qm)
            m = jnp.max(jnp.where(cmask, lt, NEG_MASK), axis=0, keepdims=True)
            p = jnp.where(cmask, jnp.exp(lt - m), 0.0)
            p = p / jnp.maximum(jnp.sum(p, axis=0, keepdims=True), 1e-30)
            psum[half] = psum[half] + p
            outs.append(_dot(vct, p.astype(BF16)))
        ot = jnp.where(row < HEAD_DIM, outs[0], outs[1])
        for s in range(qb // LANES):
            ocmp_ref[s * LANES:(s + 1) * LANES, j * LANES:(j + 1) * LANES] = (
                ot[:, s * LANES:(s + 1) * LANES].T)

    jrow = lax.broadcasted_iota(jnp.int32, (n_sel, qb), 0)
    tsel = qi * qb + lax.broadcasted_iota(jnp.int32, (n_sel, qb), 1)
    cur = tsel // SEL_BLOCK
    forced = (jrow == 0) | (jrow == cur) | (jrow == cur - 1)
    ovt = ovt_ref[...]
    for g in range(N_NSA_KV):
        hi, lo = _split2(psum[g])
        imp = _dot(ovt, hi) + _dot(ovt, lo)
        imp = jnp.where(jrow <= cur, jnp.where(forced, jnp.inf, imp), -jnp.inf)
        before = jnp.zeros((n_sel, qb), F32)
        for i in range(n_sel):
            ri = imp[i:i + 1, :]
            ahead = (ri > imp) | ((ri == imp) & (jrow > i))
            before = before + jnp.where(ahead, 1.0, 0.0)
        member = (before < min(SEL_TOPK, n_sel)) & (imp > -jnp.inf)
        not_member = jnp.where(member, 0.0, 1.0)
        padded = jnp.concatenate([not_member, jnp.zeros((LANES - n_sel, qb), F32)], axis=0)
        for s in range(qb // LANES):
            nm_ref[g, s * LANES:(s + 1) * LANES, :] = (
                padded[:, s * LANES:(s + 1) * LANES].T.astype(BF16))


def _nsa_cmp(main, kc, vct, ovt, bsz, seq, qb, n_cmp, n_sel):
    t = main.shape[0]
    nq = seq // qb
    ncp = kc.shape[1]
    kern = functools.partial(_nsa_cmp_kernel, qb=qb, n_cmp=n_cmp, n_sel=n_sel)
    return pl.pallas_call(
        kern,
        grid=(bsz, nq),
        in_specs=[pl.BlockSpec((qb, 4 * LANES), lambda b, i: (b * nq + i, 0)),
                  pl.BlockSpec((None, ncp, LANES), lambda b, i: (b, 0, 0)),
                  pl.BlockSpec((None, LANES, ncp), lambda b, i: (b, 0, 0)),
                  pl.BlockSpec((n_sel, ncp), lambda b, i: (0, 0))],
        out_specs=[pl.BlockSpec((qb, 4 * LANES), lambda b, i: (b * nq + i, 0)),
                   pl.BlockSpec((None, N_NSA_KV, qb, LANES), lambda b, i: (b, 0, i, 0))],
        out_shape=[jax.ShapeDtypeStruct((t, 4 * LANES), F32),
                   jax.ShapeDtypeStruct((bsz, N_NSA_KV, seq, LANES), BF16)],
        compiler_params=_params(2),
        name="nsa_cmp",
    )(main, kc, vct, ovt)


def _softmax_step(s, mask, vaug, carry):
    m, acc = carry
    if mask is not None:
        s = jnp.where(mask, s, NEG_MASK)
    m_new = jnp.maximum(m, jnp.max(s, axis=-1, keepdims=True))
    p = jnp.exp(s - m_new)
    acc = jnp.exp(m - m_new) * acc + _dot(p.astype(BF16), vaug)
    return m_new, acc


def _softmax_init(qb):
    return (jnp.full((qb, 1), NEG_MASK, F32), jnp.zeros((qb, 2 * LANES), F32))


def _softmax_finish(acc):
    return acc[:, :LANES] / jnp.maximum(acc[:, LANES:], 1e-30)


def _pair_tile(is_a, a, b):
    return jnp.where(is_a, a, b)


def _attn_heads(q_ref):
    qb = q_ref.shape[0]
    is_a = lax.broadcasted_iota(jnp.int32, (qb, LANES), 1) < HEAD_DIM
    qt = q_ref[...].astype(F32) * (HEAD_DIM ** -0.5)
    return is_a, [jnp.where(is_a, qt, 0.0).astype(BF16), jnp.where(is_a, 0.0, qt).astype(BF16)]


def _expand(x, e):
    hi = x.astype(BF16)
    r1 = x - hi.astype(F32)
    mid = r1.astype(BF16)
    lo = (r1 - mid.astype(F32)).astype(BF16)
    return _dot(hi, e) + _dot(mid, e) + _dot(lo, e)


def _nsa_main_kernel(q_ref, ks_ref, vs_ref, kw_ref, vw_ref, nm_ref, eneg_ref, small_ref, egate_ref,
                     ocmp_ref, o_ref):
    qb = q_ref.shape[0]
    qi = pl.program_id(2)
    is_a, qh = _attn_heads(q_ref)
    rowi = lax.broadcasted_iota(jnp.int32, (qb, KEY_BLOCK), 0)
    coli = lax.broadcasted_iota(jnp.int32, (qb, KEY_BLOCK), 1)
    qsel = [jnp.concatenate([qh[h], nm_ref[h]], axis=1) for h in range(2)]
    ones_k = jnp.ones((KEY_BLOCK, LANES), BF16)

    def sel_chunk(c, carry, mask):
        off = pl.multiple_of(c * KEY_BLOCK, KEY_BLOCK)
        k = jnp.concatenate([ks_ref[pl.ds(off, KEY_BLOCK), :], eneg_ref[pl.ds(off, KEY_BLOCK), :]], axis=1)
        v = jnp.concatenate([vs_ref[pl.ds(off, KEY_BLOCK), :], ones_k], axis=1)
        return tuple(_softmax_step(_dot_nt(qsel[h], k), mask, v, carry[h]) for h in range(2))

    init = (_softmax_init(qb), _softmax_init(qb))
    sel = lax.fori_loop(0, qi, lambda c, carry: sel_chunk(c, carry, None), init)
    sel = sel_chunk(qi, sel, coli <= rowi)
    sel_t = _pair_tile(is_a, _softmax_finish(sel[0][1]), _softmax_finish(sel[1][1]))

    span = WINDOW + SUB
    rw = lax.broadcasted_iota(jnp.int32, (SUB, span), 0)
    cw = lax.broadcasted_iota(jnp.int32, (SUB, span), 1)
    ones_w = jnp.ones((span, LANES), BF16)
    wins = [[], []]
    for r in range(qb // SUB):
        t0 = qi * qb + r * SUB
        start = pl.multiple_of(jnp.maximum(t0 - WINDOW, 0), SUB)
        diff = (t0 - start) + rw - cw
        mask = (diff >= 0) & (diff < WINDOW)
        k = kw_ref[pl.ds(start, span), :]
        v = jnp.concatenate([vw_ref[pl.ds(start, span), :], ones_w], axis=1)
        for h in range(2):
            s = jnp.where(mask, _dot_nt(qh[h][r * SUB:(r + 1) * SUB], k), NEG_MASK)
            p = jnp.exp(s - jnp.max(s, axis=-1, keepdims=True))
            wins[h].append(_softmax_finish(_dot(p.astype(BF16), v)))
    win_t = _pair_tile(is_a, jnp.concatenate(wins[0], axis=0), jnp.concatenate(wins[1], axis=0))

    gates = _expand(_sigmoid(small_ref[...]), egate_ref[...])
    out = (gates[:, :LANES] * ocmp_ref[...] + gates[:, LANES:2 * LANES] * sel_t
           + gates[:, 2 * LANES:] * win_t)
    o_ref[...] = out.astype(BF16)


def _nsa_main(main, nm, eneg, small, egate, ocmp, bsz, seq):
    t = main.shape[0]
    qb = Q_BLOCK
    nq = seq // qb
    npair = N_NSA_HEADS // 2
    kv = lambda tile: pl.BlockSpec((seq, LANES), lambda b, j, i: (b, tile))
    return pl.pallas_call(
        _nsa_main_kernel,
        grid=(bsz, npair, nq),
        in_specs=[pl.BlockSpec((qb, LANES), lambda b, j, i: (b * nq + i, T_QN + j)),
                  kv(T_KS), kv(T_VS), kv(T_KW), kv(T_VW),
                  pl.BlockSpec((None, N_NSA_KV, qb, LANES), lambda b, j, i: (b, 0, i, 0)),
                  pl.BlockSpec((seq, LANES), lambda b, j, i: (0, 0)),
                  pl.BlockSpec((qb, LANES), lambda b, j, i: (b * nq + i, 0)),
                  pl.BlockSpec((None, LANES, 3 * LANES), lambda b, j, i: (j, 0, 0)),
                  pl.BlockSpec((qb, LANES), lambda b, j, i: (b * nq + i, j))],
        out_specs=pl.BlockSpec((qb, LANES), lambda b, j, i: (b * nq + i, j)),
        out_shape=jax.ShapeDtypeStruct((t, npair * LANES), BF16),
        compiler_params=_params(3),
        name="nsa_main",
    )(main, main, main, main, main, nm, eneg, small, egate, ocmp)


def _sb_kernel(q_ref, k_ref, v_ref, u_ref, o_ref):
    qb = q_ref.shape[0]
    qi = pl.program_id(2)
    is_a, qh = _attn_heads(q_ref)
    rowi = lax.broadcasted_iota(jnp.int32, (qb, KEY_BLOCK), 0)
    coli = lax.broadcasted_iota(jnp.int32, (qb, KEY_BLOCK), 1)
    row_s = lax.broadcasted_iota(jnp.int32, (qb, SUB), 0)
    col_s = lax.broadcasted_iota(jnp.int32, (qb, SUB), 1)
    later = u_ref[...]
    nsub = KEY_BLOCK // SUB

    def chunk(c, carry, diag):
        off = pl.multiple_of(c * KEY_BLOCK, KEY_BLOCK)
        k = k_ref[pl.ds(off, KEY_BLOCK), :]
        v = v_ref[pl.ds(off, KEY_BLOCK), :]
        new = []
        for h in range(2):
            tail, acc = carry[h]
            z = _dot_nt(qh[h], k)
            log_beta = _log_sigmoid(z)
            log_1m = log_beta - z
            if diag:
                log_1m = jnp.where(coli < rowi, log_1m, 0.0)
            parts = [None] * nsub
            for b in reversed(range(nsub)):
                lo_col, hi_col = b * SUB, (b + 1) * SUB
                x = log_1m[:, lo_col:hi_col]
                excl = _dot(x.astype(BF16), later)
                a = jnp.exp(log_beta[:, lo_col:hi_col] + excl + tail)
                if diag:
                    a = jnp.where(col_s + lo_col < row_s, a, 0.0)
                parts[b] = a.astype(BF16)
                tail = tail + jnp.sum(x, axis=-1, keepdims=True)
            acc = acc + _dot(jnp.concatenate(parts, axis=1), v)
            new.append((tail, acc))
        return tuple(new)

    init = tuple((jnp.zeros((qb, 1), F32), jnp.zeros((qb, LANES), F32)) for _ in range(2))
    st = chunk(qi, init, True)
    st = lax.fori_loop(0, qi, lambda i, carry: chunk(qi - 1 - i, carry, False), st)
    o_ref[...] = _pair_tile(is_a, st[0][1], st[1][1]).astype(BF16)


def _sb(main, later, bsz, seq):
    t = main.shape[0]
    qb = Q_BLOCK
    nq = seq // qb
    npair = N_SB_HEADS // 2
    return pl.pallas_call(
        _sb_kernel,
        grid=(bsz, npair, nq),
        in_specs=[pl.BlockSpec((qb, LANES), lambda b, p, i: (b * nq + i, T_QS + p)),
                  pl.BlockSpec((seq, LANES), lambda b, p, i: (b, T_KSB + p)),
                  pl.BlockSpec((seq, LANES), lambda b, p, i: (b, T_VSB + p)),
                  pl.BlockSpec((SUB, SUB), lambda b, p, i: (0, 0))],
        out_specs=pl.BlockSpec((qb, LANES), lambda b, p, i: (b * nq + i, p)),
        out_shape=jax.ShapeDtypeStruct((t, npair * LANES), BF16),
        compiler_params=_params(3),
        name="sb",
    )(main, main, main, later)


def _fox_kernel(q_ref, k_ref, v_ref, crow_ref, ccol_ref, esel_ref, o_ref):
    qb = q_ref.shape[0]
    qi = pl.program_id(2)
    is_a, qh = _attn_heads(q_ref)
    rowi = lax.broadcasted_iota(jnp.int32, (qb, KEY_BLOCK), 0)
    coli = lax.broadcasted_iota(jnp.int32, (qb, KEY_BLOCK), 1)
    ccol = ccol_ref[...]
    ct = [_expand(ccol, esel_ref[h]) for h in range(2)]
    ones_k = jnp.ones((KEY_BLOCK, LANES), BF16)

    def chunk(c, carry, mask):
        off = pl.multiple_of(c * KEY_BLOCK, KEY_BLOCK)
        k = k_ref[pl.ds(off, KEY_BLOCK), :]
        v = jnp.concatenate([v_ref[pl.ds(off, KEY_BLOCK), :], ones_k], axis=1)
        new = []
        for h in range(2):
            cs = crow_ref[h:h + 1, pl.ds(off, KEY_BLOCK)]
            new.append(_softmax_step(_dot_nt(qh[h], k) + ct[h] - cs, mask, v, carry[h]))
        return tuple(new)

    init = (_softmax_init(qb), _softmax_init(qb))
    st = lax.fori_loop(0, qi, lambda c, carry: chunk(c, carry, None), init)
    st = chunk(qi, st, coli <= rowi)
    out = _pair_tile(is_a, _softmax_finish(st[0][1]), _softmax_finish(st[1][1]))
    o_ref[...] = out.astype(BF16)


def _fox(main, crow, ccol, esel, bsz, seq):
    t = main.shape[0]
    qb = Q_BLOCK
    nq = seq // qb
    npair = N_FOX_HEADS // 2
    return pl.pallas_call(
        _fox_kernel,
        grid=(bsz, npair, nq),
        in_specs=[pl.BlockSpec((qb, LANES), lambda b, p, i: (b * nq + i, T_QF + p)),
                  pl.BlockSpec((seq, LANES), lambda b, p, i: (b, T_KF + p)),
                  pl.BlockSpec((seq, LANES), lambda b, p, i: (b, T_VF + p)),
                  pl.BlockSpec((None, None, 2, seq), lambda b, p, i: (b, p, 0, 0)),
                  pl.BlockSpec((None, qb, LANES), lambda b, p, i: (b, i, 0)),
                  pl.BlockSpec((None, 2, LANES, KEY_BLOCK), lambda b, p, i: (p, 0, 0, 0))],
        out_specs=pl.BlockSpec((qb, LANES), lambda b, p, i: (b * nq + i, p)),
        out_shape=jax.ShapeDtypeStruct((t, npair * LANES), BF16),
        compiler_params=_params(3),
        name="fox",
    )(main, main, main, crow, ccol, esel)


def _mem_kv_kernel(m_ref, g_ref, wk_ref, wv_ref, k_ref, v_ref):
    h = _rms(m_ref[...], g_ref[...]).astype(BF16)
    k_ref[...] = _dot(h, wk_ref[...]).astype(BF16)
    v_ref[...] = _dot(h, wv_ref[...]).astype(BF16)


def _mem_kv(mem, g, wk, wv):
    b, m, d = mem.shape
    mw = wk.shape[1]
    const = lambda i: (0, 0)
    return pl.pallas_call(
        _mem_kv_kernel,
        grid=(b,),
        in_specs=[pl.BlockSpec((None, m, d), lambda i: (i, 0, 0)),
                  pl.BlockSpec((1, d), const), pl.BlockSpec((d, mw), const), pl.BlockSpec((d, mw), const)],
        out_specs=[pl.BlockSpec((None, m, mw), lambda i: (i, 0, 0)),
                   pl.BlockSpec((None, m, mw), lambda i: (i, 0, 0))],
        out_shape=[jax.ShapeDtypeStruct((b, m, mw), BF16), jax.ShapeDtypeStruct((b, m, mw), BF16)],
        compiler_params=_params(1),
        name="mem_kv",
    )(mem, g, wk, wv)


def _cross_kernel(x_ref, on_ref, os_ref, of_ref, won_ref, wos_ref, wof_ref, g_ref, wq_ref,
                  km_ref, vm_ref, wmo_ref, o_ref):
    x1 = (x_ref[...] + _dot(on_ref[...], won_ref[...]) + _dot(os_ref[...], wos_ref[...])
          + _dot(of_ref[...], wof_ref[...]))
    h = _rms(x1, g_ref[...]).astype(BF16)
    q = _dot(h, wq_ref[...]) * (MEM_HEAD_DIM ** -0.5)
    tm = q.shape[0]
    lane = lax.broadcasted_iota(jnp.int32, (tm, LANES), 1)
    is_a = lane < MEM_HEAD_DIM
    tiles = []
    for p in range(q.shape[1] // LANES):
        qt = q[:, p * LANES:(p + 1) * LANES]
        k = km_ref[:, p * LANES:(p + 1) * LANES]
        v = vm_ref[:, p * LANES:(p + 1) * LANES]
        outs = []
        for half in range(2):
            qm = jnp.where(is_a if half == 0 else jnp.logical_not(is_a), qt, 0.0).astype(BF16)
            s = _dot_nt(qm, k)
            e = jnp.exp(s - jnp.max(s, axis=-1, keepdims=True))
            pr = e / jnp.sum(e, axis=-1, keepdims=True)
            outs.append(_dot(pr.astype(BF16), v))
        tiles.append(_pair_tile(is_a, outs[0], outs[1]).astype(BF16))
    attn = jnp.concatenate(tiles, axis=1)
    o_ref[...] = x1 + _dot(attn, wmo_ref[...])


def _cross(x2d, on, osb, ofx, won, wos, wof, g, wq, km, vm, wmo, seq, tm):
    t, d = x2d.shape
    mw = wq.shape[1]
    m = km.shape[1]
    spt = seq // tm
    const = lambda i: (0, 0)
    row = lambda w: pl.BlockSpec((tm, w), lambda i: (i, 0))
    return pl.pallas_call(
        _cross_kernel,
        grid=(t // tm,),
        in_specs=[row(d), row(on.shape[1]), row(osb.shape[1]), row(ofx.shape[1]),
                  pl.BlockSpec(won.shape, const), pl.BlockSpec(wos.shape, const),
                  pl.BlockSpec(wof.shape, const), pl.BlockSpec((1, d), const),
                  pl.BlockSpec((d, mw), const),
                  pl.BlockSpec((None, m, mw), lambda i: (i // spt, 0, 0)),
                  pl.BlockSpec((None, m, mw), lambda i: (i // spt, 0, 0)),
                  pl.BlockSpec((mw, d), const)],
        out_specs=row(d),
        out_shape=jax.ShapeDtypeStruct((t, d), F32),
        compiler_params=_params(1),
        name="cross",
    )(x2d, on, osb, ofx, won, wos, wof, g, wq, km, vm, wmo)


def _ffn_kernel(x_ref, g_ref, wug_ref, wuv_ref, cwg_ref, cwv_ref, cbg_ref, cbv_ref, wd_ref, o_ref,
                h_ref, acc_ref, tail_ref, *, tiles_per_seq):
    i = pl.program_id(0)
    f = pl.program_id(1)
    nf = pl.num_programs(1)

    @pl.when(f == 0)
    def _():
        h_ref[...] = _rms(x_ref[...], g_ref[...]).astype(BF16)
        acc_ref[...] = jnp.zeros_like(acc_ref)

    h = h_ref[...]
    tm = h.shape[0]
    first = (i % tiles_per_seq) == 0
    rowi = lax.broadcasted_iota(jnp.int32, (tm, wug_ref.shape[1]), 0)

    @pl.when(first)
    def _():
        tail_ref[f] = jnp.zeros(tail_ref.shape[1:], F32)

    def conv(u, cw_ref, cb_ref, kind):
        prev = tail_ref[f, kind]
        u1 = jnp.where(rowi == 0, prev[7:8], pltpu.roll(u, 1, 0))
        u2 = jnp.where(rowi == 0, prev[6:7], jnp.where(rowi == 1, prev[7:8], pltpu.roll(u, 2, 0)))
        tail_ref[f, kind] = u[tm - 8:, :]
        return cw_ref[2:3, :] * u + cw_ref[1:2, :] * u1 + cw_ref[0:1, :] * u2 + cb_ref[...]

    gate = conv(_dot(h, wug_ref[...]), cwg_ref, cbg_ref, 0)
    val = conv(_dot(h, wuv_ref[...]), cwv_ref, cbv_ref, 1)
    act = (gate * _sigmoid(gate) * val).astype(BF16)
    acc_ref[...] += _dot(act, wd_ref[...])

    @pl.when(f == nf - 1)
    def _():
        o_ref[...] = x_ref[...] + acc_ref[...]


def _ffn(x2d, g, wup, cw, cb, wdown, seq, tm, tf):
    t, d = x2d.shape
    dff = wdown.shape[0]
    nf = dff // tf
    return pl.pallas_call(
        functools.partial(_ffn_kernel, tiles_per_seq=seq // tm),
        grid=(t // tm, nf),
        in_specs=[pl.BlockSpec((tm, d), lambda i, f: (i, 0)),
                  pl.BlockSpec((1, d), lambda i, f: (0, 0)),
                  pl.BlockSpec((d, tf), lambda i, f: (0, f)),
                  pl.BlockSpec((d, tf), lambda i, f: (0, f + nf)),
                  pl.BlockSpec((CONV_WIDTH, tf), lambda i, f: (0, f)),
                  pl.BlockSpec((CONV_WIDTH, tf), lambda i, f: (0, f + nf)),
                  pl.BlockSpec((1, tf), lambda i, f: (0, f)),
                  pl.BlockSpec((1, tf), lambda i, f: (0, f + nf)),
                  pl.BlockSpec((tf, d), lambda i, f: (f, 0))],
        out_specs=pl.BlockSpec((tm, d), lambda i, f: (i, 0)),
        out_shape=jax.ShapeDtypeStruct((t, d), F32),
        scratch_shapes=[pltpu.VMEM((tm, d), BF16), pltpu.VMEM((tm, d), F32),
                        pltpu.VMEM((nf, 2, 8, tf), F32)],
        compiler_params=_params(2),
        name="ffn",
    )(x2d, g, wup, wup, cw, cw, cb, cb, wdown)


def _norm_kernel(x_ref, g_ref, o_ref):
    o_ref[...] = _rms(x_ref[...], g_ref[...])


def _final_norm(x2d, g, tm):
    t, d = x2d.shape
    return pl.pallas_call(
        _norm_kernel,
        grid=(t // tm,),
        in_specs=[pl.BlockSpec((tm, d), lambda i: (i, 0)), pl.BlockSpec((1, d), lambda i: (0, 0))],
        out_specs=pl.BlockSpec((tm, d), lambda i: (i, 0)),
        out_shape=jax.ShapeDtypeStruct((t, d), F32),
        compiler_params=_params(1),
        name="final_norm",
    )(x2d, g)


def _in_column_order():
    nq = N_NSA_HEADS * HEAD_DIM
    nkv = N_NSA_KV * HEAD_DIM
    ngate = 3 * N_NSA_HEADS
    sbw = N_SB_HEADS * HEAD_DIM
    fxw = N_FOX_HEADS * HEAD_DIM
    off = {}
    pos = 0
    for name, size in [("qn", nq), ("kc", nkv), ("vc", nkv), ("ks", nkv), ("vs", nkv), ("kw", nkv),
                       ("vw", nkv), ("gn", ngate), ("qs", sbw), ("ksb", sbw), ("vsb", sbw),
                       ("qf", fxw), ("kf", fxw), ("vf", fxw), ("fl", N_FOX_HEADS)]:
        off[name] = (pos, size)
        pos += size
    rng = lambda name: list(range(off[name][0], off[name][0] + off[name][1]))
    group = N_NSA_HEADS // N_NSA_KV
    cols = []
    for j in range(group):
        for g in range(N_NSA_KV):
            h = g * group + j
            cols += list(range(off["qn"][0] + h * HEAD_DIM, off["qn"][0] + (h + 1) * HEAD_DIM))
    for name in ("ks", "vs", "kw", "vw", "qs", "ksb", "vsb", "qf", "kf", "vf", "kc", "vc", "gn", "fl"):
        cols += rng(name)
    cols += [-1] * (LANES - ngate - N_FOX_HEADS)
    return np.asarray(cols, np.int32), pos


def _nsa_out_rows():
    group = N_NSA_HEADS // N_NSA_KV
    rows = []
    for j in range(group):
        for g in range(N_NSA_KV):
            h = g * group + j
            rows += list(range(h * HEAD_DIM, (h + 1) * HEAD_DIM))
    return np.asarray(rows, np.int32)


def _rope_tables(seq):
    half = ROPE_DIM // 2
    inv = ROPE_THETA ** (-jnp.arange(half, dtype=F32) / half)
    ang = jnp.arange(seq).astype(F32)[:, None] * inv[None, :]
    cos, sin = jnp.cos(ang), jnp.sin(ang)
    ones = jnp.ones((seq, HEAD_DIM - ROPE_DIM), F32)
    zeros = jnp.zeros((seq, HEAD_DIM - half), F32)
    ctab = jnp.concatenate([cos, cos, ones], axis=1)
    s1 = jnp.concatenate([-sin, zeros], axis=1)
    s2 = jnp.concatenate([jnp.zeros((seq, half), F32), sin, jnp.zeros((seq, HEAD_DIM - ROPE_DIM), F32)], axis=1)
    rep = LANES // HEAD_DIM
    return tuple(jnp.tile(a, (1, rep)) for a in (ctab, s1, s2))


def _overlap_t(seq, ncp):
    n_cmp = (seq - CMP_BLOCK) // CMP_STRIDE + 1
    n_sel = seq // SEL_BLOCK
    starts = np.arange(n_cmp) * CMP_STRIDE
    sel_starts = np.arange(n_sel) * SEL_BLOCK
    ov = ((starts[:, None] < sel_starts[None, :] + SEL_BLOCK)
          & (starts[:, None] + CMP_BLOCK > sel_starts[None, :])).astype(np.float32)
    out = np.zeros((n_sel, ncp), np.float32)
    out[:, :n_cmp] = ov.T
    return out, n_cmp, n_sel


def _sel_bias_table(seq, n_sel):
    tab = np.zeros((seq, LANES), np.float32)
    tab[np.arange(seq), np.arange(seq) // SEL_BLOCK] = SEL_BIAS
    return tab


def _gate_expanders():
    group = N_NSA_HEADS // N_NSA_KV
    e = np.zeros((group, LANES, 3 * LANES), np.float32)
    for j in range(group):
        for g in range(N_NSA_KV):
            h = g * group + j
            for r in range(3):
                lo = r * LANES + g * HEAD_DIM
                e[j, 3 * h + r, lo:lo + HEAD_DIM] = 1.0
    return e


def _forget_selectors():
    npair = N_FOX_HEADS // 2
    e = np.zeros((npair, 2, LANES, KEY_BLOCK), np.float32)
    for p in range(npair):
        for half in range(2):
            e[p, half, FL_LANE + 2 * p + half, :] = 1.0
    return e


def _blockdiag(blocks):
    n = len(blocks)
    r, c = blocks[0].shape
    out = jnp.zeros((n * r, n * c), blocks[0].dtype)
    for i, blk in enumerate(blocks):
        out = out.at[i * r:(i + 1) * r, i * c:(i + 1) * c].set(blk)
    return out


def _compress_weights(pe_k, pe_v, wk1, wk2, wv1, wv2):
    hop = CMP_STRIDE
    lblk = CMP_BLOCK

    nblk = 2 * N_NSA_KV
    stacked = jnp.stack([wk1] * N_NSA_KV + [wv1] * N_NSA_KV, axis=1)
    bd = jnp.einsum('lkdc,kj->lkdjc', stacked, jnp.eye(nblk, dtype=stacked.dtype))
    bd = bd.reshape(lblk, nblk * HEAD_DIM, nblk * HEAD_DIM).astype(BF16)
    pe = jnp.concatenate([pe_k] * N_NSA_KV + [pe_v] * N_NSA_KV, axis=1).astype(F32)

    def first_layer(lo):
        return bd[lo:lo + hop].reshape(hop * nblk * HEAD_DIM, nblk * HEAD_DIM)

    def pe_row(lo):
        return pe[lo:lo + hop].reshape(1, hop * nblk * HEAD_DIM)

    assert lblk == 2 * hop
    w2k = _blockdiag([wk2] * N_NSA_KV).astype(BF16)
    w2vt = _blockdiag([wv2.T] * N_NSA_KV).astype(BF16)
    return pe_row(0), pe_row(hop), first_layer(0), first_layer(hop), w2k, w2vt


def kernel(x, mem, norm_mix, w_in, b_forget, cmp_pe_k, cmp_pe_v, cmp_wk1, cmp_wk2, cmp_wv1, cmp_wv2, w_out, norm_cross, norm_mem, w_mq, w_mk, w_mv, w_mo, norm_ffn, w_up, conv_w, conv_b, w_down, norm_final):
    bsz, seq, d = x.shape
    depth = w_in.shape[0]
    t = bsz * seq
    dff = w_down.shape[1]
    assert seq % 512 == 0 and d % LANES == 0

    cols, n_in = _in_column_order()
    assert n_in == w_in.shape[2]
    col_ok = jnp.asarray(cols >= 0)[None, :]
    col_src = jnp.asarray(np.maximum(cols, 0))
    out_rows = jnp.asarray(_nsa_out_rows())
    ctab, s1tab, s2tab = _rope_tables(seq)
    ncp = seq // CMP_STRIDE
    ovt_np, n_cmp, n_sel = _overlap_t(seq, ncp)
    ovt = jnp.asarray(ovt_np, BF16)
    eneg = jnp.asarray(_sel_bias_table(seq, n_sel), BF16)
    later = jnp.asarray(np.tril(np.ones((SUB, SUB), np.float32), -1), BF16)
    egate = jnp.asarray(_gate_expanders(), BF16)
    esel = jnp.asarray(_forget_selectors(), BF16)
    nsa_w = N_NSA_HEADS * HEAD_DIM
    sb_w = N_SB_HEADS * HEAD_DIM
    bf_row = jnp.zeros((depth, 1, LANES), F32).at[:, 0, FL_LANE:FL_LANE + N_FOX_HEADS].set(b_forget)

    xs = x.reshape(t, d)
    for i in range(depth):
        w = jnp.where(col_ok, jnp.take(w_in[i], col_src, axis=1), 0.0).astype(BF16)
        main, kcv, small = _proj(xs, norm_mix[i][None, :], w, ctab, s1tab, s2tab, seq, ROW_TILE)

        ccol, cum = _gates(small.reshape(bsz, seq, LANES), bf_row[i])
        crow = cum[:, :N_FOX_HEADS].reshape(bsz, N_FOX_HEADS // 2, 2, seq)

        pea, peb, wa, wb, w2k, w2vt = _compress_weights(
            cmp_pe_k[i], cmp_pe_v[i], cmp_wk1[i], cmp_wk2[i], cmp_wv1[i], cmp_wv2[i])
        kc, vct = _compress(kcv.reshape(bsz, ncp, CMP_STRIDE * 2 * LANES), pea, peb, wa, wb, w2k, w2vt)

        ocmp, nm = _nsa_cmp(main, kc, vct, ovt, bsz, seq, CMP_Q_BLOCK, n_cmp, n_sel)
        o_nsa = _nsa_main(main, nm, eneg, small, egate, ocmp, bsz, seq)
        o_sb = _sb(main, later, bsz, seq)
        o_fox = _fox(main, crow, ccol, esel, bsz, seq)

        km, vm = _mem_kv(mem, norm_mem[i][None, :], w_mk[i].astype(BF16), w_mv[i].astype(BF16))
        wo = w_out[i]
        xs = _cross(xs, o_nsa, o_sb, o_fox,
                    jnp.take(wo[:nsa_w], out_rows, axis=0).astype(BF16),
                    wo[nsa_w:nsa_w + sb_w].astype(BF16), wo[nsa_w + sb_w:].astype(BF16),
                    norm_cross[i][None, :], w_mq[i].astype(BF16), km, vm, w_mo[i].astype(BF16), seq, ROW_TILE)
        xs = _ffn(xs, norm_ffn[i][None, :], w_up[i].astype(BF16), conv_w[i], conv_b[i][None, :],
                  w_down[i].astype(BF16), seq, ROW_TILE, FFN_TF)
    return _final_norm(xs, norm_final[None, :], 2 * ROW_TILE).reshape(bsz, seq, d)
```

```python
import functools

import numpy as np
import jax
import jax.numpy as jnp
from jax import lax
from jax.experimental import pallas as pl
from jax.experimental.pallas import tpu as pltpu

N_NSA_HEADS = 8
N_NSA_KV = 2
N_SB_HEADS = 4
N_FOX_HEADS = 4
HEAD_DIM = 64
ROPE_DIM = 16
ROPE_THETA = 500000.0
CMP_BLOCK = 32
CMP_STRIDE = 16
SEL_BLOCK = 64
SEL_TOPK = 16
WINDOW = 512
MEM_HEAD_DIM = 64
CONV_WIDTH = 3
EPS = 1e-6

LANES = 128
Q_BLOCK = 512
KEY_BLOCK = 512
SUB = 128
FL_LANE = 3 * N_NSA_HEADS
ROW_TILE = 512
FFN_TF = 1408
CMP_Q_BLOCK = 256
NSA_TILES = 2
NEG_MASK = -1e30
SEL_BIAS = -(2.0 ** 30)
VMEM_LIMIT = 48 * 1024 * 1024

F32 = jnp.float32
BF16 = jnp.bfloat16

T_QN, T_KS, T_VS, T_KW, T_VW = 0, 4, 5, 6, 7
T_QS, T_KSB, T_VSB = 8, 10, 12
T_QF, T_KF, T_VF = 14, 16, 18
N_MAIN_TILES = 20
ROPE_MAIN_TILES = (0, 1, 2, 3, T_KS, T_KW)

_NT = (((1,), (1,)), ((), ()))


def _params(n_grid):
    return pltpu.CompilerParams(dimension_semantics=("arbitrary",) * n_grid,
                                vmem_limit_bytes=VMEM_LIMIT)


def _rms(xf, g):
    return xf * lax.rsqrt(jnp.mean(xf * xf, axis=-1, keepdims=True) + EPS) * g


def _sigmoid(x):
    return 1.0 / (1.0 + jnp.exp(-x))


def _log_sigmoid(x):
    return jnp.minimum(x, 0.0) - jnp.log(1.0 + jnp.exp(-jnp.abs(x)))


def _dot(a, b):
    return jnp.dot(a, b, preferred_element_type=F32)


def _dot_nt(a, b):
    return lax.dot_general(a, b, _NT, preferred_element_type=F32)


def _split2(x):
    hi = x.astype(BF16)
    lo = (x - hi.astype(F32)).astype(BF16)
    return hi, lo


def _proj_kernel(x_ref, g_ref, w_ref, c_ref, s1_ref, s2_ref, main_ref, kcv_ref, small_ref):
    h = _rms(x_ref[...], g_ref[...]).astype(BF16)
    cos = c_ref[...]
    sin_lo = s1_ref[...]
    sin_hi = s2_ref[...]

    def rope(a):
        return a * cos + pltpu.roll(a, LANES - 8, 1) * sin_lo + pltpu.roll(a, 8, 1) * sin_hi

    for c in range(N_MAIN_TILES // 2):
        acc = _dot(h, w_ref[:, 2 * c * LANES:(2 * c + 2) * LANES])
        for k in range(2):
            t = 2 * c + k
            a = acc[:, k * LANES:(k + 1) * LANES]
            if t in ROPE_MAIN_TILES:
                a = rope(a)
            main_ref[:, t * LANES:(t + 1) * LANES] = a.astype(BF16)
    base = N_MAIN_TILES * LANES
    acc = _dot(h, w_ref[:, base:base + 2 * LANES])
    kcv_ref[:, :LANES] = rope(acc[:, :LANES])
    kcv_ref[:, LANES:] = acc[:, LANES:]
    small_ref[...] = _dot(h, w_ref[:, base + 2 * LANES:base + 3 * LANES])


def _proj(x2d, g, w, ctab, s1tab, s2tab, seq, tm):
    t = x2d.shape[0]
    d = x2d.shape[1]
    ncol = w.shape[1]
    spt = seq // tm
    tab = pl.BlockSpec((tm, LANES), lambda i: (i % spt, 0))
    return pl.pallas_call(
        _proj_kernel,
        grid=(t // tm,),
        in_specs=[pl.BlockSpec((tm, d), lambda i: (i, 0)),
                  pl.BlockSpec((1, d), lambda i: (0, 0)),
                  pl.BlockSpec((d, ncol), lambda i: (0, 0)),
                  tab, tab, tab],
        out_specs=[pl.BlockSpec((tm, N_MAIN_TILES * LANES), lambda i: (i, 0)),
                   pl.BlockSpec((tm, 2 * LANES), lambda i: (i, 0)),
                   pl.BlockSpec((tm, LANES), lambda i: (i, 0))],
        out_shape=[jax.ShapeDtypeStruct((t, N_MAIN_TILES * LANES), BF16),
                   jax.ShapeDtypeStruct((t, 2 * LANES), F32),
                   jax.ShapeDtypeStruct((t, LANES), F32)],
        compiler_params=_params(1),
        name="proj",
    )(x2d, g, w, ctab, s1tab, s2tab)


def _gates_kernel(s_ref, bf_ref, ccol_ref, crow_ref):
    lf = _log_sigmoid(s_ref[...] + bf_ref[...])
    seq = lf.shape[0]
    row = lax.broadcasted_iota(jnp.int32, lf.shape, 0)
    sh = 1
    while sh < seq:
        lf = lf + jnp.where(row >= sh, pltpu.roll(lf, sh, 0), 0.0)
        sh *= 2
    ccol_ref[...] = lf
    for s in range(seq // LANES):
        blk = lf[s * LANES:(s + 1) * LANES, :].T
        crow_ref[:, s * LANES:(s + 1) * LANES] = blk[FL_LANE:FL_LANE + 8, :]


def _gates(small3, bf):
    b, seq, _ = small3.shape
    return pl.pallas_call(
        _gates_kernel,
        grid=(b,),
        in_specs=[pl.BlockSpec((None, seq, LANES), lambda i: (i, 0, 0)),
                  pl.BlockSpec((1, LANES), lambda i: (0, 0))],
        out_specs=[pl.BlockSpec((None, seq, LANES), lambda i: (i, 0, 0)),
                   pl.BlockSpec((None, 8, seq), lambda i: (i, 0, 0))],
        out_shape=[jax.ShapeDtypeStruct((b, seq, LANES), F32),
                   jax.ShapeDtypeStruct((b, 8, seq), F32)],
        compiler_params=_params(1),
        name="gates",
    )(small3, bf)


def _compress_kernel(r_ref, pea_ref, peb_ref, wa_ref, wb_ref, w2k_ref, w2vt_ref, kc_ref, vct_ref):
    r = r_ref[...]
    a = _dot((r + pea_ref[...]).astype(BF16), wa_ref[...])
    b = _dot((r + peb_ref[...]).astype(BF16), wb_ref[...])
    nrow = r.shape[0]
    hp = a + pltpu.roll(b, nrow - 1, 0)
    hid = (hp * _sigmoid(hp)).astype(BF16)
    kc_ref[...] = _dot(hid[:, :LANES], w2k_ref[...]).astype(BF16)
    vct_ref[...] = _dot_nt(w2vt_ref[...], hid[:, LANES:]).astype(BF16)


def _compress(r3, pea, peb, wa, wb, w2k, w2vt):
    b, nrow, width = r3.shape
    const = lambda i: (0, 0)
    return pl.pallas_call(
        _compress_kernel,
        grid=(b,),
        in_specs=[pl.BlockSpec((None, nrow, width), lambda i: (i, 0, 0)),
                  pl.BlockSpec((1, width), const), pl.BlockSpec((1, width), const),
                  pl.BlockSpec((width, 2 * LANES), const), pl.BlockSpec((width, 2 * LANES), const),
                  pl.BlockSpec((LANES, LANES), const), pl.BlockSpec((LANES, LANES), const)],
        out_specs=[pl.BlockSpec((None, nrow, LANES), lambda i: (i, 0, 0)),
                   pl.BlockSpec((None, LANES, nrow), lambda i: (i, 0, 0))],
        out_shape=[jax.ShapeDtypeStruct((b, nrow, LANES), BF16),
                   jax.ShapeDtypeStruct((b, LANES, nrow), BF16)],
        compiler_params=_params(1),
        name="compress",
    )(r3, pea, peb, wa, wb, w2k, w2vt)


def _nsa_cmp_kernel(q_ref, kc_ref, vct_ref, ovt_ref, ocmp_ref, nm_ref, *, qb, n_cmp, n_sel):
    qi = pl.program_id(1)
    ncp = kc_ref.shape[0]
    tq = qi * qb + lax.broadcasted_iota(jnp.int32, (ncp, qb), 1)
    nblk = lax.broadcasted_iota(jnp.int32, (ncp, qb), 0)
    cmask = (nblk * CMP_STRIDE + (CMP_BLOCK - 1) <= tq) & (nblk < n_cmp)
    row = lax.broadcasted_iota(jnp.int32, (LANES, qb), 0)
    lane = lax.broadcasted_iota(jnp.int32, (qb, LANES), 1)
    kc = kc_ref[...]
    vct = vct_ref[...]
    psum = [jnp.zeros((ncp, qb), F32), jnp.zeros((ncp, qb), F32)]
    for j in range(N_NSA_HEADS // 2):
        qt = q_ref[:, j * LANES:(j + 1) * LANES].astype(F32) * (HEAD_DIM ** -0.5)
        outs = []
        for half in range(2):
            qm = jnp.where(lane < HEAD_DIM if half == 0 else lane >= HEAD_DIM, qt, 0.0).astype(BF16)
            lt = _dot_nt(kc, qm)
            m = jnp.max(jnp.where(cmask, lt, NEG_MASK), axis=0, keepdims=True)
            p = jnp.where(cmask, jnp.exp(lt - m), 0.0)
            p = p / jnp.maximum(jnp.sum(p, axis=0, keepdims=True), 1e-30)
            psum[half] = psum[half] + p
            outs.append(_dot(vct, p.astype(BF16)))
        ot = jnp.where(row < HEAD_DIM, outs[0], outs[1])
        for s in range(qb // LANES):
            ocmp_ref[s * LANES:(s + 1) * LANES, j * LANES:(j + 1) * LANES] = (
                ot[:, s * LANES:(s + 1) * LANES].T)

    jrow = lax.broadcasted_iota(jnp.int32, (n_sel, qb), 0)
    tsel = qi * qb + lax.broadcasted_iota(jnp.int32, (n_sel, qb), 1)
    cur = tsel // SEL_BLOCK
    forced = (jrow == 0) | (jrow == cur) | (jrow == cur - 1)
    ovt = ovt_ref[...]
    for g in range(N_NSA_KV):
        hi, lo = _split2(psum[g])
        imp = _dot(ovt, hi) + _dot(ovt, lo)
        imp = jnp.where(jrow <= cur, jnp.where(forced, jnp.inf, imp), -jnp.inf)
        before = jnp.zeros((n_sel, qb), F32)
        for i in range(n_sel):
            ri = imp[i:i + 1, :]
            ahead = (ri > imp) | ((ri == imp) & (jrow > i))
            before = before + jnp.where(ahead, 1.0, 0.0)
        member = (before < min(SEL_TOPK, n_sel)) & (imp > -jnp.inf)
        not_member = jnp.where(member, 0.0, 1.0)
        padded = jnp.concatenate([not_member, jnp.zeros((LANES - n_sel, qb), F32)], axis=0)
        for s in range(qb // LANES):
            nm_ref[g, s * LANES:(s + 1) * LANES, :] = (
                padded[:, s * LANES:(s + 1) * LANES].T.astype(BF16))


def _nsa_cmp(main, kc, vct, ovt, bsz, seq, qb, n_cmp, n_sel):
    t = main.shape[0]
    nq = seq // qb
    ncp = kc.shape[1]
    kern = functools.partial(_nsa_cmp_kernel, qb=qb, n_cmp=n_cmp, n_sel=n_sel)
    return pl.pallas_call(
        kern,
        grid=(bsz, nq),
        in_specs=[pl.BlockSpec((qb, 4 * LANES), lambda b, i: (b * nq + i, 0)),
                  pl.BlockSpec((None, ncp, LANES), lambda b, i: (b, 0, 0)),
                  pl.BlockSpec((None, LANES, ncp), lambda b, i: (b, 0, 0)),
                  pl.BlockSpec((n_sel, ncp), lambda b, i: (0, 0))],
        out_specs=[pl.BlockSpec((qb, 4 * LANES), lambda b, i: (b * nq + i, 0)),
                   pl.BlockSpec((None, N_NSA_KV, qb, LANES), lambda b, i: (b, 0, i, 0))],
        out_shape=[jax.ShapeDtypeStruct((t, 4 * LANES), F32),
                   jax.ShapeDtypeStruct((bsz, N_NSA_KV, seq, LANES), BF16)],
        compiler_params=_params(2),
        name="nsa_cmp",
    )(main, kc, vct, ovt)


def _softmax_step(s, mask, vaug, carry):
    m, acc = carry
    if mask is not None:
        s = jnp.where(mask, s, NEG_MASK)
    m_new = jnp.maximum(m, jnp.max(s, axis=-1, keepdims=True))
    p = jnp.exp(s - m_new)
    acc = jnp.exp(m - m_new) * acc + _dot(p.astype(BF16), vaug)
    return m_new, acc


def _softmax_init(qb):
    return (jnp.full((qb, 1), NEG_MASK, F32), jnp.zeros((qb, 2 * LANES), F32))


def _softmax_finish(acc):
    return acc[:, :LANES] / jnp.maximum(acc[:, LANES:], 1e-30)


def _pair_tile(is_a, a, b):
    return jnp.where(is_a, a, b)


def _attn_heads(q_ref):
    qb, width = q_ref.shape
    is_a = lax.broadcasted_iota(jnp.int32, (qb, LANES), 1) < HEAD_DIM
    heads = []
    for t in range(width // LANES):
        qt = q_ref[:, t * LANES:(t + 1) * LANES].astype(F32) * (HEAD_DIM ** -0.5)
        heads.append((t, jnp.where(is_a, qt, 0.0).astype(BF16)))
        heads.append((t, jnp.where(is_a, 0.0, qt).astype(BF16)))
    return is_a, heads


def _expand(x, e):
    hi = x.astype(BF16)
    r1 = x - hi.astype(F32)
    mid = r1.astype(BF16)
    lo = (r1 - mid.astype(F32)).astype(BF16)
    return _dot(hi, e) + _dot(mid, e) + _dot(lo, e)


def _nsa_main_kernel(q_ref, ks_ref, vs_ref, kw_ref, vw_ref, nm_ref, eneg_ref, small_ref, egate_ref,
                     ocmp_ref, o_ref):
    qb = q_ref.shape[0]
    qi = pl.program_id(2)
    is_a, heads = _attn_heads(q_ref)
    nh = len(heads)
    rowi = lax.broadcasted_iota(jnp.int32, (qb, KEY_BLOCK), 0)
    coli = lax.broadcasted_iota(jnp.int32, (qb, KEY_BLOCK), 1)
    qsel = [jnp.concatenate([q, nm_ref[h % 2]], axis=1) for h, (_, q) in enumerate(heads)]
    ones_k = jnp.ones((KEY_BLOCK, LANES), BF16)

    def sel_chunk(c, carry, mask):
        off = pl.multiple_of(c * KEY_BLOCK, KEY_BLOCK)
        k = jnp.concatenate([ks_ref[pl.ds(off, KEY_BLOCK), :], eneg_ref[pl.ds(off, KEY_BLOCK), :]], axis=1)
        v = jnp.concatenate([vs_ref[pl.ds(off, KEY_BLOCK), :], ones_k], axis=1)
        return tuple(_softmax_step(_dot_nt(qsel[h], k), mask, v, carry[h]) for h in range(nh))

    init = tuple(_softmax_init(qb) for _ in range(nh))
    sel = lax.fori_loop(0, qi, lambda c, carry: sel_chunk(c, carry, None), init)
    sel = sel_chunk(qi, sel, coli <= rowi)

    span = WINDOW + SUB
    rw = lax.broadcasted_iota(jnp.int32, (SUB, span), 0)
    cw = lax.broadcasted_iota(jnp.int32, (SUB, span), 1)
    ones_w = jnp.ones((span, LANES), BF16)
    wins = [[] for _ in range(nh)]
    for r in range(qb // SUB):
        t0 = qi * qb + r * SUB
        start = pl.multiple_of(jnp.maximum(t0 - WINDOW, 0), SUB)
        diff = (t0 - start) + rw - cw
        mask = (diff >= 0) & (diff < WINDOW)
        k = kw_ref[pl.ds(start, span), :]
        v = jnp.concatenate([vw_ref[pl.ds(start, span), :], ones_w], axis=1)
        for h, (_, q) in enumerate(heads):
            s = jnp.where(mask, _dot_nt(q[r * SUB:(r + 1) * SUB], k), NEG_MASK)
            p = jnp.exp(s - jnp.max(s, axis=-1, keepdims=True))
            wins[h].append(_softmax_finish(_dot(p.astype(BF16), v)))

    sig = _sigmoid(small_ref[...])
    for t in range(nh // 2):
        lanes = slice(t * LANES, (t + 1) * LANES)
        sel_t = _pair_tile(is_a, _softmax_finish(sel[2 * t][1]), _softmax_finish(sel[2 * t + 1][1]))
        win_t = _pair_tile(is_a, jnp.concatenate(wins[2 * t], axis=0), jnp.concatenate(wins[2 * t + 1], axis=0))
        gates = _expand(sig, egate_ref[t])
        out = (gates[:, :LANES] * ocmp_ref[:, lanes] + gates[:, LANES:2 * LANES] * sel_t
               + gates[:, 2 * LANES:] * win_t)
        o_ref[:, lanes] = out.astype(BF16)


def _nsa_main(main, nm, eneg, small, egate, ocmp, bsz, seq):
    t = main.shape[0]
    qb = Q_BLOCK
    nq = seq // qb
    ntile = N_NSA_HEADS // 2
    nstep = ntile // NSA_TILES
    w = NSA_TILES * LANES
    kv = lambda tile: pl.BlockSpec((seq, LANES), lambda b, j, i: (b, tile))
    return pl.pallas_call(
        _nsa_main_kernel,
        grid=(bsz, nstep, nq),
        in_specs=[pl.BlockSpec((qb, w), lambda b, j, i: (b * nq + i, T_QN // NSA_TILES + j)),
                  kv(T_KS), kv(T_VS), kv(T_KW), kv(T_VW),
                  pl.BlockSpec((None, N_NSA_KV, qb, LANES), lambda b, j, i: (b, 0, i, 0)),
                  pl.BlockSpec((seq, LANES), lambda b, j, i: (0, 0)),
                  pl.BlockSpec((qb, LANES), lambda b, j, i: (b * nq + i, 0)),
                  pl.BlockSpec((NSA_TILES, LANES, 3 * LANES), lambda b, j, i: (j, 0, 0)),
                  pl.BlockSpec((qb, w), lambda b, j, i: (b * nq + i, j))],
        out_specs=pl.BlockSpec((qb, w), lambda b, j, i: (b * nq + i, j)),
        out_shape=jax.ShapeDtypeStruct((t, ntile * LANES), BF16),
        compiler_params=_params(3),
        name="nsa_main",
    )(main, main, main, main, main, nm, eneg, small, egate, ocmp)


def _sb_kernel(q_ref, k_ref, v_ref, u_ref, o_ref):
    qb = q_ref.shape[0]
    qi = pl.program_id(1)
    is_a, heads = _attn_heads(q_ref)
    nh = len(heads)
    rowi = lax.broadcasted_iota(jnp.int32, (qb, KEY_BLOCK), 0)
    coli = lax.broadcasted_iota(jnp.int32, (qb, KEY_BLOCK), 1)
    row_s = lax.broadcasted_iota(jnp.int32, (qb, SUB), 0)
    col_s = lax.broadcasted_iota(jnp.int32, (qb, SUB), 1)
    later = u_ref[...]
    nsub = KEY_BLOCK // SUB

    def chunk(c, carry, diag):
        off = pl.multiple_of(c * KEY_BLOCK, KEY_BLOCK)
        new = []
        for h, (t, q) in enumerate(heads):
            lanes = slice(t * LANES, (t + 1) * LANES)
            k = k_ref[pl.ds(off, KEY_BLOCK), lanes]
            v = v_ref[pl.ds(off, KEY_BLOCK), lanes]
            tail, acc = carry[h]
            z = _dot_nt(q, k)
            log_beta = _log_sigmoid(z)
            log_1m = log_beta - z
            if diag:
                log_1m = jnp.where(coli < rowi, log_1m, 0.0)
            parts = [None] * nsub
            for b in reversed(range(nsub)):
                lo_col, hi_col = b * SUB, (b + 1) * SUB
                x = log_1m[:, lo_col:hi_col]
                excl = _dot(x.astype(BF16), later)
                a = jnp.exp(log_beta[:, lo_col:hi_col] + excl + tail)
                if diag:
                    a = jnp.where(col_s + lo_col < row_s, a, 0.0)
                parts[b] = a.astype(BF16)
                tail = tail + jnp.sum(x, axis=-1, keepdims=True)
            acc = acc + _dot(jnp.concatenate(parts, axis=1), v)
            new.append((tail, acc))
        return tuple(new)

    init = tuple((jnp.zeros((qb, 1), F32), jnp.zeros((qb, LANES), F32)) for _ in range(nh))
    st = chunk(qi, init, True)
    st = lax.fori_loop(0, qi, lambda i, carry: chunk(qi - 1 - i, carry, False), st)
    for t in range(nh // 2):
        o_ref[:, t * LANES:(t + 1) * LANES] = _pair_tile(is_a, st[2 * t][1], st[2 * t + 1][1]).astype(BF16)


def _sb(main, later, bsz, seq):
    t = main.shape[0]
    qb = Q_BLOCK
    nq = seq // qb
    w = N_SB_HEADS * HEAD_DIM
    blk = lambda tile: tile * LANES // w
    return pl.pallas_call(
        _sb_kernel,
        grid=(bsz, nq),
        in_specs=[pl.BlockSpec((qb, w), lambda b, i: (b * nq + i, blk(T_QS))),
                  pl.BlockSpec((seq, w), lambda b, i: (b, blk(T_KSB))),
                  pl.BlockSpec((seq, w), lambda b, i: (b, blk(T_VSB))),
                  pl.BlockSpec((SUB, SUB), lambda b, i: (0, 0))],
        out_specs=pl.BlockSpec((qb, w), lambda b, i: (b * nq + i, 0)),
        out_shape=jax.ShapeDtypeStruct((t, w), BF16),
        compiler_params=_params(2),
        name="sb",
    )(main, main, main, later)


def _fox_kernel(q_ref, k_ref, v_ref, crow_ref, ccol_ref, esel_ref, o_ref):
    qb = q_ref.shape[0]
    qi = pl.program_id(1)
    is_a, heads = _attn_heads(q_ref)
    nh = len(heads)
    rowi = lax.broadcasted_iota(jnp.int32, (qb, KEY_BLOCK), 0)
    coli = lax.broadcasted_iota(jnp.int32, (qb, KEY_BLOCK), 1)
    ccol = ccol_ref[...]
    ct = [_expand(ccol, esel_ref[h]) for h in range(nh)]
    ones_k = jnp.ones((KEY_BLOCK, LANES), BF16)

    def chunk(c, carry, mask):
        off = pl.multiple_of(c * KEY_BLOCK, KEY_BLOCK)
        new = []
        for h, (t, q) in enumerate(heads):
            lanes = slice(t * LANES, (t + 1) * LANES)
            k = k_ref[pl.ds(off, KEY_BLOCK), lanes]
            v = jnp.concatenate([v_ref[pl.ds(off, KEY_BLOCK), lanes], ones_k], axis=1)
            cs = crow_ref[h:h + 1, pl.ds(off, KEY_BLOCK)]
            new.append(_softmax_step(_dot_nt(q, k) + ct[h] - cs, mask, v, carry[h]))
        return tuple(new)

    init = tuple(_softmax_init(qb) for _ in range(nh))
    st = lax.fori_loop(0, qi, lambda c, carry: chunk(c, carry, None), init)
    st = chunk(qi, st, coli <= rowi)
    for t in range(nh // 2):
        out = _pair_tile(is_a, _softmax_finish(st[2 * t][1]), _softmax_finish(st[2 * t + 1][1]))
        o_ref[:, t * LANES:(t + 1) * LANES] = out.astype(BF16)


def _fox(main, crow, ccol, esel, bsz, seq):
    t = main.shape[0]
    qb = Q_BLOCK
    nq = seq // qb
    w = N_FOX_HEADS * HEAD_DIM
    blk = lambda tile: tile * LANES // w
    return pl.pallas_call(
        _fox_kernel,
        grid=(bsz, nq),
        in_specs=[pl.BlockSpec((qb, w), lambda b, i: (b * nq + i, blk(T_QF))),
                  pl.BlockSpec((seq, w), lambda b, i: (b, blk(T_KF))),
                  pl.BlockSpec((seq, w), lambda b, i: (b, blk(T_VF))),
                  pl.BlockSpec((None, N_FOX_HEADS, seq), lambda b, i: (b, 0, 0)),
                  pl.BlockSpec((None, qb, LANES), lambda b, i: (b, i, 0)),
                  pl.BlockSpec((N_FOX_HEADS, LANES, KEY_BLOCK), lambda b, i: (0, 0, 0))],
        out_specs=pl.BlockSpec((qb, w), lambda b, i: (b * nq + i, 0)),
        out_shape=jax.ShapeDtypeStruct((t, w), BF16),
        compiler_params=_params(2),
        name="fox",
    )(main, main, main, crow, ccol, esel)


def _mem_kv_kernel(m_ref, g_ref, wk_ref, wv_ref, k_ref, v_ref):
    h = _rms(m_ref[...], g_ref[...]).astype(BF16)
    k_ref[...] = _dot(h, wk_ref[...]).astype(BF16)
    v_ref[...] = _dot(h, wv_ref[...]).astype(BF16)


def _mem_kv(mem, g, wk, wv):
    b, m, d = mem.shape
    mw = wk.shape[1]
    const = lambda i: (0, 0)
    return pl.pallas_call(
        _mem_kv_kernel,
        grid=(b,),
        in_specs=[pl.BlockSpec((None, m, d), lambda i: (i, 0, 0)),
                  pl.BlockSpec((1, d), const), pl.BlockSpec((d, mw), const), pl.BlockSpec((d, mw), const)],
        out_specs=[pl.BlockSpec((None, m, mw), lambda i: (i, 0, 0)),
                   pl.BlockSpec((None, m, mw), lambda i: (i, 0, 0))],
        out_shape=[jax.ShapeDtypeStruct((b, m, mw), BF16), jax.ShapeDtypeStruct((b, m, mw), BF16)],
        compiler_params=_params(1),
        name="mem_kv",
    )(mem, g, wk, wv)


def _cross_kernel(x_ref, on_ref, os_ref, of_ref, won_ref, wos_ref, wof_ref, g_ref, wq_ref,
                  km_ref, vm_ref, wmo_ref, o_ref):
    x1 = (x_ref[...] + _dot(on_ref[...], won_ref[...]) + _dot(os_ref[...], wos_ref[...])
          + _dot(of_ref[...], wof_ref[...]))
    h = _rms(x1, g_ref[...]).astype(BF16)
    q = _dot(h, wq_ref[...]) * (MEM_HEAD_DIM ** -0.5)
    tm = q.shape[0]
    lane = lax.broadcasted_iota(jnp.int32, (tm, LANES), 1)
    is_a = lane < MEM_HEAD_DIM
    tiles = []
    for p in range(q.shape[1] // LANES):
        qt = q[:, p * LANES:(p + 1) * LANES]
        k = km_ref[:, p * LANES:(p + 1) * LANES]
        v = vm_ref[:, p * LANES:(p + 1) * LANES]
        outs = []
        for half in range(2):
            qm = jnp.where(is_a if half == 0 else jnp.logical_not(is_a), qt, 0.0).astype(BF16)
            s = _dot_nt(qm, k)
            e = jnp.exp(s - jnp.max(s, axis=-1, keepdims=True))
            pr = e / jnp.sum(e, axis=-1, keepdims=True)
            outs.append(_dot(pr.astype(BF16), v))
        tiles.append(_pair_tile(is_a, outs[0], outs[1]).astype(BF16))
    attn = jnp.concatenate(tiles, axis=1)
    o_ref[...] = x1 + _dot(attn, wmo_ref[...])


def _cross(x2d, on, osb, ofx, won, wos, wof, g, wq, km, vm, wmo, seq, tm):
    t, d = x2d.shape
    mw = wq.shape[1]
    m = km.shape[1]
    spt = seq // tm
    const = lambda i: (0, 0)
    row = lambda w: pl.BlockSpec((tm, w), lambda i: (i, 0))
    return pl.pallas_call(
        _cross_kernel,
        grid=(t // tm,),
        in_specs=[row(d), row(on.shape[1]), row(osb.shape[1]), row(ofx.shape[1]),
                  pl.BlockSpec(won.shape, const), pl.BlockSpec(wos.shape, const),
                  pl.BlockSpec(wof.shape, const), pl.BlockSpec((1, d), const),
                  pl.BlockSpec((d, mw), const),
                  pl.BlockSpec((None, m, mw), lambda i: (i // spt, 0, 0)),
                  pl.BlockSpec((None, m, mw), lambda i: (i // spt, 0, 0)),
                  pl.BlockSpec((mw, d), const)],
        out_specs=row(d),
        out_shape=jax.ShapeDtypeStruct((t, d), F32),
        compiler_params=_params(1),
        name="cross",
    )(x2d, on, osb, ofx, won, wos, wof, g, wq, km, vm, wmo)


def _ffn_kernel(x_ref, g_ref, wug_ref, wuv_ref, cwg_ref, cwv_ref, cbg_ref, cbv_ref, wd_ref, o_ref,
                h_ref, acc_ref, tail_ref, *, tiles_per_seq):
    i = pl.program_id(0)
    f = pl.program_id(1)
    nf = pl.num_programs(1)

    @pl.when(f == 0)
    def _():
        h_ref[...] = _rms(x_ref[...], g_ref[...]).astype(BF16)
        acc_ref[...] = jnp.zeros_like(acc_ref)

    h = h_ref[...]
    tm = h.shape[0]
    first = (i % tiles_per_seq) == 0
    rowi = lax.broadcasted_iota(jnp.int32, (tm, wug_ref.shape[1]), 0)

    @pl.when(first)
    def _():
        tail_ref[f] = jnp.zeros(tail_ref.shape[1:], F32)

    def conv(u, cw_ref, cb_ref, kind):
        prev = tail_ref[f, kind]
        u1 = jnp.where(rowi == 0, prev[7:8], pltpu.roll(u, 1, 0))
        u2 = jnp.where(rowi == 0, prev[6:7], jnp.where(rowi == 1, prev[7:8], pltpu.roll(u, 2, 0)))
        tail_ref[f, kind] = u[tm - 8:, :]
        return cw_ref[2:3, :] * u + cw_ref[1:2, :] * u1 + cw_ref[0:1, :] * u2 + cb_ref[...]

    gate = conv(_dot(h, wug_ref[...]), cwg_ref, cbg_ref, 0)
    val = conv(_dot(h, wuv_ref[...]), cwv_ref, cbv_ref, 1)
    act = (gate * _sigmoid(gate) * val).astype(BF16)
    acc_ref[...] += _dot(act, wd_ref[...])

    @pl.when(f == nf - 1)
    def _():
        o_ref[...] = x_ref[...] + acc_ref[...]


def _ffn(x2d, g, wup, cw, cb, wdown, seq, tm, tf):
    t, d = x2d.shape
    dff = wdown.shape[0]
    nf = dff // tf
    return pl.pallas_call(
        functools.partial(_ffn_kernel, tiles_per_seq=seq // tm),
        grid=(t // tm, nf),
        in_specs=[pl.BlockSpec((tm, d), lambda i, f: (i, 0)),
                  pl.BlockSpec((1, d), lambda i, f: (0, 0)),
                  pl.BlockSpec((d, tf), lambda i, f: (0, f)),
                  pl.BlockSpec((d, tf), lambda i, f: (0, f + nf)),
                  pl.BlockSpec((CONV_WIDTH, tf), lambda i, f: (0, f)),
                  pl.BlockSpec((CONV_WIDTH, tf), lambda i, f: (0, f + nf)),
                  pl.BlockSpec((1, tf), lambda i, f: (0, f)),
                  pl.BlockSpec((1, tf), lambda i, f: (0, f + nf)),
                  pl.BlockSpec((tf, d), lambda i, f: (f, 0))],
        out_specs=pl.BlockSpec((tm, d), lambda i, f: (i, 0)),
        out_shape=jax.ShapeDtypeStruct((t, d), F32),
        scratch_shapes=[pltpu.VMEM((tm, d), BF16), pltpu.VMEM((tm, d), F32),
                        pltpu.VMEM((nf, 2, 8, tf), F32)],
        compiler_params=_params(2),
        name="ffn",
    )(x2d, g, wup, wup, cw, cw, cb, cb, wdown)


def _norm_kernel(x_ref, g_ref, o_ref):
    o_ref[...] = _rms(x_ref[...], g_ref[...])


def _final_norm(x2d, g, tm):
    t, d = x2d.shape
    return pl.pallas_call(
        _norm_kernel,
        grid=(t // tm,),
        in_specs=[pl.BlockSpec((tm, d), lambda i: (i, 0)), pl.BlockSpec((1, d), lambda i: (0, 0))],
        out_specs=pl.BlockSpec((tm, d), lambda i: (i, 0)),
        out_shape=jax.ShapeDtypeStruct((t, d), F32),
        compiler_params=_params(1),
        name="final_norm",
    )(x2d, g)


def _in_column_order():
    nq = N_NSA_HEADS * HEAD_DIM
    nkv = N_NSA_KV * HEAD_DIM
    ngate = 3 * N_NSA_HEADS
    sbw = N_SB_HEADS * HEAD_DIM
    fxw = N_FOX_HEADS * HEAD_DIM
    off = {}
    pos = 0
    for name, size in [("qn", nq), ("kc", nkv), ("vc", nkv), ("ks", nkv), ("vs", nkv), ("kw", nkv),
                       ("vw", nkv), ("gn", ngate), ("qs", sbw), ("ksb", sbw), ("vsb", sbw),
                       ("qf", fxw), ("kf", fxw), ("vf", fxw), ("fl", N_FOX_HEADS)]:
        off[name] = (pos, size)
        pos += size
    rng = lambda name: list(range(off[name][0], off[name][0] + off[name][1]))
    group = N_NSA_HEADS // N_NSA_KV
    cols = []
    for j in range(group):
        for g in range(N_NSA_KV):
            h = g * group + j
            cols += list(range(off["qn"][0] + h * HEAD_DIM, off["qn"][0] + (h + 1) * HEAD_DIM))
    for name in ("ks", "vs", "kw", "vw", "qs", "ksb", "vsb", "qf", "kf", "vf", "kc", "vc", "gn", "fl"):
        cols += rng(name)
    cols += [-1] * (LANES - ngate - N_FOX_HEADS)
    return np.asarray(cols, np.int32), pos


def _nsa_out_rows():
    group = N_NSA_HEADS // N_NSA_KV
    rows = []
    for j in range(group):
        for g in range(N_NSA_KV):
            h = g * group + j
            rows += list(range(h * HEAD_DIM, (h + 1) * HEAD_DIM))
    return np.asarray(rows, np.int32)


def _rope_tables(seq):
    half = ROPE_DIM // 2
    inv = ROPE_THETA ** (-jnp.arange(half, dtype=F32) / half)
    ang = jnp.arange(seq).astype(F32)[:, None] * inv[None, :]
    cos, sin = jnp.cos(ang), jnp.sin(ang)
    ones = jnp.ones((seq, HEAD_DIM - ROPE_DIM), F32)
    zeros = jnp.zeros((seq, HEAD_DIM - half), F32)
    ctab = jnp.concatenate([cos, cos, ones], axis=1)
    s1 = jnp.concatenate([-sin, zeros], axis=1)
    s2 = jnp.concatenate([jnp.zeros((seq, half), F32), sin, jnp.zeros((seq, HEAD_DIM - ROPE_DIM), F32)], axis=1)
    rep = LANES // HEAD_DIM
    return tuple(jnp.tile(a, (1, rep)) for a in (ctab, s1, s2))


def _overlap_t(seq, ncp):
    n_cmp = (seq - CMP_BLOCK) // CMP_STRIDE + 1
    n_sel = seq // SEL_BLOCK
    starts = np.arange(n_cmp) * CMP_STRIDE
    sel_starts = np.arange(n_sel) * SEL_BLOCK
    ov = ((starts[:, None] < sel_starts[None, :] + SEL_BLOCK)
          & (starts[:, None] + CMP_BLOCK > sel_starts[None, :])).astype(np.float32)
    out = np.zeros((n_sel, ncp), np.float32)
    out[:, :n_cmp] = ov.T
    return out, n_cmp, n_sel


def _sel_bias_table(seq, n_sel):
    tab = np.zeros((seq, LANES), np.float32)
    tab[np.arange(seq), np.arange(seq) // SEL_BLOCK] = SEL_BIAS
    return tab


def _gate_expanders():
    group = N_NSA_HEADS // N_NSA_KV
    e = np.zeros((group, LANES, 3 * LANES), np.float32)
    for j in range(group):
        for g in range(N_NSA_KV):
            h = g * group + j
            for r in range(3):
                lo = r * LANES + g * HEAD_DIM
                e[j, 3 * h + r, lo:lo + HEAD_DIM] = 1.0
    return e


def _forget_selectors():
    e = np.zeros((N_FOX_HEADS, LANES, KEY_BLOCK), np.float32)
    for h in range(N_FOX_HEADS):
        e[h, FL_LANE + h, :] = 1.0
    return e


def _blockdiag(blocks):
    n = len(blocks)
    r, c = blocks[0].shape
    out = jnp.zeros((n * r, n * c), blocks[0].dtype)
    for i, blk in enumerate(blocks):
        out = out.at[i * r:(i + 1) * r, i * c:(i + 1) * c].set(blk)
    return out


def _compress_weights(pe_k, pe_v, wk1, wk2, wv1, wv2):
    hop = CMP_STRIDE
    lblk = CMP_BLOCK

    nblk = 2 * N_NSA_KV
    stacked = jnp.stack([wk1] * N_NSA_KV + [wv1] * N_NSA_KV, axis=1)
    bd = jnp.einsum('lkdc,kj->lkdjc', stacked, jnp.eye(nblk, dtype=stacked.dtype))
    bd = bd.reshape(lblk, nblk * HEAD_DIM, nblk * HEAD_DIM).astype(BF16)
    pe = jnp.concatenate([pe_k] * N_NSA_KV + [pe_v] * N_NSA_KV, axis=1).astype(F32)

    def first_layer(lo):
        return bd[lo:lo + hop].reshape(hop * nblk * HEAD_DIM, nblk * HEAD_DIM)

    def pe_row(lo):
        return pe[lo:lo + hop].reshape(1, hop * nblk * HEAD_DIM)

    assert lblk == 2 * hop
    w2k = _blockdiag([wk2] * N_NSA_KV).astype(BF16)
    w2vt = _blockdiag([wv2.T] * N_NSA_KV).astype(BF16)
    return pe_row(0), pe_row(hop), first_layer(0), first_layer(hop), w2k, w2vt


def kernel(x, mem, norm_mix, w_in, b_forget, cmp_pe_k, cmp_pe_v, cmp_wk1, cmp_wk2, cmp_wv1, cmp_wv2, w_out, norm_cross, norm_mem, w_mq, w_mk, w_mv, w_mo, norm_ffn, w_up, conv_w, conv_b, w_down, norm_final):
    bsz, seq, d = x.shape
    depth = w_in.shape[0]
    t = bsz * seq
    dff = w_down.shape[1]
    assert seq % 512 == 0 and d % LANES == 0

    cols, n_in = _in_column_order()
    assert n_in == w_in.shape[2]
    col_ok = jnp.asarray(cols >= 0)[None, :]
    col_src = jnp.asarray(np.maximum(cols, 0))
    out_rows = jnp.asarray(_nsa_out_rows())
    ctab, s1tab, s2tab = _rope_tables(seq)
    ncp = seq // CMP_STRIDE
    ovt_np, n_cmp, n_sel = _overlap_t(seq, ncp)
    ovt = jnp.asarray(ovt_np, BF16)
    eneg = jnp.asarray(_sel_bias_table(seq, n_sel), BF16)
    later = jnp.asarray(np.tril(np.ones((SUB, SUB), np.float32), -1), BF16)
    egate = jnp.asarray(_gate_expanders(), BF16)
    esel = jnp.asarray(_forget_selectors(), BF16)
    nsa_w = N_NSA_HEADS * HEAD_DIM
    sb_w = N_SB_HEADS * HEAD_DIM
    bf_row = jnp.zeros((depth, 1, LANES), F32).at[:, 0, FL_LANE:FL_LANE + N_FOX_HEADS].set(b_forget)

    xs = x.reshape(t, d)
    for i in range(depth):
        w = jnp.where(col_ok, jnp.take(w_in[i], col_src, axis=1), 0.0).astype(BF16)
        main, kcv, small = _proj(xs, norm_mix[i][None, :], w, ctab, s1tab, s2tab, seq, ROW_TILE)

        ccol, cum = _gates(small.reshape(bsz, seq, LANES), bf_row[i])
        crow = cum[:, :N_FOX_HEADS]

        pea, peb, wa, wb, w2k, w2vt = _compress_weights(
            cmp_pe_k[i], cmp_pe_v[i], cmp_wk1[i], cmp_wk2[i], cmp_wv1[i], cmp_wv2[i])
        kc, vct = _compress(kcv.reshape(bsz, ncp, CMP_STRIDE * 2 * LANES), pea, peb, wa, wb, w2k, w2vt)

        ocmp, nm = _nsa_cmp(main, kc, vct, ovt, bsz, seq, CMP_Q_BLOCK, n_cmp, n_sel)
        o_nsa = _nsa_main(main, nm, eneg, small, egate, ocmp, bsz, seq)
        o_sb = _sb(main, later, bsz, seq)
        o_fox = _fox(main, crow, ccol, esel, bsz, seq)

        km, vm = _mem_kv(mem, norm_mem[i][None, :], w_mk[i].astype(BF16), w_mv[i].astype(BF16))
        wo = w_out[i]
        xs = _cross(xs, o_nsa, o_sb, o_fox,
                    jnp.take(wo[:nsa_w], out_rows, axis=0).astype(BF16),
                    wo[nsa_w:nsa_w + sb_w].astype(BF16), wo[nsa_w + sb_w:].astype(BF16),
                    norm_cross[i][None, :], w_mq[i].astype(BF16), km, vm, w_mo[i].astype(BF16), seq, ROW_TILE)
        xs = _ffn(xs, norm_ffn[i][None, :], w_up[i].astype(BF16), conv_w[i], conv_b[i][None, :],
                  w_down[i].astype(BF16), seq, ROW_TILE, FFN_TF)
    return _final_norm(xs, norm_final[None, :], 2 * ROW_TILE).reshape(bsz, seq, d)
```

```python
import functools

import numpy as np
import jax
import jax.numpy as jnp
from jax import lax
from jax.experimental import pallas as pl
from jax.experimental.pallas import tpu as pltpu

N_NSA_HEADS = 8
N_NSA_KV = 2
N_SB_HEADS = 4
N_FOX_HEADS = 4
HEAD_DIM = 64
ROPE_DIM = 16
ROPE_THETA = 500000.0
CMP_BLOCK = 32
CMP_STRIDE = 16
SEL_BLOCK = 64
SEL_TOPK = 16
WINDOW = 512
MEM_HEAD_DIM = 64
CONV_WIDTH = 3
EPS = 1e-6

LANES = 128
Q_BLOCK = 512
KEY_BLOCK = 512
SUB = 128
FL_LANE = 3 * N_NSA_HEADS
FL_COPIES = 6
ROW_TILE = 512
CROSS_TILE = 1024
FFN_TF = 1408
CMP_Q_BLOCK = 256
NSA_TILES = 4
NEG_MASK = -1e30
SEL_BIAS = -(2.0 ** 30)
VMEM_LIMIT = 48 * 1024 * 1024

F32 = jnp.float32
BF16 = jnp.bfloat16

T_QN, T_KS, T_VS, T_KW, T_VW = 0, 4, 5, 6, 7
T_QS, T_KSB, T_VSB = 8, 10, 12
T_QF, T_KF, T_VF = 14, 16, 18
N_MAIN_TILES = 20
ROPE_MAIN_TILES = (0, 1, 2, 3, T_KS, T_KW)

_NT = (((1,), (1,)), ((), ()))


def _params(n_grid):
    return pltpu.CompilerParams(dimension_semantics=("arbitrary",) * n_grid,
                                vmem_limit_bytes=VMEM_LIMIT)


def _rms(xf, g):
    return xf * lax.rsqrt(jnp.mean(xf * xf, axis=-1, keepdims=True) + EPS) * g


def _sigmoid(x):
    return 1.0 / (1.0 + jnp.exp(-x))


def _log_sigmoid(x):
    return jnp.minimum(x, 0.0) - jnp.log(1.0 + jnp.exp(-jnp.abs(x)))


def _dot(a, b):
    return jnp.dot(a, b, preferred_element_type=F32)


def _dot_nt(a, b):
    return lax.dot_general(a, b, _NT, preferred_element_type=F32)


def _split2(x):
    hi = x.astype(BF16)
    lo = (x - hi.astype(F32)).astype(BF16)
    return hi, lo


def _proj_kernel(x_ref, g_ref, w_ref, c_ref, s1_ref, s2_ref, main_ref, kcv_ref, small_ref):
    h = _rms(x_ref[...], g_ref[...]).astype(BF16)
    cos = c_ref[...]
    sin_lo = s1_ref[...]
    sin_hi = s2_ref[...]

    def rope(a):
        return a * cos + pltpu.roll(a, LANES - 8, 1) * sin_lo + pltpu.roll(a, 8, 1) * sin_hi

    for c in range(N_MAIN_TILES // 2):
        acc = _dot(h, w_ref[:, 2 * c * LANES:(2 * c + 2) * LANES])
        for k in range(2):
            t = 2 * c + k
            a = acc[:, k * LANES:(k + 1) * LANES]
            if t in ROPE_MAIN_TILES:
                a = rope(a)
            main_ref[:, t * LANES:(t + 1) * LANES] = a.astype(BF16)
    base = N_MAIN_TILES * LANES
    acc = _dot(h, w_ref[:, base:base + 2 * LANES])
    kcv_ref[:, :LANES] = rope(acc[:, :LANES])
    kcv_ref[:, LANES:] = acc[:, LANES:]
    small_ref[...] = _dot(h, w_ref[:, base + 2 * LANES:base + 3 * LANES])


def _proj(x2d, g, w, ctab, s1tab, s2tab, seq, tm):
    t = x2d.shape[0]
    d = x2d.shape[1]
    ncol = w.shape[1]
    spt = seq // tm
    tab = pl.BlockSpec((tm, LANES), lambda i: (i % spt, 0))
    return pl.pallas_call(
        _proj_kernel,
        grid=(t // tm,),
        in_specs=[pl.BlockSpec((tm, d), lambda i: (i, 0)),
                  pl.BlockSpec((1, d), lambda i: (0, 0)),
                  pl.BlockSpec((d, ncol), lambda i: (0, 0)),
                  tab, tab, tab],
        out_specs=[pl.BlockSpec((tm, N_MAIN_TILES * LANES), lambda i: (i, 0)),
                   pl.BlockSpec((tm, 2 * LANES), lambda i: (i, 0)),
                   pl.BlockSpec((tm, LANES), lambda i: (i, 0))],
        out_shape=[jax.ShapeDtypeStruct((t, N_MAIN_TILES * LANES), BF16),
                   jax.ShapeDtypeStruct((t, 2 * LANES), F32),
                   jax.ShapeDtypeStruct((t, LANES), F32)],
        compiler_params=_params(1),
        name="proj",
    )(x2d, g, w, ctab, s1tab, s2tab)


def _gates_kernel(s_ref, bf_ref, qa_ref, ka_ref):
    lf = _log_sigmoid(s_ref[...] + bf_ref[...])
    seq = lf.shape[0]
    row = lax.broadcasted_iota(jnp.int32, lf.shape, 0)
    sh = 1
    while sh < seq:
        lf = lf + jnp.where(row >= sh, pltpu.roll(lf, sh, 0), 0.0)
        sh *= 2
    hi = lf.astype(BF16).astype(F32)
    r1 = lf - hi
    mid = r1.astype(BF16).astype(F32)
    lo = (r1 - mid).astype(BF16).astype(F32)
    lane = lax.broadcasted_iota(jnp.int32, lf.shape, 1)

    def group(i):
        return (lane >= FL_LANE + i * N_FOX_HEADS) & (lane < FL_LANE + (i + 1) * N_FOX_HEADS)

    piece = jnp.where(group(0) | group(3), hi, jnp.where(group(1) | group(4), mid, lo))
    in_t = group(0) | group(1) | group(2)
    in_s = group(3) | group(4) | group(5)
    qa_ref[...] = jnp.where(in_t, piece, jnp.where(in_s, 1.0, 0.0)).astype(BF16)
    ka_ref[...] = jnp.where(in_t, 1.0, jnp.where(in_s, -piece, 0.0)).astype(BF16)


def _gates(small3, bf):
    b, seq, _ = small3.shape
    return pl.pallas_call(
        _gates_kernel,
        grid=(b,),
        in_specs=[pl.BlockSpec((None, seq, LANES), lambda i: (i, 0, 0)),
                  pl.BlockSpec((1, LANES), lambda i: (0, 0))],
        out_specs=[pl.BlockSpec((None, seq, LANES), lambda i: (i, 0, 0)),
                   pl.BlockSpec((None, seq, LANES), lambda i: (i, 0, 0))],
        out_shape=[jax.ShapeDtypeStruct((b, seq, LANES), BF16),
                   jax.ShapeDtypeStruct((b, seq, LANES), BF16)],
        compiler_params=_params(1),
        name="gates",
    )(small3, bf)


def _compress_kernel(r_ref, pea_ref, peb_ref, wa_ref, wb_ref, w2k_ref, w2vt_ref, kc_ref, vct_ref):
    r = r_ref[...]
    a = _dot((r + pea_ref[...]).astype(BF16), wa_ref[...])
    b = _dot((r + peb_ref[...]).astype(BF16), wb_ref[...])
    nrow = r.shape[0]
    hp = a + pltpu.roll(b, nrow - 1, 0)
    hid = (hp * _sigmoid(hp)).astype(BF16)
    kc_ref[...] = _dot(hid[:, :LANES], w2k_ref[...]).astype(BF16)
    vct_ref[...] = _dot_nt(w2vt_ref[...], hid[:, LANES:]).astype(BF16)


def _compress(r3, pea, peb, wa, wb, w2k, w2vt):
    b, nrow, width = r3.shape
    const = lambda i: (0, 0)
    return pl.pallas_call(
        _compress_kernel,
        grid=(b,),
        in_specs=[pl.BlockSpec((None, nrow, width), lambda i: (i, 0, 0)),
                  pl.BlockSpec((1, width), const), pl.BlockSpec((1, width), const),
                  pl.BlockSpec((width, 2 * LANES), const), pl.BlockSpec((width, 2 * LANES), const),
                  pl.BlockSpec((LANES, LANES), const), pl.BlockSpec((LANES, LANES), const)],
        out_specs=[pl.BlockSpec((None, nrow, LANES), lambda i: (i, 0, 0)),
                   pl.BlockSpec((None, LANES, nrow), lambda i: (i, 0, 0))],
        out_shape=[jax.ShapeDtypeStruct((b, nrow, LANES), BF16),
                   jax.ShapeDtypeStruct((b, LANES, nrow), BF16)],
        compiler_params=_params(1),
        name="compress",
    )(r3, pea, peb, wa, wb, w2k, w2vt)


def _nsa_cmp_kernel(q_ref, kc_ref, vct_ref, ovt_ref, ocmp_ref, nm_ref, *, qb, n_cmp, n_sel):
    qi = pl.program_id(1)
    ncp = kc_ref.shape[0]
    tq = qi * qb + lax.broadcasted_iota(jnp.int32, (ncp, qb), 1)
    nblk = lax.broadcasted_iota(jnp.int32, (ncp, qb), 0)
    cmask = (nblk * CMP_STRIDE + (CMP_BLOCK - 1) <= tq) & (nblk < n_cmp)
    row = lax.broadcasted_iota(jnp.int32, (LANES, qb), 0)
    lane = lax.broadcasted_iota(jnp.int32, (qb, LANES), 1)
    kc = kc_ref[...]
    vct = vct_ref[...]
    psum = [jnp.zeros((ncp, qb), F32), jnp.zeros((ncp, qb), F32)]
    for j in range(N_NSA_HEADS // 2):
        qt = q_ref[:, j * LANES:(j + 1) * LANES].astype(F32) * (HEAD_DIM ** -0.5)
        outs = []
        for half in range(2):
            qm = jnp.where(lane < HEAD_DIM if half == 0 else lane >= HEAD_DIM, qt, 0.0).astype(BF16)
            lt = _dot_nt(kc, qm)
            m = jnp.max(jnp.where(cmask, lt, NEG_MASK), axis=0, keepdims=True)
            p = jnp.where(cmask, jnp.exp(lt - m), 0.0)
            p = p / jnp.maximum(jnp.sum(p, axis=0, keepdims=True), 1e-30)
            psum[half] = psum[half] + p
            outs.append(_dot(vct, p.astype(BF16)))
        ot = jnp.where(row < HEAD_DIM, outs[0], outs[1])
        for s in range(qb // LANES):
            ocmp_ref[s * LANES:(s + 1) * LANES, j * LANES:(j + 1) * LANES] = (
                ot[:, s * LANES:(s + 1) * LANES].T)

    jrow = lax.broadcasted_iota(jnp.int32, (n_sel, qb), 0)
    tsel = qi * qb + lax.broadcasted_iota(jnp.int32, (n_sel, qb), 1)
    cur = tsel // SEL_BLOCK
    forced = (jrow == 0) | (jrow == cur) | (jrow == cur - 1)
    ovt = ovt_ref[...]
    for g in range(N_NSA_KV):
        hi, lo = _split2(psum[g])
        imp = _dot(ovt, hi) + _dot(ovt, lo)
        imp = jnp.where(jrow <= cur, jnp.where(forced, jnp.inf, imp), -jnp.inf)
        before = jnp.zeros((n_sel, qb), F32)
        for i in range(n_sel):
            ri = imp[i:i + 1, :]
            ahead = (ri > imp) | ((ri == imp) & (jrow > i))
            before = before + jnp.where(ahead, 1.0, 0.0)
        member = (before < min(SEL_TOPK, n_sel)) & (imp > -jnp.inf)
        not_member = jnp.where(member, 0.0, 1.0)
        padded = jnp.concatenate([not_member, jnp.zeros((LANES - n_sel, qb), F32)], axis=0)
        for s in range(qb // LANES):
            nm_ref[g, s * LANES:(s + 1) * LANES, :] = (
                padded[:, s * LANES:(s + 1) * LANES].T.astype(BF16))


def _nsa_cmp(main, kc, vct, ovt, bsz, seq, qb, n_cmp, n_sel):
    t = main.shape[0]
    nq = seq // qb
    ncp = kc.shape[1]
    kern = functools.partial(_nsa_cmp_kernel, qb=qb, n_cmp=n_cmp, n_sel=n_sel)
    return pl.pallas_call(
        kern,
        grid=(bsz, nq),
        in_specs=[pl.BlockSpec((qb, 4 * LANES), lambda b, i: (b * nq + i, 0)),
                  pl.BlockSpec((None, ncp, LANES), lambda b, i: (b, 0, 0)),
                  pl.BlockSpec((None, LANES, ncp), lambda b, i: (b, 0, 0)),
                  pl.BlockSpec((n_sel, ncp), lambda b, i: (0, 0))],
        out_specs=[pl.BlockSpec((qb, 4 * LANES), lambda b, i: (b * nq + i, 0)),
                   pl.BlockSpec((None, N_NSA_KV, qb, LANES), lambda b, i: (b, 0, i, 0))],
        out_shape=[jax.ShapeDtypeStruct((t, 4 * LANES), F32),
                   jax.ShapeDtypeStruct((bsz, N_NSA_KV, seq, LANES), BF16)],
        compiler_params=_params(2),
        name="nsa_cmp",
    )(main, kc, vct, ovt)


def _softmax_step(s, mask, vaug, carry):
    m, acc = carry
    if mask is not None:
        s = jnp.where(mask, s, NEG_MASK)
    m_new = jnp.maximum(m, jnp.max(s, axis=-1, keepdims=True))
    p = jnp.exp(s - m_new)
    acc = jnp.exp(m - m_new) * acc + _dot(p.astype(BF16), vaug)
    return m_new, acc


def _softmax_init(qb):
    return (jnp.full((qb, 1), NEG_MASK, F32), jnp.zeros((qb, 2 * LANES), F32))


def _softmax_finish(acc):
    return acc[:, :LANES] / jnp.maximum(acc[:, LANES:], 1e-30)


def _pair_tile(is_a, a, b):
    return jnp.where(is_a, a, b)


def _attn_heads(q_ref):
    qb, width = q_ref.shape
    is_a = lax.broadcasted_iota(jnp.int32, (qb, LANES), 1) < HEAD_DIM
    heads = []
    for t in range(width // LANES):
        qt = q_ref[:, t * LANES:(t + 1) * LANES].astype(F32) * (HEAD_DIM ** -0.5)
        heads.append((t, jnp.where(is_a, qt, 0.0).astype(BF16)))
        heads.append((t, jnp.where(is_a, 0.0, qt).astype(BF16)))
    return is_a, heads


def _expand(x, e):
    hi = x.astype(BF16)
    r1 = x - hi.astype(F32)
    mid = r1.astype(BF16)
    lo = (r1 - mid.astype(F32)).astype(BF16)
    return _dot(hi, e) + _dot(mid, e) + _dot(lo, e)


def _nsa_main_kernel(q_ref, ks_ref, vs_ref, kw_ref, vw_ref, nm_ref, eneg_ref, small_ref, egate_ref,
                     ocmp_ref, o_ref):
    qb = q_ref.shape[0]
    qi = pl.program_id(2)
    is_a, heads = _attn_heads(q_ref)
    nh = len(heads)
    rowi = lax.broadcasted_iota(jnp.int32, (qb, KEY_BLOCK), 0)
    coli = lax.broadcasted_iota(jnp.int32, (qb, KEY_BLOCK), 1)
    qsel = [jnp.concatenate([q, nm_ref[h % 2]], axis=1) for h, (_, q) in enumerate(heads)]
    ones_k = jnp.ones((KEY_BLOCK, LANES), BF16)

    def sel_chunk(c, carry, mask):
        off = pl.multiple_of(c * KEY_BLOCK, KEY_BLOCK)
        k = jnp.concatenate([ks_ref[pl.ds(off, KEY_BLOCK), :], eneg_ref[pl.ds(off, KEY_BLOCK), :]], axis=1)
        v = jnp.concatenate([vs_ref[pl.ds(off, KEY_BLOCK), :], ones_k], axis=1)
        return tuple(_softmax_step(_dot_nt(qsel[h], k), mask, v, carry[h]) for h in range(nh))

    init = tuple(_softmax_init(qb) for _ in range(nh))
    sel = lax.fori_loop(0, qi, lambda c, carry: sel_chunk(c, carry, None), init)
    sel = sel_chunk(qi, sel, coli <= rowi)

    span = WINDOW + SUB
    rw = lax.broadcasted_iota(jnp.int32, (SUB, span), 0)
    cw = lax.broadcasted_iota(jnp.int32, (SUB, span), 1)
    ones_w = jnp.ones((span, LANES), BF16)
    wins = [[] for _ in range(nh)]
    for r in range(qb // SUB):
        t0 = qi * qb + r * SUB
        start = pl.multiple_of(jnp.maximum(t0 - WINDOW, 0), SUB)
        diff = (t0 - start) + rw - cw
        mask = (diff >= 0) & (diff < WINDOW)
        k = kw_ref[pl.ds(start, span), :]
        v = jnp.concatenate([vw_ref[pl.ds(start, span), :], ones_w], axis=1)
        for h, (_, q) in enumerate(heads):
            s = jnp.where(mask, _dot_nt(q[r * SUB:(r + 1) * SUB], k), NEG_MASK)
            p = jnp.exp(s - jnp.max(s, axis=-1, keepdims=True))
            wins[h].append(_softmax_finish(_dot(p.astype(BF16), v)))

    sig = _sigmoid(small_ref[...])
    for t in range(nh // 2):
        lanes = slice(t * LANES, (t + 1) * LANES)
        sel_t = _pair_tile(is_a, _softmax_finish(sel[2 * t][1]), _softmax_finish(sel[2 * t + 1][1]))
        win_t = _pair_tile(is_a, jnp.concatenate(wins[2 * t], axis=0), jnp.concatenate(wins[2 * t + 1], axis=0))
        gates = _expand(sig, egate_ref[t])
        out = (gates[:, :LANES] * ocmp_ref[:, lanes] + gates[:, LANES:2 * LANES] * sel_t
               + gates[:, 2 * LANES:] * win_t)
        o_ref[:, lanes] = out.astype(BF16)


def _nsa_main(main, nm, eneg, small, egate, ocmp, bsz, seq):
    t = main.shape[0]
    qb = Q_BLOCK
    nq = seq // qb
    ntile = N_NSA_HEADS // 2
    nstep = ntile // NSA_TILES
    w = NSA_TILES * LANES
    kv = lambda tile: pl.BlockSpec((seq, LANES), lambda b, j, i: (b, tile))
    return pl.pallas_call(
        _nsa_main_kernel,
        grid=(bsz, nstep, nq),
        in_specs=[pl.BlockSpec((qb, w), lambda b, j, i: (b * nq + i, T_QN // NSA_TILES + j)),
                  kv(T_KS), kv(T_VS), kv(T_KW), kv(T_VW),
                  pl.BlockSpec((None, N_NSA_KV, qb, LANES), lambda b, j, i: (b, 0, i, 0)),
                  pl.BlockSpec((seq, LANES), lambda b, j, i: (0, 0)),
                  pl.BlockSpec((qb, LANES), lambda b, j, i: (b * nq + i, 0)),
                  pl.BlockSpec((NSA_TILES, LANES, 3 * LANES), lambda b, j, i: (j, 0, 0)),
                  pl.BlockSpec((qb, w), lambda b, j, i: (b * nq + i, j))],
        out_specs=pl.BlockSpec((qb, w), lambda b, j, i: (b * nq + i, j)),
        out_shape=jax.ShapeDtypeStruct((t, ntile * LANES), BF16),
        compiler_params=_params(3),
        name="nsa_main",
    )(main, main, main, main, main, nm, eneg, small, egate, ocmp)


def _sb_kernel(q_ref, k_ref, v_ref, u_ref, o_ref):
    qb = q_ref.shape[0]
    qi = pl.program_id(1)
    is_a, heads = _attn_heads(q_ref)
    nh = len(heads)
    rowi = lax.broadcasted_iota(jnp.int32, (qb, KEY_BLOCK), 0)
    coli = lax.broadcasted_iota(jnp.int32, (qb, KEY_BLOCK), 1)
    row_s = lax.broadcasted_iota(jnp.int32, (qb, SUB), 0)
    col_s = lax.broadcasted_iota(jnp.int32, (qb, SUB), 1)
    later = u_ref[...]
    nsub = KEY_BLOCK // SUB

    def chunk(c, carry, diag):
        off = pl.multiple_of(c * KEY_BLOCK, KEY_BLOCK)
        new = []
        for h, (t, q) in enumerate(heads):
            lanes = slice(t * LANES, (t + 1) * LANES)
            k = k_ref[pl.ds(off, KEY_BLOCK), lanes]
            v = v_ref[pl.ds(off, KEY_BLOCK), lanes]
            tail, acc = carry[h]
            z = _dot_nt(q, k)
            log_beta = _log_sigmoid(z)
            log_1m = log_beta - z
            if diag:
                log_1m = jnp.where(coli < rowi, log_1m, 0.0)
            parts = [None] * nsub
            for b in reversed(range(nsub)):
                lo_col, hi_col = b * SUB, (b + 1) * SUB
                x = log_1m[:, lo_col:hi_col]
                excl = _dot(x.astype(BF16), later)
                a = jnp.exp(log_beta[:, lo_col:hi_col] + excl + tail)
                if diag:
                    a = jnp.where(col_s + lo_col < row_s, a, 0.0)
                parts[b] = a.astype(BF16)
                tail = tail + jnp.sum(x, axis=-1, keepdims=True)
            acc = acc + _dot(jnp.concatenate(parts, axis=1), v)
            new.append((tail, acc))
        return tuple(new)

    init = tuple((jnp.zeros((qb, 1), F32), jnp.zeros((qb, LANES), F32)) for _ in range(nh))
    st = chunk(qi, init, True)
    st = lax.fori_loop(0, qi, lambda i, carry: chunk(qi - 1 - i, carry, False), st)
    for t in range(nh // 2):
        o_ref[:, t * LANES:(t + 1) * LANES] = _pair_tile(is_a, st[2 * t][1], st[2 * t + 1][1]).astype(BF16)


def _sb(main, later, bsz, seq):
    t = main.shape[0]
    qb = Q_BLOCK
    nq = seq // qb
    w = N_SB_HEADS * HEAD_DIM
    blk = lambda tile: tile * LANES // w
    return pl.pallas_call(
        _sb_kernel,
        grid=(bsz, nq),
        in_specs=[pl.BlockSpec((qb, w), lambda b, i: (b * nq + i, blk(T_QS))),
                  pl.BlockSpec((seq, w), lambda b, i: (b, blk(T_KSB))),
                  pl.BlockSpec((seq, w), lambda b, i: (b, blk(T_VSB))),
                  pl.BlockSpec((SUB, SUB), lambda b, i: (0, 0))],
        out_specs=pl.BlockSpec((qb, w), lambda b, i: (b * nq + i, 0)),
        out_shape=jax.ShapeDtypeStruct((t, w), BF16),
        compiler_params=_params(2),
        name="sb",
    )(main, main, main, later)


def _fox_kernel(q_ref, k_ref, v_ref, qa_ref, ka_ref, o_ref):
    qb = q_ref.shape[0]
    qi = pl.program_id(1)
    is_a, heads = _attn_heads(q_ref)
    nh = len(heads)
    rowi = lax.broadcasted_iota(jnp.int32, (qb, KEY_BLOCK), 0)
    coli = lax.broadcasted_iota(jnp.int32, (qb, KEY_BLOCK), 1)
    lane = lax.broadcasted_iota(jnp.int32, (qb, LANES), 1)
    qa = qa_ref[...]
    qaug = []
    for h, (_, q) in enumerate(heads):
        assert FL_LANE % N_FOX_HEADS == 0 and N_FOX_HEADS & (N_FOX_HEADS - 1) == 0
        mine = ((lane >= FL_LANE) & (lane < FL_LANE + FL_COPIES * N_FOX_HEADS)
                & ((lane & (N_FOX_HEADS - 1)) == h))
        qaug.append(jnp.concatenate([q, jnp.where(mine, qa, jnp.zeros_like(qa))], axis=1))
    ones_k = jnp.ones((KEY_BLOCK, LANES), BF16)

    def chunk(c, carry, mask):
        off = pl.multiple_of(c * KEY_BLOCK, KEY_BLOCK)
        ka = ka_ref[pl.ds(off, KEY_BLOCK), :]
        new = []
        for h, (t, _) in enumerate(heads):
            lanes = slice(t * LANES, (t + 1) * LANES)
            k = jnp.concatenate([k_ref[pl.ds(off, KEY_BLOCK), lanes], ka], axis=1)
            v = jnp.concatenate([v_ref[pl.ds(off, KEY_BLOCK), lanes], ones_k], axis=1)
            new.append(_softmax_step(_dot_nt(qaug[h], k), mask, v, carry[h]))
        return tuple(new)

    init = tuple(_softmax_init(qb) for _ in range(nh))
    st = lax.fori_loop(0, qi, lambda c, carry: chunk(c, carry, None), init)
    st = chunk(qi, st, coli <= rowi)
    for t in range(nh // 2):
        out = _pair_tile(is_a, _softmax_finish(st[2 * t][1]), _softmax_finish(st[2 * t + 1][1]))
        o_ref[:, t * LANES:(t + 1) * LANES] = out.astype(BF16)


def _fox(main, qa, ka, bsz, seq):
    t = main.shape[0]
    qb = Q_BLOCK
    nq = seq // qb
    w = N_FOX_HEADS * HEAD_DIM
    blk = lambda tile: tile * LANES // w
    return pl.pallas_call(
        _fox_kernel,
        grid=(bsz, nq),
        in_specs=[pl.BlockSpec((qb, w), lambda b, i: (b * nq + i, blk(T_QF))),
                  pl.BlockSpec((seq, w), lambda b, i: (b, blk(T_KF))),
                  pl.BlockSpec((seq, w), lambda b, i: (b, blk(T_VF))),
                  pl.BlockSpec((None, qb, LANES), lambda b, i: (b, i, 0)),
                  pl.BlockSpec((None, seq, LANES), lambda b, i: (b, 0, 0))],
        out_specs=pl.BlockSpec((qb, w), lambda b, i: (b * nq + i, 0)),
        out_shape=jax.ShapeDtypeStruct((t, w), BF16),
        compiler_params=_params(2),
        name="fox",
    )(main, main, main, qa, ka)


def _mem_kv_kernel(m_ref, g_ref, wk_ref, wv_ref, k_ref, v_ref):
    h = _rms(m_ref[...], g_ref[...]).astype(BF16)
    k_ref[...] = _dot(h, wk_ref[...]).astype(BF16)
    v_ref[...] = _dot(h, wv_ref[...]).astype(BF16)


def _mem_kv(mem, g, wk, wv):
    b, m, d = mem.shape
    mw = wk.shape[1]
    const = lambda i: (0, 0)
    return pl.pallas_call(
        _mem_kv_kernel,
        grid=(b,),
        in_specs=[pl.BlockSpec((None, m, d), lambda i: (i, 0, 0)),
                  pl.BlockSpec((1, d), const), pl.BlockSpec((d, mw), const), pl.BlockSpec((d, mw), const)],
        out_specs=[pl.BlockSpec((None, m, mw), lambda i: (i, 0, 0)),
                   pl.BlockSpec((None, m, mw), lambda i: (i, 0, 0))],
        out_shape=[jax.ShapeDtypeStruct((b, m, mw), BF16), jax.ShapeDtypeStruct((b, m, mw), BF16)],
        compiler_params=_params(1),
        name="mem_kv",
    )(mem, g, wk, wv)


def _cross_kernel(x_ref, on_ref, os_ref, of_ref, won_ref, wos_ref, wof_ref, g_ref, wq_ref,
                  km_ref, vm_ref, wmo_ref, o_ref):
    x1 = (x_ref[...] + _dot(on_ref[...], won_ref[...]) + _dot(os_ref[...], wos_ref[...])
          + _dot(of_ref[...], wof_ref[...]))
    h = _rms(x1, g_ref[...]).astype(BF16)
    q = _dot(h, wq_ref[...]) * (MEM_HEAD_DIM ** -0.5)
    tm = q.shape[0]
    lane = lax.broadcasted_iota(jnp.int32, (tm, LANES), 1)
    is_a = lane < MEM_HEAD_DIM
    tiles = []
    for p in range(q.shape[1] // LANES):
        qt = q[:, p * LANES:(p + 1) * LANES]
        k = km_ref[:, p * LANES:(p + 1) * LANES]
        v = vm_ref[:, p * LANES:(p + 1) * LANES]
        outs = []
        for half in range(2):
            qm = jnp.where(is_a if half == 0 else jnp.logical_not(is_a), qt, 0.0).astype(BF16)
            s = _dot_nt(qm, k)
            e = jnp.exp(s - jnp.max(s, axis=-1, keepdims=True))
            pr = e / jnp.sum(e, axis=-1, keepdims=True)
            outs.append(_dot(pr.astype(BF16), v))
        tiles.append(_pair_tile(is_a, outs[0], outs[1]).astype(BF16))
    attn = jnp.concatenate(tiles, axis=1)
    o_ref[...] = x1 + _dot(attn, wmo_ref[...])


def _cross(x2d, on, osb, ofx, won, wos, wof, g, wq, km, vm, wmo, seq, tm):
    t, d = x2d.shape
    mw = wq.shape[1]
    m = km.shape[1]
    spt = seq // tm
    const = lambda i: (0, 0)
    row = lambda w: pl.BlockSpec((tm, w), lambda i: (i, 0))
    return pl.pallas_call(
        _cross_kernel,
        grid=(t // tm,),
        in_specs=[row(d), row(on.shape[1]), row(osb.shape[1]), row(ofx.shape[1]),
                  pl.BlockSpec(won.shape, const), pl.BlockSpec(wos.shape, const),
                  pl.BlockSpec(wof.shape, const), pl.BlockSpec((1, d), const),
                  pl.BlockSpec((d, mw), const),
                  pl.BlockSpec((None, m, mw), lambda i: (i // spt, 0, 0)),
                  pl.BlockSpec((None, m, mw), lambda i: (i // spt, 0, 0)),
                  pl.BlockSpec((mw, d), const)],
        out_specs=row(d),
        out_shape=jax.ShapeDtypeStruct((t, d), F32),
        compiler_params=_params(1),
        name="cross",
    )(x2d, on, osb, ofx, won, wos, wof, g, wq, km, vm, wmo)


def _ffn_kernel(x_ref, g_ref, wug_ref, wuv_ref, cwg_ref, cwv_ref, cbg_ref, cbv_ref, wd_ref, gout_ref,
                o_ref, h_ref, acc_ref, tail_ref, *, tiles_per_seq, norm_out):
    i = pl.program_id(0)
    f = pl.program_id(1)
    nf = pl.num_programs(1)

    @pl.when(f == 0)
    def _():
        h_ref[...] = _rms(x_ref[...], g_ref[...]).astype(BF16)
        acc_ref[...] = jnp.zeros_like(acc_ref)

    h = h_ref[...]
    tm = h.shape[0]
    first = (i % tiles_per_seq) == 0
    rowi = lax.broadcasted_iota(jnp.int32, (tm, wug_ref.shape[1]), 0)

    @pl.when(first)
    def _():
        tail_ref[f] = jnp.zeros(tail_ref.shape[1:], F32)

    def conv(u, cw_ref, cb_ref, kind):
        prev = tail_ref[f, kind]
        u1 = jnp.where(rowi == 0, prev[7:8], pltpu.roll(u, 1, 0))
        u2 = jnp.where(rowi == 0, prev[6:7], jnp.where(rowi == 1, prev[7:8], pltpu.roll(u, 2, 0)))
        tail_ref[f, kind] = u[tm - 8:, :]
        return cw_ref[2:3, :] * u + cw_ref[1:2, :] * u1 + cw_ref[0:1, :] * u2 + cb_ref[...]

    gate = conv(_dot(h, wug_ref[...]), cwg_ref, cbg_ref, 0)
    val = conv(_dot(h, wuv_ref[...]), cwv_ref, cbv_ref, 1)
    act = (gate * _sigmoid(gate) * val).astype(BF16)
    acc_ref[...] += _dot(act, wd_ref[...])

    @pl.when(f == nf - 1)
    def _():
        y = x_ref[...] + acc_ref[...]
        o_ref[...] = _rms(y, gout_ref[...]) if norm_out else y


def _ffn(x2d, g, wup, cw, cb, wdown, gout, seq, tm, tf, norm_out):
    t, d = x2d.shape
    dff = wdown.shape[0]
    nf = dff // tf
    return pl.pallas_call(
        functools.partial(_ffn_kernel, tiles_per_seq=seq // tm, norm_out=norm_out),
        grid=(t // tm, nf),
        in_specs=[pl.BlockSpec((tm, d), lambda i, f: (i, 0)),
                  pl.BlockSpec((1, d), lambda i, f: (0, 0)),
                  pl.BlockSpec((d, tf), lambda i, f: (0, f)),
                  pl.BlockSpec((d, tf), lambda i, f: (0, f + nf)),
                  pl.BlockSpec((CONV_WIDTH, tf), lambda i, f: (0, f)),
                  pl.BlockSpec((CONV_WIDTH, tf), lambda i, f: (0, f + nf)),
                  pl.BlockSpec((1, tf), lambda i, f: (0, f)),
                  pl.BlockSpec((1, tf), lambda i, f: (0, f + nf)),
                  pl.BlockSpec((tf, d), lambda i, f: (f, 0)),
                  pl.BlockSpec((1, d), lambda i, f: (0, 0))],
        out_specs=pl.BlockSpec((tm, d), lambda i, f: (i, 0)),
        out_shape=jax.ShapeDtypeStruct((t, d), F32),
        scratch_shapes=[pltpu.VMEM((tm, d), BF16), pltpu.VMEM((tm, d), F32),
                        pltpu.VMEM((nf, 2, 8, tf), F32)],
        compiler_params=_params(2),
        name="ffn",
    )(x2d, g, wup, wup, cw, cw, cb, cb, wdown, gout)


def _in_column_order():
    nq = N_NSA_HEADS * HEAD_DIM
    nkv = N_NSA_KV * HEAD_DIM
    ngate = 3 * N_NSA_HEADS
    sbw = N_SB_HEADS * HEAD_DIM
    fxw = N_FOX_HEADS * HEAD_DIM
    off = {}
    pos = 0
    for name, size in [("qn", nq), ("kc", nkv), ("vc", nkv), ("ks", nkv), ("vs", nkv), ("kw", nkv),
                       ("vw", nkv), ("gn", ngate), ("qs", sbw), ("ksb", sbw), ("vsb", sbw),
                       ("qf", fxw), ("kf", fxw), ("vf", fxw), ("fl", N_FOX_HEADS)]:
        off[name] = (pos, size)
        pos += size
    rng = lambda name: list(range(off[name][0], off[name][0] + off[name][1]))
    group = N_NSA_HEADS // N_NSA_KV
    cols = []
    for j in range(group):
        for g in range(N_NSA_KV):
            h = g * group + j
            cols += list(range(off["qn"][0] + h * HEAD_DIM, off["qn"][0] + (h + 1) * HEAD_DIM))
    for name in ("ks", "vs", "kw", "vw", "qs", "ksb", "vsb", "qf", "kf", "vf", "kc", "vc", "gn"):
        cols += rng(name)
    cols += rng("fl") * FL_COPIES
    cols += [-1] * (LANES - ngate - FL_COPIES * N_FOX_HEADS)
    return np.asarray(cols, np.int32), pos


def _nsa_out_rows():
    group = N_NSA_HEADS // N_NSA_KV
    rows = []
    for j in range(group):
        for g in range(N_NSA_KV):
            h = g * group + j
            rows += list(range(h * HEAD_DIM, (h + 1) * HEAD_DIM))
    return np.asarray(rows, np.int32)


def _rope_tables(seq):
    half = ROPE_DIM // 2
    inv = ROPE_THETA ** (-jnp.arange(half, dtype=F32) / half)
    ang = jnp.arange(seq).astype(F32)[:, None] * inv[None, :]
    cos, sin = jnp.cos(ang), jnp.sin(ang)
    ones = jnp.ones((seq, HEAD_DIM - ROPE_DIM), F32)
    zeros = jnp.zeros((seq, HEAD_DIM - half), F32)
    ctab = jnp.concatenate([cos, cos, ones], axis=1)
    s1 = jnp.concatenate([-sin, zeros], axis=1)
    s2 = jnp.concatenate([jnp.zeros((seq, half), F32), sin, jnp.zeros((seq, HEAD_DIM - ROPE_DIM), F32)], axis=1)
    rep = LANES // HEAD_DIM
    return tuple(jnp.tile(a, (1, rep)) for a in (ctab, s1, s2))


def _overlap_t(seq, ncp):
    n_cmp = (seq - CMP_BLOCK) // CMP_STRIDE + 1
    n_sel = seq // SEL_BLOCK
    starts = np.arange(n_cmp) * CMP_STRIDE
    sel_starts = np.arange(n_sel) * SEL_BLOCK
    ov = ((starts[:, None] < sel_starts[None, :] + SEL_BLOCK)
          & (starts[:, None] + CMP_BLOCK > sel_starts[None, :])).astype(np.float32)
    out = np.zeros((n_sel, ncp), np.float32)
    out[:, :n_cmp] = ov.T
    return out, n_cmp, n_sel


def _sel_bias_table(seq, n_sel):
    tab = np.zeros((seq, LANES), np.float32)
    tab[np.arange(seq), np.arange(seq) // SEL_BLOCK] = SEL_BIAS
    return tab


def _gate_expanders():
    group = N_NSA_HEADS // N_NSA_KV
    e = np.zeros((group, LANES, 3 * LANES), np.float32)
    for j in range(group):
        for g in range(N_NSA_KV):
            h = g * group + j
            for r in range(3):
                lo = r * LANES + g * HEAD_DIM
                e[j, 3 * h + r, lo:lo + HEAD_DIM] = 1.0
    return e


def _blockdiag(blocks):
    n = len(blocks)
    r, c = blocks[0].shape
    out = jnp.zeros((n * r, n * c), blocks[0].dtype)
    for i, blk in enumerate(blocks):
        out = out.at[i * r:(i + 1) * r, i * c:(i + 1) * c].set(blk)
    return out


def _compress_weights(pe_k, pe_v, wk1, wk2, wv1, wv2):
    hop = CMP_STRIDE
    lblk = CMP_BLOCK

    nblk = 2 * N_NSA_KV
    stacked = jnp.stack([wk1] * N_NSA_KV + [wv1] * N_NSA_KV, axis=1)
    bd = jnp.einsum('lkdc,kj->lkdjc', stacked, jnp.eye(nblk, dtype=stacked.dtype))
    bd = bd.reshape(lblk, nblk * HEAD_DIM, nblk * HEAD_DIM).astype(BF16)
    pe = jnp.concatenate([pe_k] * N_NSA_KV + [pe_v] * N_NSA_KV, axis=1).astype(F32)

    def first_layer(lo):
        return bd[lo:lo + hop].reshape(hop * nblk * HEAD_DIM, nblk * HEAD_DIM)

    def pe_row(lo):
        return pe[lo:lo + hop].reshape(1, hop * nblk * HEAD_DIM)

    assert lblk == 2 * hop
    w2k = _blockdiag([wk2] * N_NSA_KV).astype(BF16)
    w2vt = _blockdiag([wv2.T] * N_NSA_KV).astype(BF16)
    return pe_row(0), pe_row(hop), first_layer(0), first_layer(hop), w2k, w2vt


def kernel(x, mem, norm_mix, w_in, b_forget, cmp_pe_k, cmp_pe_v, cmp_wk1, cmp_wk2, cmp_wv1, cmp_wv2, w_out, norm_cross, norm_mem, w_mq, w_mk, w_mv, w_mo, norm_ffn, w_up, conv_w, conv_b, w_down, norm_final):
    bsz, seq, d = x.shape
    depth = w_in.shape[0]
    t = bsz * seq
    dff = w_down.shape[1]
    assert seq % 512 == 0 and d % LANES == 0

    cols, n_in = _in_column_order()
    assert n_in == w_in.shape[2]
    col_ok = jnp.asarray(cols >= 0)[None, :]
    col_src = jnp.asarray(np.maximum(cols, 0))
    out_rows = jnp.asarray(_nsa_out_rows())
    ctab, s1tab, s2tab = _rope_tables(seq)
    ncp = seq // CMP_STRIDE
    ovt_np, n_cmp, n_sel = _overlap_t(seq, ncp)
    ovt = jnp.asarray(ovt_np, BF16)
    eneg = jnp.asarray(_sel_bias_table(seq, n_sel), BF16)
    later = jnp.asarray(np.tril(np.ones((SUB, SUB), np.float32), -1), BF16)
    egate = jnp.asarray(_gate_expanders(), BF16)
    nsa_w = N_NSA_HEADS * HEAD_DIM
    sb_w = N_SB_HEADS * HEAD_DIM
    bf_row = jnp.zeros((depth, 1, LANES), F32).at[:, 0, FL_LANE:FL_LANE + FL_COPIES * N_FOX_HEADS].set(
        jnp.tile(b_forget, (1, FL_COPIES)))

    xs = x.reshape(t, d)
    for i in range(depth):
        w = jnp.where(col_ok, jnp.take(w_in[i], col_src, axis=1), 0.0).astype(BF16)
        main, kcv, small = _proj(xs, norm_mix[i][None, :], w, ctab, s1tab, s2tab, seq, ROW_TILE)

        fox_qa, fox_ka = _gates(small.reshape(bsz, seq, LANES), bf_row[i])

        pea, peb, wa, wb, w2k, w2vt = _compress_weights(
            cmp_pe_k[i], cmp_pe_v[i], cmp_wk1[i], cmp_wk2[i], cmp_wv1[i], cmp_wv2[i])
        kc, vct = _compress(kcv.reshape(bsz, ncp, CMP_STRIDE * 2 * LANES), pea, peb, wa, wb, w2k, w2vt)

        ocmp, nm = _nsa_cmp(main, kc, vct, ovt, bsz, seq, CMP_Q_BLOCK, n_cmp, n_sel)
        o_nsa = _nsa_main(main, nm, eneg, small, egate, ocmp, bsz, seq)
        o_sb = _sb(main, later, bsz, seq)
        o_fox = _fox(main, fox_qa, fox_ka, bsz, seq)

        km, vm = _mem_kv(mem, norm_mem[i][None, :], w_mk[i].astype(BF16), w_mv[i].astype(BF16))
        wo = w_out[i]
        xs = _cross(xs, o_nsa, o_sb, o_fox,
                    jnp.take(wo[:nsa_w], out_rows, axis=0).astype(BF16),
                    wo[nsa_w:nsa_w + sb_w].astype(BF16), wo[nsa_w + sb_w:].astype(BF16),
                    norm_cross[i][None, :], w_mq[i].astype(BF16), km, vm, w_mo[i].astype(BF16), seq, CROSS_TILE)
        xs = _ffn(xs, norm_ffn[i][None, :], w_up[i].astype(BF16), conv_w[i], conv_b[i][None, :],
                  w_down[i].astype(BF16), norm_final[None, :], seq, ROW_TILE, FFN_TF, i == depth - 1)
    return xs.reshape(bsz, seq, d)
```

```python
import functools

import numpy as np
import jax
import jax.numpy as jnp
from jax import lax
from jax.experimental import pallas as pl
from jax.experimental.pallas import tpu as pltpu

N_NSA_HEADS = 8
N_NSA_KV = 2
N_SB_HEADS = 4
N_FOX_HEADS = 4
HEAD_DIM = 64
ROPE_DIM = 16
ROPE_THETA = 500000.0
CMP_BLOCK = 32
CMP_STRIDE = 16
SEL_BLOCK = 64
SEL_TOPK = 16
WINDOW = 512
MEM_HEAD_DIM = 64
CONV_WIDTH = 3
EPS = 1e-6

LANES = 128
Q_BLOCK = 512
KEY_BLOCK = 512
SUB = 128
FL_LANE = 3 * N_NSA_HEADS
FL_COPIES = 6
ROW_TILE = 512
CROSS_TILE = 1024
FFN_TF = 1408
CMP_Q_BLOCK = 256
NSA_TILES = 4
NEG_MASK = -1e30
SEL_BIAS = -(2.0 ** 30)
VMEM_LIMIT = 48 * 1024 * 1024

F32 = jnp.float32
BF16 = jnp.bfloat16

T_QN, T_KS, T_VS, T_KW, T_VW = 0, 4, 5, 6, 7
T_QS, T_KSB, T_VSB = 8, 10, 12
T_QF, T_KF, T_VF = 14, 16, 18
N_MAIN_TILES = 20
ROPE_MAIN_TILES = (0, 1, 2, 3, T_KS, T_KW)

_NT = (((1,), (1,)), ((), ()))


def _params(n_grid):
    return pltpu.CompilerParams(dimension_semantics=("arbitrary",) * n_grid,
                                vmem_limit_bytes=VMEM_LIMIT)


def _rms(xf, g):
    return xf * lax.rsqrt(jnp.mean(xf * xf, axis=-1, keepdims=True) + EPS) * g


def _sigmoid(x):
    return 1.0 / (1.0 + jnp.exp(-x))


def _log_sigmoid(x):
    return jnp.minimum(x, 0.0) - jnp.log(1.0 + jnp.exp(-jnp.abs(x)))


def _dot(a, b):
    return jnp.dot(a, b, preferred_element_type=F32)


def _dot_nt(a, b):
    return lax.dot_general(a, b, _NT, preferred_element_type=F32)


def _split2(x):
    hi = x.astype(BF16)
    lo = (x - hi.astype(F32)).astype(BF16)
    return hi, lo


def _proj_kernel(x_ref, g_ref, w_ref, c_ref, s1_ref, s2_ref, main_ref, kcv_ref, small_ref):
    h = _rms(x_ref[...], g_ref[...]).astype(BF16)
    cos = c_ref[...]
    sin_lo = s1_ref[...]
    sin_hi = s2_ref[...]

    def rope(a):
        return a * cos + pltpu.roll(a, LANES - 8, 1) * sin_lo + pltpu.roll(a, 8, 1) * sin_hi

    for c in range(N_MAIN_TILES // 2):
        acc = _dot(h, w_ref[:, 2 * c * LANES:(2 * c + 2) * LANES])
        for k in range(2):
            t = 2 * c + k
            a = acc[:, k * LANES:(k + 1) * LANES]
            if t in ROPE_MAIN_TILES:
                a = rope(a)
            main_ref[:, t * LANES:(t + 1) * LANES] = a.astype(BF16)
    base = N_MAIN_TILES * LANES
    acc = _dot(h, w_ref[:, base:base + 2 * LANES])
    kcv_ref[:, :LANES] = rope(acc[:, :LANES])
    kcv_ref[:, LANES:] = acc[:, LANES:]
    small_ref[...] = _dot(h, w_ref[:, base + 2 * LANES:base + 3 * LANES])


def _proj(x2d, g, w, ctab, s1tab, s2tab, seq, tm):
    t = x2d.shape[0]
    d = x2d.shape[1]
    ncol = w.shape[1]
    spt = seq // tm
    tab = pl.BlockSpec((tm, LANES), lambda i: (i % spt, 0))
    return pl.pallas_call(
        _proj_kernel,
        grid=(t // tm,),
        in_specs=[pl.BlockSpec((tm, d), lambda i: (i, 0)),
                  pl.BlockSpec((1, d), lambda i: (0, 0)),
                  pl.BlockSpec((d, ncol), lambda i: (0, 0)),
                  tab, tab, tab],
        out_specs=[pl.BlockSpec((tm, N_MAIN_TILES * LANES), lambda i: (i, 0)),
                   pl.BlockSpec((tm, 2 * LANES), lambda i: (i, 0)),
                   pl.BlockSpec((tm, LANES), lambda i: (i, 0))],
        out_shape=[jax.ShapeDtypeStruct((t, N_MAIN_TILES * LANES), BF16),
                   jax.ShapeDtypeStruct((t, 2 * LANES), F32),
                   jax.ShapeDtypeStruct((t, LANES), F32)],
        compiler_params=_params(1),
        name="proj",
    )(x2d, g, w, ctab, s1tab, s2tab)


def _gates_kernel(s_ref, bf_ref, qa_ref, ka_ref):
    lf = _log_sigmoid(s_ref[...] + bf_ref[...])
    seq = lf.shape[0]
    row = lax.broadcasted_iota(jnp.int32, lf.shape, 0)
    sh = 1
    while sh < seq:
        lf = lf + jnp.where(row >= sh, pltpu.roll(lf, sh, 0), 0.0)
        sh *= 2
    hi = lf.astype(BF16).astype(F32)
    r1 = lf - hi
    mid = r1.astype(BF16).astype(F32)
    lo = (r1 - mid).astype(BF16).astype(F32)
    lane = lax.broadcasted_iota(jnp.int32, lf.shape, 1)

    def group(i):
        return (lane >= FL_LANE + i * N_FOX_HEADS) & (lane < FL_LANE + (i + 1) * N_FOX_HEADS)

    piece = jnp.where(group(0) | group(3), hi, jnp.where(group(1) | group(4), mid, lo))
    in_t = group(0) | group(1) | group(2)
    in_s = group(3) | group(4) | group(5)
    qa_ref[...] = jnp.where(in_t, piece, jnp.where(in_s, 1.0, 0.0)).astype(BF16)
    ka_ref[...] = jnp.where(in_t, 1.0, jnp.where(in_s, -piece, 0.0)).astype(BF16)


def _gates(small3, bf):
    b, seq, _ = small3.shape
    return pl.pallas_call(
        _gates_kernel,
        grid=(b,),
        in_specs=[pl.BlockSpec((None, seq, LANES), lambda i: (i, 0, 0)),
                  pl.BlockSpec((1, LANES), lambda i: (0, 0))],
        out_specs=[pl.BlockSpec((None, seq, LANES), lambda i: (i, 0, 0)),
                   pl.BlockSpec((None, seq, LANES), lambda i: (i, 0, 0))],
        out_shape=[jax.ShapeDtypeStruct((b, seq, LANES), BF16),
                   jax.ShapeDtypeStruct((b, seq, LANES), BF16)],
        compiler_params=_params(1),
        name="gates",
    )(small3, bf)


def _compress_kernel(r_ref, pea_ref, peb_ref, wa_ref, wb_ref, w2k_ref, w2vt_ref, kc_ref, vct_ref):
    r = r_ref[...]
    a = _dot((r + pea_ref[...]).astype(BF16), wa_ref[...])
    b = _dot((r + peb_ref[...]).astype(BF16), wb_ref[...])
    nrow = r.shape[0]
    hp = a + pltpu.roll(b, nrow - 1, 0)
    hid = (hp * _sigmoid(hp)).astype(BF16)
    kc_ref[...] = _dot(hid[:, :LANES], w2k_ref[...]).astype(BF16)
    vct_ref[...] = _dot_nt(w2vt_ref[...], hid[:, LANES:]).astype(BF16)


def _compress(r3, pea, peb, wa, wb, w2k, w2vt):
    b, nrow, width = r3.shape
    const = lambda i: (0, 0)
    return pl.pallas_call(
        _compress_kernel,
        grid=(b,),
        in_specs=[pl.BlockSpec((None, nrow, width), lambda i: (i, 0, 0)),
                  pl.BlockSpec((1, width), const), pl.BlockSpec((1, width), const),
                  pl.BlockSpec((width, 2 * LANES), const), pl.BlockSpec((width, 2 * LANES), const),
                  pl.BlockSpec((LANES, LANES), const), pl.BlockSpec((LANES, LANES), const)],
        out_specs=[pl.BlockSpec((None, nrow, LANES), lambda i: (i, 0, 0)),
                   pl.BlockSpec((None, LANES, nrow), lambda i: (i, 0, 0))],
        out_shape=[jax.ShapeDtypeStruct((b, nrow, LANES), BF16),
                   jax.ShapeDtypeStruct((b, LANES, nrow), BF16)],
        compiler_params=_params(1),
        name="compress",
    )(r3, pea, peb, wa, wb, w2k, w2vt)


def _nsa_cmp_kernel(q_ref, kc_ref, vct_ref, ovt_ref, ocmp_ref, nm_ref, *, qb, n_cmp, n_sel):
    qi = pl.program_id(1)
    ncp = kc_ref.shape[0]
    tq = qi * qb + lax.broadcasted_iota(jnp.int32, (ncp, qb), 1)
    nblk = lax.broadcasted_iota(jnp.int32, (ncp, qb), 0)
    cmask = (nblk * CMP_STRIDE + (CMP_BLOCK - 1) <= tq) & (nblk < n_cmp)
    row = lax.broadcasted_iota(jnp.int32, (LANES, qb), 0)
    lane = lax.broadcasted_iota(jnp.int32, (qb, LANES), 1)
    kc = kc_ref[...]
    vct = vct_ref[...]
    psum = [jnp.zeros((ncp, qb), F32), jnp.zeros((ncp, qb), F32)]
    for j in range(N_NSA_HEADS // 2):
        qt = q_ref[:, j * LANES:(j + 1) * LANES].astype(F32) * (HEAD_DIM ** -0.5)
        outs = []
        for half in range(2):
            qm = jnp.where(lane < HEAD_DIM if half == 0 else lane >= HEAD_DIM, qt, 0.0).astype(BF16)
            lt = _dot_nt(kc, qm)
            m = jnp.max(jnp.where(cmask, lt, NEG_MASK), axis=0, keepdims=True)
            p = jnp.where(cmask, jnp.exp(lt - m), 0.0)
            p = p / jnp.maximum(jnp.sum(p, axis=0, keepdims=True), 1e-30)
            psum[half] = psum[half] + p
            outs.append(_dot(vct, p.astype(BF16)))
        ot = jnp.where(row < HEAD_DIM, outs[0], outs[1])
        for s in range(qb // LANES):
            ocmp_ref[s * LANES:(s + 1) * LANES, j * LANES:(j + 1) * LANES] = (
                ot[:, s * LANES:(s + 1) * LANES].T)

    jrow = lax.broadcasted_iota(jnp.int32, (n_sel, qb), 0)
    tsel = qi * qb + lax.broadcasted_iota(jnp.int32, (n_sel, qb), 1)
    cur = tsel // SEL_BLOCK
    forced = (jrow == 0) | (jrow == cur) | (jrow == cur - 1)
    ovt = ovt_ref[...]
    for g in range(N_NSA_KV):
        hi, lo = _split2(psum[g])
        imp = _dot(ovt, hi) + _dot(ovt, lo)
        imp = jnp.where(jrow <= cur, jnp.where(forced, jnp.inf, imp), -jnp.inf)
        before = jnp.zeros((n_sel, qb), F32)
        for i in range(n_sel):
            ri = imp[i:i + 1, :]
            ahead = (ri > imp) | ((ri == imp) & (jrow > i))
            before = before + jnp.where(ahead, 1.0, 0.0)
        member = (before < min(SEL_TOPK, n_sel)) & (imp > -jnp.inf)
        not_member = jnp.where(member, 0.0, 1.0)
        padded = jnp.concatenate([not_member, jnp.zeros((LANES - n_sel, qb), F32)], axis=0)
        for s in range(qb // LANES):
            nm_ref[g, s * LANES:(s + 1) * LANES, :] = (
                padded[:, s * LANES:(s + 1) * LANES].T.astype(BF16))


def _nsa_cmp(main, kc, vct, ovt, bsz, seq, qb, n_cmp, n_sel):
    t = main.shape[0]
    nq = seq // qb
    ncp = kc.shape[1]
    kern = functools.partial(_nsa_cmp_kernel, qb=qb, n_cmp=n_cmp, n_sel=n_sel)
    return pl.pallas_call(
        kern,
        grid=(bsz, nq),
        in_specs=[pl.BlockSpec((qb, 4 * LANES), lambda b, i: (b * nq + i, 0)),
                  pl.BlockSpec((None, ncp, LANES), lambda b, i: (b, 0, 0)),
                  pl.BlockSpec((None, LANES, ncp), lambda b, i: (b, 0, 0)),
                  pl.BlockSpec((n_sel, ncp), lambda b, i: (0, 0))],
        out_specs=[pl.BlockSpec((qb, 4 * LANES), lambda b, i: (b * nq + i, 0)),
                   pl.BlockSpec((None, N_NSA_KV, qb, LANES), lambda b, i: (b, 0, i, 0))],
        out_shape=[jax.ShapeDtypeStruct((t, 4 * LANES), F32),
                   jax.ShapeDtypeStruct((bsz, N_NSA_KV, seq, LANES), BF16)],
        compiler_params=_params(2),
        name="nsa_cmp",
    )(main, kc, vct, ovt)


def _softmax_step(s, mask, vaug, carry):
    m, acc = carry
    if mask is not None:
        s = jnp.where(mask, s, NEG_MASK)
    m_new = jnp.maximum(m, jnp.max(s, axis=-1, keepdims=True))
    p = jnp.exp(s - m_new)
    acc = jnp.exp(m - m_new) * acc + _dot(p.astype(BF16), vaug)
    return m_new, acc


def _softmax_init(qb):
    return (jnp.full((qb, 1), NEG_MASK, F32), jnp.zeros((qb, 2 * LANES), F32))


def _softmax_finish(acc):
    return acc[:, :LANES] / jnp.maximum(acc[:, LANES:], 1e-30)


def _pair_tile(is_a, a, b):
    return jnp.where(is_a, a, b)


def _attn_heads(q_ref):
    qb, width = q_ref.shape
    is_a = lax.broadcasted_iota(jnp.int32, (qb, LANES), 1) < HEAD_DIM
    heads = []
    for t in range(width // LANES):
        qt = q_ref[:, t * LANES:(t + 1) * LANES].astype(F32) * (HEAD_DIM ** -0.5)
        heads.append((t, jnp.where(is_a, qt, 0.0).astype(BF16)))
        heads.append((t, jnp.where(is_a, 0.0, qt).astype(BF16)))
    return is_a, heads


def _expand(x, e):
    hi = x.astype(BF16)
    r1 = x - hi.astype(F32)
    mid = r1.astype(BF16)
    lo = (r1 - mid.astype(F32)).astype(BF16)
    return _dot(hi, e) + _dot(mid, e) + _dot(lo, e)


def _nsa_main_kernel(q_ref, ks_ref, vs_ref, kw_ref, vw_ref, nm_ref, eneg_ref, small_ref, egate_ref,
                     ocmp_ref, o_ref, *acc_refs):
    qb = q_ref.shape[0]
    qi = pl.program_id(2)
    is_a, heads = _attn_heads(q_ref)
    nh = len(heads)
    rowi = lax.broadcasted_iota(jnp.int32, (qb, KEY_BLOCK), 0)
    coli = lax.broadcasted_iota(jnp.int32, (qb, KEY_BLOCK), 1)
    qsel = [jnp.concatenate([q, nm_ref[h % 2]], axis=1) for h, (_, q) in enumerate(heads)]
    ones_k = jnp.ones((KEY_BLOCK, LANES), BF16)

    def sel_chunk(c, ms, mask):
        off = pl.multiple_of(c * KEY_BLOCK, KEY_BLOCK)
        k = jnp.concatenate([ks_ref[pl.ds(off, KEY_BLOCK), :], eneg_ref[pl.ds(off, KEY_BLOCK), :]], axis=1)
        v = jnp.concatenate([vs_ref[pl.ds(off, KEY_BLOCK), :], ones_k], axis=1)
        out = []
        for h in range(nh):
            m_new, acc_new = _softmax_step(_dot_nt(qsel[h], k), mask, v, (ms[h], acc_refs[h][...]))
            out.append(m_new)
            acc_refs[h][...] = acc_new
        return tuple(out)

    for h in range(nh):
        acc_refs[h][...] = jnp.zeros((qb, 2 * LANES), F32)
    ms = tuple(jnp.full((qb, 1), NEG_MASK, F32) for _ in range(nh))
    ms = lax.fori_loop(0, qi, lambda c, ms: sel_chunk(c, ms, None), ms)
    sel_chunk(qi, ms, coli <= rowi)

    span = WINDOW + SUB
    rw = lax.broadcasted_iota(jnp.int32, (SUB, span), 0)
    cw = lax.broadcasted_iota(jnp.int32, (SUB, span), 1)
    ones_w = jnp.ones((span, LANES), BF16)
    wins = [[] for _ in range(nh)]
    for r in range(qb // SUB):
        t0 = qi * qb + r * SUB
        start = pl.multiple_of(jnp.maximum(t0 - WINDOW, 0), SUB)
        diff = (t0 - start) + rw - cw
        mask = (diff >= 0) & (diff < WINDOW)
        k = kw_ref[pl.ds(start, span), :]
        v = jnp.concatenate([vw_ref[pl.ds(start, span), :], ones_w], axis=1)
        for h, (_, q) in enumerate(heads):
            s = jnp.where(mask, _dot_nt(q[r * SUB:(r + 1) * SUB], k), NEG_MASK)
            p = jnp.exp(s - jnp.max(s, axis=-1, keepdims=True))
            wins[h].append(_softmax_finish(_dot(p.astype(BF16), v)))

    sig = _sigmoid(small_ref[...])
    for t in range(nh // 2):
        lanes = slice(t * LANES, (t + 1) * LANES)
        sel_t = _pair_tile(is_a, _softmax_finish(acc_refs[2 * t][...]), _softmax_finish(acc_refs[2 * t + 1][...]))
        win_t = _pair_tile(is_a, jnp.concatenate(wins[2 * t], axis=0), jnp.concatenate(wins[2 * t + 1], axis=0))
        gates = _expand(sig, egate_ref[t])
        out = (gates[:, :LANES] * ocmp_ref[:, lanes] + gates[:, LANES:2 * LANES] * sel_t
               + gates[:, 2 * LANES:] * win_t)
        o_ref[:, lanes] = out.astype(BF16)


def _nsa_main(main, nm, eneg, small, egate, ocmp, bsz, seq):
    t = main.shape[0]
    qb = Q_BLOCK
    nq = seq // qb
    ntile = N_NSA_HEADS // 2
    nstep = ntile // NSA_TILES
    w = NSA_TILES * LANES
    kv = lambda tile: pl.BlockSpec((seq, LANES), lambda b, j, i: (b, tile))
    return pl.pallas_call(
        _nsa_main_kernel,
        grid=(bsz, nstep, nq),
        in_specs=[pl.BlockSpec((qb, w), lambda b, j, i: (b * nq + i, T_QN // NSA_TILES + j)),
                  kv(T_KS), kv(T_VS), kv(T_KW), kv(T_VW),
                  pl.BlockSpec((None, N_NSA_KV, qb, LANES), lambda b, j, i: (b, 0, i, 0)),
                  pl.BlockSpec((seq, LANES), lambda b, j, i: (0, 0)),
                  pl.BlockSpec((qb, LANES), lambda b, j, i: (b * nq + i, 0)),
                  pl.BlockSpec((NSA_TILES, LANES, 3 * LANES), lambda b, j, i: (j, 0, 0)),
                  pl.BlockSpec((qb, w), lambda b, j, i: (b * nq + i, j))],
        out_specs=pl.BlockSpec((qb, w), lambda b, j, i: (b * nq + i, j)),
        out_shape=jax.ShapeDtypeStruct((t, ntile * LANES), BF16),
        scratch_shapes=[pltpu.VMEM((qb, 2 * LANES), F32)] * (2 * NSA_TILES),
        compiler_params=_params(3),
        name="nsa_main",
    )(main, main, main, main, main, nm, eneg, small, egate, ocmp)


def _sb_kernel(q_ref, k_ref, v_ref, u_ref, o_ref, *acc_refs):
    qb = q_ref.shape[0]
    qi = pl.program_id(1)
    is_a, heads = _attn_heads(q_ref)
    nh = len(heads)
    rowi = lax.broadcasted_iota(jnp.int32, (qb, KEY_BLOCK), 0)
    coli = lax.broadcasted_iota(jnp.int32, (qb, KEY_BLOCK), 1)
    row_s = lax.broadcasted_iota(jnp.int32, (qb, SUB), 0)
    col_s = lax.broadcasted_iota(jnp.int32, (qb, SUB), 1)
    later = u_ref[...]
    nsub = KEY_BLOCK // SUB

    def chunk(c, tails, diag):
        off = pl.multiple_of(c * KEY_BLOCK, KEY_BLOCK)
        new = []
        for h, (t, q) in enumerate(heads):
            lanes = slice(t * LANES, (t + 1) * LANES)
            k = k_ref[pl.ds(off, KEY_BLOCK), lanes]
            v = v_ref[pl.ds(off, KEY_BLOCK), lanes]
            tail = tails[h]
            z = _dot_nt(q, k)
            log_beta = _log_sigmoid(z)
            log_1m = log_beta - z
            if diag:
                log_1m = jnp.where(coli < rowi, log_1m, 0.0)
            parts = [None] * nsub
            for b in reversed(range(nsub)):
                lo_col, hi_col = b * SUB, (b + 1) * SUB
                x = log_1m[:, lo_col:hi_col]
                excl = _dot(x.astype(BF16), later)
                a = jnp.exp(log_beta[:, lo_col:hi_col] + excl + tail)
                if diag:
                    a = jnp.where(col_s + lo_col < row_s, a, 0.0)
                parts[b] = a.astype(BF16)
                tail = tail + jnp.sum(x, axis=-1, keepdims=True)
            acc_refs[h][...] += _dot(jnp.concatenate(parts, axis=1), v)
            new.append(tail)
        return tuple(new)

    for h in range(nh):
        acc_refs[h][...] = jnp.zeros((qb, LANES), F32)
    tails = chunk(qi, tuple(jnp.zeros((qb, 1), F32) for _ in range(nh)), True)
    lax.fori_loop(0, qi, lambda i, tails: chunk(qi - 1 - i, tails, False), tails)
    for t in range(nh // 2):
        o_ref[:, t * LANES:(t + 1) * LANES] = _pair_tile(
            is_a, acc_refs[2 * t][...], acc_refs[2 * t + 1][...]).astype(BF16)


def _sb(main, later, bsz, seq):
    t = main.shape[0]
    qb = Q_BLOCK
    nq = seq // qb
    w = N_SB_HEADS * HEAD_DIM
    blk = lambda tile: tile * LANES // w
    return pl.pallas_call(
        _sb_kernel,
        grid=(bsz, nq),
        in_specs=[pl.BlockSpec((qb, w), lambda b, i: (b * nq + i, blk(T_QS))),
                  pl.BlockSpec((seq, w), lambda b, i: (b, blk(T_KSB))),
                  pl.BlockSpec((seq, w), lambda b, i: (b, blk(T_VSB))),
                  pl.BlockSpec((SUB, SUB), lambda b, i: (0, 0))],
        out_specs=pl.BlockSpec((qb, w), lambda b, i: (b * nq + i, 0)),
        out_shape=jax.ShapeDtypeStruct((t, w), BF16),
        scratch_shapes=[pltpu.VMEM((qb, LANES), F32)] * N_SB_HEADS,
        compiler_params=_params(2),
        name="sb",
    )(main, main, main, later)


def _fox_kernel(q_ref, k_ref, v_ref, qa_ref, ka_ref, o_ref, *state):
    qb = q_ref.shape[0]
    qi = pl.program_id(1)
    is_a, heads = _attn_heads(q_ref)
    nh = len(heads)
    rowi = lax.broadcasted_iota(jnp.int32, (qb, KEY_BLOCK), 0)
    coli = lax.broadcasted_iota(jnp.int32, (qb, KEY_BLOCK), 1)
    lane = lax.broadcasted_iota(jnp.int32, (qb, LANES), 1)
    qa = qa_ref[...]
    qaug = []
    for h, (_, q) in enumerate(heads):
        assert FL_LANE % N_FOX_HEADS == 0 and N_FOX_HEADS & (N_FOX_HEADS - 1) == 0
        mine = ((lane >= FL_LANE) & (lane < FL_LANE + FL_COPIES * N_FOX_HEADS)
                & ((lane & (N_FOX_HEADS - 1)) == h))
        qaug.append(jnp.concatenate([q, jnp.where(mine, qa, jnp.zeros_like(qa))], axis=1))
    ones_k = jnp.ones((KEY_BLOCK, LANES), BF16)

    m_refs, acc_refs = state[:nh], state[nh:]

    def chunk(c, ms, mask):
        off = pl.multiple_of(c * KEY_BLOCK, KEY_BLOCK)
        ka = ka_ref[pl.ds(off, KEY_BLOCK), :]
        out = []
        for h, (t, _) in enumerate(heads):
            lanes = slice(t * LANES, (t + 1) * LANES)
            k = jnp.concatenate([k_ref[pl.ds(off, KEY_BLOCK), lanes], ka], axis=1)
            v = jnp.concatenate([v_ref[pl.ds(off, KEY_BLOCK), lanes], ones_k], axis=1)
            m_new, acc_new = _softmax_step(_dot_nt(qaug[h], k), mask, v, (ms[h], acc_refs[h][...]))
            out.append(m_new)
            acc_refs[h][...] = acc_new
        return tuple(out)

    for h in range(nh):
        acc_refs[h][...] = jnp.zeros((qb, 2 * LANES), F32)

    ms = tuple(jnp.full((qb, 1), NEG_MASK, F32) for _ in range(nh))
    ms = lax.fori_loop(0, qi, lambda c, ms: chunk(c, ms, None), ms)
    chunk(qi, ms, coli <= rowi)
    for t in range(nh // 2):
        out = _pair_tile(is_a, _softmax_finish(acc_refs[2 * t][...]), _softmax_finish(acc_refs[2 * t + 1][...]))
        o_ref[:, t * LANES:(t + 1) * LANES] = out.astype(BF16)


def _fox(main, qa, ka, bsz, seq):
    t = main.shape[0]
    qb = Q_BLOCK
    nq = seq // qb
    w = N_FOX_HEADS * HEAD_DIM
    blk = lambda tile: tile * LANES // w
    return pl.pallas_call(
        _fox_kernel,
        grid=(bsz, nq),
        in_specs=[pl.BlockSpec((qb, w), lambda b, i: (b * nq + i, blk(T_QF))),
                  pl.BlockSpec((seq, w), lambda b, i: (b, blk(T_KF))),
                  pl.BlockSpec((seq, w), lambda b, i: (b, blk(T_VF))),
                  pl.BlockSpec((None, qb, LANES), lambda b, i: (b, i, 0)),
                  pl.BlockSpec((None, seq, LANES), lambda b, i: (b, 0, 0))],
        out_specs=pl.BlockSpec((qb, w), lambda b, i: (b * nq + i, 0)),
        out_shape=jax.ShapeDtypeStruct((t, w), BF16),
        scratch_shapes=([pltpu.VMEM((qb, 1), F32)] * N_FOX_HEADS
                        + [pltpu.VMEM((qb, 2 * LANES), F32)] * N_FOX_HEADS),
        compiler_params=_params(2),
        name="fox",
    )(main, main, main, qa, ka)


def _mem_kv_kernel(m_ref, g_ref, wk_ref, wv_ref, k_ref, v_ref):
    h = _rms(m_ref[...], g_ref[...]).astype(BF16)
    k_ref[...] = _dot(h, wk_ref[...]).astype(BF16)
    v_ref[...] = _dot(h, wv_ref[...]).astype(BF16)


def _mem_kv(mem, g, wk, wv):
    b, m, d = mem.shape
    mw = wk.shape[1]
    const = lambda i: (0, 0)
    return pl.pallas_call(
        _mem_kv_kernel,
        grid=(b,),
        in_specs=[pl.BlockSpec((None, m, d), lambda i: (i, 0, 0)),
                  pl.BlockSpec((1, d), const), pl.BlockSpec((d, mw), const), pl.BlockSpec((d, mw), const)],
        out_specs=[pl.BlockSpec((None, m, mw), lambda i: (i, 0, 0)),
                   pl.BlockSpec((None, m, mw), lambda i: (i, 0, 0))],
        out_shape=[jax.ShapeDtypeStruct((b, m, mw), BF16), jax.ShapeDtypeStruct((b, m, mw), BF16)],
        compiler_params=_params(1),
        name="mem_kv",
    )(mem, g, wk, wv)


def _cross_kernel(x_ref, on_ref, os_ref, of_ref, won_ref, wos_ref, wof_ref, g_ref, wq_ref,
                  km_ref, vm_ref, wmo_ref, o_ref):
    x1 = (x_ref[...] + _dot(on_ref[...], won_ref[...]) + _dot(os_ref[...], wos_ref[...])
          + _dot(of_ref[...], wof_ref[...]))
    h = _rms(x1, g_ref[...]).astype(BF16)
    q = _dot(h, wq_ref[...]) * (MEM_HEAD_DIM ** -0.5)
    tm = q.shape[0]
    lane = lax.broadcasted_iota(jnp.int32, (tm, LANES), 1)
    is_a = lane < MEM_HEAD_DIM
    tiles = []
    for p in range(q.shape[1] // LANES):
        qt = q[:, p * LANES:(p + 1) * LANES]
        k = km_ref[:, p * LANES:(p + 1) * LANES]
        v = vm_ref[:, p * LANES:(p + 1) * LANES]
        outs = []
        for half in range(2):
            qm = jnp.where(is_a if half == 0 else jnp.logical_not(is_a), qt, 0.0).astype(BF16)
            s = _dot_nt(qm, k)
            e = jnp.exp(s - jnp.max(s, axis=-1, keepdims=True))
            pr = e / jnp.sum(e, axis=-1, keepdims=True)
            outs.append(_dot(pr.astype(BF16), v))
        tiles.append(_pair_tile(is_a, outs[0], outs[1]).astype(BF16))
    attn = jnp.concatenate(tiles, axis=1)
    o_ref[...] = x1 + _dot(attn, wmo_ref[...])


def _cross(x2d, on, osb, ofx, won, wos, wof, g, wq, km, vm, wmo, seq, tm):
    t, d = x2d.shape
    mw = wq.shape[1]
    m = km.shape[1]
    spt = seq // tm
    const = lambda i: (0, 0)
    row = lambda w: pl.BlockSpec((tm, w), lambda i: (i, 0))
    return pl.pallas_call(
        _cross_kernel,
        grid=(t // tm,),
        in_specs=[row(d), row(on.shape[1]), row(osb.shape[1]), row(ofx.shape[1]),
                  pl.BlockSpec(won.shape, const), pl.BlockSpec(wos.shape, const),
                  pl.BlockSpec(wof.shape, const), pl.BlockSpec((1, d), const),
                  pl.BlockSpec((d, mw), const),
                  pl.BlockSpec((None, m, mw), lambda i: (i // spt, 0, 0)),
                  pl.BlockSpec((None, m, mw), lambda i: (i // spt, 0, 0)),
                  pl.BlockSpec((mw, d), const)],
        out_specs=row(d),
        out_shape=jax.ShapeDtypeStruct((t, d), F32),
        compiler_params=_params(1),
        name="cross",
    )(x2d, on, osb, ofx, won, wos, wof, g, wq, km, vm, wmo)


def _ffn_kernel(x_ref, g_ref, wug_ref, wuv_ref, cwg_ref, cwv_ref, cbg_ref, cbv_ref, wd_ref, gout_ref,
                o_ref, h_ref, acc_ref, tail_ref, *, tiles_per_seq, norm_out):
    i = pl.program_id(0)
    f = pl.program_id(1)
    nf = pl.num_programs(1)

    @pl.when(f == 0)
    def _():
        h_ref[...] = _rms(x_ref[...], g_ref[...]).astype(BF16)
        acc_ref[...] = jnp.zeros_like(acc_ref)

    h = h_ref[...]
    tm = h.shape[0]
    first = (i % tiles_per_seq) == 0
    rowi = lax.broadcasted_iota(jnp.int32, (tm, wug_ref.shape[1]), 0)

    @pl.when(first)
    def _():
        tail_ref[f] = jnp.zeros(tail_ref.shape[1:], F32)

    def conv(u, cw_ref, cb_ref, kind):
        prev = tail_ref[f, kind]
        u1 = jnp.where(rowi == 0, prev[7:8], pltpu.roll(u, 1, 0))
        u2 = jnp.where(rowi == 0, prev[6:7], jnp.where(rowi == 1, prev[7:8], pltpu.roll(u, 2, 0)))
        tail_ref[f, kind] = u[tm - 8:, :]
        return cw_ref[2:3, :] * u + cw_ref[1:2, :] * u1 + cw_ref[0:1, :] * u2 + cb_ref[...]

    gate = conv(_dot(h, wug_ref[...]), cwg_ref, cbg_ref, 0)
    val = conv(_dot(h, wuv_ref[...]), cwv_ref, cbv_ref, 1)
    act = (gate * _sigmoid(gate) * val).astype(BF16)
    acc_ref[...] += _dot(act, wd_ref[...])

    @pl.when(f == nf - 1)
    def _():
        y = x_ref[...] + acc_ref[...]
        o_ref[...] = _rms(y, gout_ref[...]) if norm_out else y


def _ffn(x2d, g, wup, cw, cb, wdown, gout, seq, tm, tf, norm_out):
    t, d = x2d.shape
    dff = wdown.shape[0]
    nf = dff // tf
    return pl.pallas_call(
        functools.partial(_ffn_kernel, tiles_per_seq=seq // tm, norm_out=norm_out),
        grid=(t // tm, nf),
        in_specs=[pl.BlockSpec((tm, d), lambda i, f: (i, 0)),
                  pl.BlockSpec((1, d), lambda i, f: (0, 0)),
                  pl.BlockSpec((d, tf), lambda i, f: (0, f)),
                  pl.BlockSpec((d, tf), lambda i, f: (0, f + nf)),
                  pl.BlockSpec((CONV_WIDTH, tf), lambda i, f: (0, f)),
                  pl.BlockSpec((CONV_WIDTH, tf), lambda i, f: (0, f + nf)),
                  pl.BlockSpec((1, tf), lambda i, f: (0, f)),
                  pl.BlockSpec((1, tf), lambda i, f: (0, f + nf)),
                  pl.BlockSpec((tf, d), lambda i, f: (f, 0)),
                  pl.BlockSpec((1, d), lambda i, f: (0, 0))],
        out_specs=pl.BlockSpec((tm, d), lambda i, f: (i, 0)),
        out_shape=jax.ShapeDtypeStruct((t, d), F32),
        scratch_shapes=[pltpu.VMEM((tm, d), BF16), pltpu.VMEM((tm, d), F32),
                        pltpu.VMEM((nf, 2, 8, tf), F32)],
        compiler_params=_params(2),
        name="ffn",
    )(x2d, g, wup, wup, cw, cw, cb, cb, wdown, gout)


def _in_column_order():
    nq = N_NSA_HEADS * HEAD_DIM
    nkv = N_NSA_KV * HEAD_DIM
    ngate = 3 * N_NSA_HEADS
    sbw = N_SB_HEADS * HEAD_DIM
    fxw = N_FOX_HEADS * HEAD_DIM
    off = {}
    pos = 0
    for name, size in [("qn", nq), ("kc", nkv), ("vc", nkv), ("ks", nkv), ("vs", nkv), ("kw", nkv),
                       ("vw", nkv), ("gn", ngate), ("qs", sbw), ("ksb", sbw), ("vsb", sbw),
                       ("qf", fxw), ("kf", fxw), ("vf", fxw), ("fl", N_FOX_HEADS)]:
        off[name] = (pos, size)
        pos += size
    rng = lambda name: list(range(off[name][0], off[name][0] + off[name][1]))
    group = N_NSA_HEADS // N_NSA_KV
    cols = []
    for j in range(group):
        for g in range(N_NSA_KV):
            h = g * group + j
            cols += list(range(off["qn"][0] + h * HEAD_DIM, off["qn"][0] + (h + 1) * HEAD_DIM))
    for name in ("ks", "vs", "kw", "vw", "qs", "ksb", "vsb", "qf", "kf", "vf", "kc", "vc", "gn"):
        cols += rng(name)
    cols += rng("fl") * FL_COPIES
    cols += [-1] * (LANES - ngate - FL_COPIES * N_FOX_HEADS)
    return np.asarray(cols, np.int32), pos


def _nsa_out_rows():
    group = N_NSA_HEADS // N_NSA_KV
    rows = []
    for j in range(group):
        for g in range(N_NSA_KV):
            h = g * group + j
            rows += list(range(h * HEAD_DIM, (h + 1) * HEAD_DIM))
    return np.asarray(rows, np.int32)


def _rope_tables(seq):
    half = ROPE_DIM // 2
    inv = ROPE_THETA ** (-jnp.arange(half, dtype=F32) / half)
    ang = jnp.arange(seq).astype(F32)[:, None] * inv[None, :]
    cos, sin = jnp.cos(ang), jnp.sin(ang)
    ones = jnp.ones((seq, HEAD_DIM - ROPE_DIM), F32)
    zeros = jnp.zeros((seq, HEAD_DIM - half), F32)
    ctab = jnp.concatenate([cos, cos, ones], axis=1)
    s1 = jnp.concatenate([-sin, zeros], axis=1)
    s2 = jnp.concatenate([jnp.zeros((seq, half), F32), sin, jnp.zeros((seq, HEAD_DIM - ROPE_DIM), F32)], axis=1)
    rep = LANES // HEAD_DIM
    return tuple(jnp.tile(a, (1, rep)) for a in (ctab, s1, s2))


def _overlap_t(seq, ncp):
    n_cmp = (seq - CMP_BLOCK) // CMP_STRIDE + 1
    n_sel = seq // SEL_BLOCK
    starts = np.arange(n_cmp) * CMP_STRIDE
    sel_starts = np.arange(n_sel) * SEL_BLOCK
    ov = ((starts[:, None] < sel_starts[None, :] + SEL_BLOCK)
          & (starts[:, None] + CMP_BLOCK > sel_starts[None, :])).astype(np.float32)
    out = np.zeros((n_sel, ncp), np.float32)
    out[:, :n_cmp] = ov.T
    return out, n_cmp, n_sel


def _sel_bias_table(seq, n_sel):
    tab = np.zeros((seq, LANES), np.float32)
    tab[np.arange(seq), np.arange(seq) // SEL_BLOCK] = SEL_BIAS
    return tab


def _gate_expanders():
    group = N_NSA_HEADS // N_NSA_KV
    e = np.zeros((group, LANES, 3 * LANES), np.float32)
    for j in range(group):
        for g in range(N_NSA_KV):
            h = g * group + j
            for r in range(3):
                lo = r * LANES + g * HEAD_DIM
                e[j, 3 * h + r, lo:lo + HEAD_DIM] = 1.0
    return e


def _blockdiag(blocks):
    n = len(blocks)
    r, c = blocks[0].shape
    out = jnp.zeros((n * r, n * c), blocks[0].dtype)
    for i, blk in enumerate(blocks):
        out = out.at[i * r:(i + 1) * r, i * c:(i + 1) * c].set(blk)
    return out


def _compress_weights(pe_k, pe_v, wk1, wk2, wv1, wv2):
    hop = CMP_STRIDE
    lblk = CMP_BLOCK

    nblk = 2 * N_NSA_KV
    stacked = jnp.stack([wk1] * N_NSA_KV + [wv1] * N_NSA_KV, axis=1)
    bd = jnp.einsum('lkdc,kj->lkdjc', stacked, jnp.eye(nblk, dtype=stacked.dtype))
    bd = bd.reshape(lblk, nblk * HEAD_DIM, nblk * HEAD_DIM).astype(BF16)
    pe = jnp.concatenate([pe_k] * N_NSA_KV + [pe_v] * N_NSA_KV, axis=1).astype(F32)

    def first_layer(lo):
        return bd[lo:lo + hop].reshape(hop * nblk * HEAD_DIM, nblk * HEAD_DIM)

    def pe_row(lo):
        return pe[lo:lo + hop].reshape(1, hop * nblk * HEAD_DIM)

    assert lblk == 2 * hop
    w2k = _blockdiag([wk2] * N_NSA_KV).astype(BF16)
    w2vt = _blockdiag([wv2.T] * N_NSA_KV).astype(BF16)
    return pe_row(0), pe_row(hop), first_layer(0), first_layer(hop), w2k, w2vt


def kernel(x, mem, norm_mix, w_in, b_forget, cmp_pe_k, cmp_pe_v, cmp_wk1, cmp_wk2, cmp_wv1, cmp_wv2, w_out, norm_cross, norm_mem, w_mq, w_mk, w_mv, w_mo, norm_ffn, w_up, conv_w, conv_b, w_down, norm_final):
    bsz, seq, d = x.shape
    depth = w_in.shape[0]
    t = bsz * seq
    dff = w_down.shape[1]
    assert seq % 512 == 0 and d % LANES == 0

    cols, n_in = _in_column_order()
    assert n_in == w_in.shape[2]
    col_ok = jnp.asarray(cols >= 0)[None, :]
    col_src = jnp.asarray(np.maximum(cols, 0))
    out_rows = jnp.asarray(_nsa_out_rows())
    ctab, s1tab, s2tab = _rope_tables(seq)
    ncp = seq // CMP_STRIDE
    ovt_np, n_cmp, n_sel = _overlap_t(seq, ncp)
    ovt = jnp.asarray(ovt_np, BF16)
    eneg = jnp.asarray(_sel_bias_table(seq, n_sel), BF16)
    later = jnp.asarray(np.tril(np.ones((SUB, SUB), np.float32), -1), BF16)
    egate = jnp.asarray(_gate_expanders(), BF16)
    nsa_w = N_NSA_HEADS * HEAD_DIM
    sb_w = N_SB_HEADS * HEAD_DIM
    bf_row = jnp.zeros((depth, 1, LANES), F32).at[:, 0, FL_LANE:FL_LANE + FL_COPIES * N_FOX_HEADS].set(
        jnp.tile(b_forget, (1, FL_COPIES)))

    xs = x.reshape(t, d)
    for i in range(depth):
        w = jnp.where(col_ok, jnp.take(w_in[i], col_src, axis=1), 0.0).astype(BF16)
        main, kcv, small = _proj(xs, norm_mix[i][None, :], w, ctab, s1tab, s2tab, seq, ROW_TILE)

        fox_qa, fox_ka = _gates(small.reshape(bsz, seq, LANES), bf_row[i])

        pea, peb, wa, wb, w2k, w2vt = _compress_weights(
            cmp_pe_k[i], cmp_pe_v[i], cmp_wk1[i], cmp_wk2[i], cmp_wv1[i], cmp_wv2[i])
        kc, vct = _compress(kcv.reshape(bsz, ncp, CMP_STRIDE * 2 * LANES), pea, peb, wa, wb, w2k, w2vt)

        ocmp, nm = _nsa_cmp(main, kc, vct, ovt, bsz, seq, CMP_Q_BLOCK, n_cmp, n_sel)
        o_nsa = _nsa_main(main, nm, eneg, small, egate, ocmp, bsz, seq)
        o_sb = _sb(main, later, bsz, seq)
        o_fox = _fox(main, fox_qa, fox_ka, bsz, seq)

        km, vm = _mem_kv(mem, norm_mem[i][None, :], w_mk[i].astype(BF16), w_mv[i].astype(BF16))
        wo = w_out[i]
        xs = _cross(xs, o_nsa, o_sb, o_fox,
                    jnp.take(wo[:nsa_w], out_rows, axis=0).astype(BF16),
                    wo[nsa_w:nsa_w + sb_w].astype(BF16), wo[nsa_w + sb_w:].astype(BF16),
                    norm_cross[i][None, :], w_mq[i].astype(BF16), km, vm, w_mo[i].astype(BF16), seq, CROSS_TILE)
        xs = _ffn(xs, norm_ffn[i][None, :], w_up[i].astype(BF16), conv_w[i], conv_b[i][None, :],
                  w_down[i].astype(BF16), norm_final[None, :], seq, ROW_TILE, FFN_TF, i == depth - 1)
    return xs.reshape(bsz, seq, d)
```

```python
import functools

import numpy as np
import jax
import jax.numpy as jnp
from jax import lax
from jax.experimental import pallas as pl
from jax.experimental.pallas import tpu as pltpu

N_NSA_HEADS = 8
N_NSA_KV = 2
N_SB_HEADS = 4
N_FOX_HEADS = 4
HEAD_DIM = 64
ROPE_DIM = 16
ROPE_THETA = 500000.0
CMP_BLOCK = 32
CMP_STRIDE = 16
SEL_BLOCK = 64
SEL_TOPK = 16
WINDOW = 512
MEM_HEAD_DIM = 64
CONV_WIDTH = 3
EPS = 1e-6

LANES = 128
Q_BLOCK = 512
KEY_BLOCK = 512
SUB = 128
FL_LANE = 3 * N_NSA_HEADS
FL_COPIES = 6
ROW_TILE = 512
CROSS_TILE = 1024
CMP_Q_BLOCK = 256
NSA_TILES = 4
NEG_MASK = -1e30
SEL_BIAS = -(2.0 ** 30)
VMEM_LIMIT = 48 * 1024 * 1024

F32 = jnp.float32
BF16 = jnp.bfloat16

T_QN, T_KS, T_VS, T_KW, T_VW = 0, 4, 5, 6, 7
T_QS, T_KSB, T_VSB = 8, 10, 12
T_QF, T_KF, T_VF = 14, 16, 18
N_MAIN_TILES = 20
ROPE_MAIN_TILES = (0, 1, 2, 3, T_KS, T_KW)

_NT = (((1,), (1,)), ((), ()))


def _params(n_grid):
    return pltpu.CompilerParams(dimension_semantics=("arbitrary",) * n_grid,
                                vmem_limit_bytes=VMEM_LIMIT)


def _rms(xf, g):
    return xf * lax.rsqrt(jnp.mean(xf * xf, axis=-1, keepdims=True) + EPS) * g


def _sigmoid(x):
    return 1.0 / (1.0 + jnp.exp(-x))


def _log_sigmoid(x):
    return jnp.minimum(x, 0.0) - jnp.log(1.0 + jnp.exp(-jnp.abs(x)))


def _dot(a, b):
    return jnp.dot(a, b, preferred_element_type=F32)


def _dot_nt(a, b):
    return lax.dot_general(a, b, _NT, preferred_element_type=F32)


def _split2(x):
    hi = x.astype(BF16)
    lo = (x - hi.astype(F32)).astype(BF16)
    return hi, lo


def _proj_kernel(x_ref, g_ref, w_ref, c_ref, s1_ref, s2_ref, main_ref, kcv_ref, small_ref):
    h = _rms(x_ref[...], g_ref[...]).astype(BF16)
    cos = c_ref[...]
    sin_lo = s1_ref[...]
    sin_hi = s2_ref[...]

    def rope(a):
        return a * cos + pltpu.roll(a, LANES - 8, 1) * sin_lo + pltpu.roll(a, 8, 1) * sin_hi

    for c in range(N_MAIN_TILES // 2):
        acc = _dot(h, w_ref[:, 2 * c * LANES:(2 * c + 2) * LANES])
        for k in range(2):
            t = 2 * c + k
            a = acc[:, k * LANES:(k + 1) * LANES]
            if t in ROPE_MAIN_TILES:
                a = rope(a)
            main_ref[:, t * LANES:(t + 1) * LANES] = a.astype(BF16)
    base = N_MAIN_TILES * LANES
    acc = _dot(h, w_ref[:, base:base + 2 * LANES])
    kcv_ref[:, :LANES] = rope(acc[:, :LANES])
    kcv_ref[:, LANES:] = acc[:, LANES:]
    small_ref[...] = _dot(h, w_ref[:, base + 2 * LANES:base + 3 * LANES])


def _proj(x2d, g, w, ctab, s1tab, s2tab, seq, tm):
    t = x2d.shape[0]
    d = x2d.shape[1]
    ncol = w.shape[1]
    spt = seq // tm
    tab = pl.BlockSpec((tm, LANES), lambda i: (i % spt, 0))
    return pl.pallas_call(
        _proj_kernel,
        grid=(t // tm,),
        in_specs=[pl.BlockSpec((tm, d), lambda i: (i, 0)),
                  pl.BlockSpec((1, d), lambda i: (0, 0)),
                  pl.BlockSpec((d, ncol), lambda i: (0, 0)),
                  tab, tab, tab],
        out_specs=[pl.BlockSpec((tm, N_MAIN_TILES * LANES), lambda i: (i, 0)),
                   pl.BlockSpec((tm, 2 * LANES), lambda i: (i, 0)),
                   pl.BlockSpec((tm, LANES), lambda i: (i, 0))],
        out_shape=[jax.ShapeDtypeStruct((t, N_MAIN_TILES * LANES), BF16),
                   jax.ShapeDtypeStruct((t, 2 * LANES), F32),
                   jax.ShapeDtypeStruct((t, LANES), F32)],
        compiler_params=_params(1),
        name="proj",
    )(x2d, g, w, ctab, s1tab, s2tab)


def _gates_kernel(s_ref, bf_ref, qa_ref, ka_ref):
    lf = _log_sigmoid(s_ref[...] + bf_ref[...])
    seq = lf.shape[0]
    row = lax.broadcasted_iota(jnp.int32, lf.shape, 0)
    sh = 1
    while sh < seq:
        lf = lf + jnp.where(row >= sh, pltpu.roll(lf, sh, 0), 0.0)
        sh *= 2
    hi = lf.astype(BF16).astype(F32)
    r1 = lf - hi
    mid = r1.astype(BF16).astype(F32)
    lo = (r1 - mid).astype(BF16).astype(F32)
    lane = lax.broadcasted_iota(jnp.int32, lf.shape, 1)

    def group(i):
        return (lane >= FL_LANE + i * N_FOX_HEADS) & (lane < FL_LANE + (i + 1) * N_FOX_HEADS)

    piece = jnp.where(group(0) | group(3), hi, jnp.where(group(1) | group(4), mid, lo))
    in_t = group(0) | group(1) | group(2)
    in_s = group(3) | group(4) | group(5)
    qa_ref[...] = jnp.where(in_t, piece, jnp.where(in_s, 1.0, 0.0)).astype(BF16)
    ka_ref[...] = jnp.where(in_t, 1.0, jnp.where(in_s, -piece, 0.0)).astype(BF16)


def _gates(small3, bf):
    b, seq, _ = small3.shape
    return pl.pallas_call(
        _gates_kernel,
        grid=(b,),
        in_specs=[pl.BlockSpec((None, seq, LANES), lambda i: (i, 0, 0)),
                  pl.BlockSpec((1, LANES), lambda i: (0, 0))],
        out_specs=[pl.BlockSpec((None, seq, LANES), lambda i: (i, 0, 0)),
                   pl.BlockSpec((None, seq, LANES), lambda i: (i, 0, 0))],
        out_shape=[jax.ShapeDtypeStruct((b, seq, LANES), BF16),
                   jax.ShapeDtypeStruct((b, seq, LANES), BF16)],
        compiler_params=_params(1),
        name="gates",
    )(small3, bf)


def _compress_kernel(r_ref, pea_ref, peb_ref, wa_ref, wb_ref, w2k_ref, w2vt_ref, kc_ref, vct_ref):
    r = r_ref[...]
    a = _dot((r + pea_ref[...]).astype(BF16), wa_ref[...])
    b = _dot((r + peb_ref[...]).astype(BF16), wb_ref[...])
    nrow = r.shape[0]
    hp = a + pltpu.roll(b, nrow - 1, 0)
    hid = (hp * _sigmoid(hp)).astype(BF16)
    kc_ref[...] = _dot(hid[:, :LANES], w2k_ref[...]).astype(BF16)
    vct_ref[...] = _dot_nt(w2vt_ref[...], hid[:, LANES:]).astype(BF16)


def _compress(r3, pea, peb, wa, wb, w2k, w2vt):
    b, nrow, width = r3.shape
    const = lambda i: (0, 0)
    return pl.pallas_call(
        _compress_kernel,
        grid=(b,),
        in_specs=[pl.BlockSpec((None, nrow, width), lambda i: (i, 0, 0)),
                  pl.BlockSpec((1, width), const), pl.BlockSpec((1, width), const),
                  pl.BlockSpec((width, 2 * LANES), const), pl.BlockSpec((width, 2 * LANES), const),
                  pl.BlockSpec((LANES, LANES), const), pl.BlockSpec((LANES, LANES), const)],
        out_specs=[pl.BlockSpec((None, nrow, LANES), lambda i: (i, 0, 0)),
                   pl.BlockSpec((None, LANES, nrow), lambda i: (i, 0, 0))],
        out_shape=[jax.ShapeDtypeStruct((b, nrow, LANES), BF16),
                   jax.ShapeDtypeStruct((b, LANES, nrow), BF16)],
        compiler_params=_params(1),
        name="compress",
    )(r3, pea, peb, wa, wb, w2k, w2vt)


def _nsa_cmp_kernel(q_ref, kc_ref, vct_ref, ovt_ref, ocmp_ref, nm_ref, *, qb, n_cmp, n_sel):
    qi = pl.program_id(1)
    ncp = kc_ref.shape[0]
    tq = qi * qb + lax.broadcasted_iota(jnp.int32, (ncp, qb), 1)
    nblk = lax.broadcasted_iota(jnp.int32, (ncp, qb), 0)
    cmask = (nblk * CMP_STRIDE + (CMP_BLOCK - 1) <= tq) & (nblk < n_cmp)
    row = lax.broadcasted_iota(jnp.int32, (LANES, qb), 0)
    lane = lax.broadcasted_iota(jnp.int32, (qb, LANES), 1)
    kc = kc_ref[...]
    vct = vct_ref[...]
    psum = [jnp.zeros((ncp, qb), F32), jnp.zeros((ncp, qb), F32)]
    for j in range(N_NSA_HEADS // 2):
        qt = q_ref[:, j * LANES:(j + 1) * LANES].astype(F32) * (HEAD_DIM ** -0.5)
        outs = []
        for half in range(2):
            qm = jnp.where(lane < HEAD_DIM if half == 0 else lane >= HEAD_DIM, qt, 0.0).astype(BF16)
            lt = _dot_nt(kc, qm)
            m = jnp.max(jnp.where(cmask, lt, NEG_MASK), axis=0, keepdims=True)
            p = jnp.where(cmask, jnp.exp(lt - m), 0.0)
            p = p / jnp.maximum(jnp.sum(p, axis=0, keepdims=True), 1e-30)
            psum[half] = psum[half] + p
            outs.append(_dot(vct, p.astype(BF16)))
        ot = jnp.where(row < HEAD_DIM, outs[0], outs[1])
        for s in range(qb // LANES):
            ocmp_ref[s * LANES:(s + 1) * LANES, j * LANES:(j + 1) * LANES] = (
                ot[:, s * LANES:(s + 1) * LANES].T)

    jrow = lax.broadcasted_iota(jnp.int32, (n_sel, qb), 0)
    tsel = qi * qb + lax.broadcasted_iota(jnp.int32, (n_sel, qb), 1)
    cur = tsel // SEL_BLOCK
    forced = (jrow == 0) | (jrow == cur) | (jrow == cur - 1)
    ovt = ovt_ref[...]
    for g in range(N_NSA_KV):
        hi, lo = _split2(psum[g])
        imp = _dot(ovt, hi) + _dot(ovt, lo)
        imp = jnp.where(jrow <= cur, jnp.where(forced, jnp.inf, imp), -jnp.inf)
        before = jnp.zeros((n_sel, qb), F32)
        for i in range(n_sel):
            ri = imp[i:i + 1, :]
            ahead = (ri > imp) | ((ri == imp) & (jrow > i))
            before = before + jnp.where(ahead, 1.0, 0.0)
        member = (before < min(SEL_TOPK, n_sel)) & (imp > -jnp.inf)
        not_member = jnp.where(member, 0.0, 1.0)
        padded = jnp.concatenate([not_member, jnp.zeros((LANES - n_sel, qb), F32)], axis=0)
        for s in range(qb // LANES):
            nm_ref[g, s * LANES:(s + 1) * LANES, :] = (
                padded[:, s * LANES:(s + 1) * LANES].T.astype(BF16))


def _nsa_cmp(main, kc, vct, ovt, bsz, seq, qb, n_cmp, n_sel):
    t = main.shape[0]
    nq = seq // qb
    ncp = kc.shape[1]
    kern = functools.partial(_nsa_cmp_kernel, qb=qb, n_cmp=n_cmp, n_sel=n_sel)
    return pl.pallas_call(
        kern,
        grid=(bsz, nq),
        in_specs=[pl.BlockSpec((qb, 4 * LANES), lambda b, i: (b * nq + i, 0)),
                  pl.BlockSpec((None, ncp, LANES), lambda b, i: (b, 0, 0)),
                  pl.BlockSpec((None, LANES, ncp), lambda b, i: (b, 0, 0)),
                  pl.BlockSpec((n_sel, ncp), lambda b, i: (0, 0))],
        out_specs=[pl.BlockSpec((qb, 4 * LANES), lambda b, i: (b * nq + i, 0)),
                   pl.BlockSpec((None, N_NSA_KV, qb, LANES), lambda b, i: (b, 0, i, 0))],
        out_shape=[jax.ShapeDtypeStruct((t, 4 * LANES), F32),
                   jax.ShapeDtypeStruct((bsz, N_NSA_KV, seq, LANES), BF16)],
        compiler_params=_params(2),
        name="nsa_cmp",
    )(main, kc, vct, ovt)


def _softmax_step(s, mask, vaug, carry):
    m, acc = carry
    if mask is not None:
        s = jnp.where(mask, s, NEG_MASK)
    m_new = jnp.maximum(m, jnp.max(s, axis=-1, keepdims=True))
    p = jnp.exp(s - m_new)
    acc = jnp.exp(m - m_new) * acc + _dot(p.astype(BF16), vaug)
    return m_new, acc


def _softmax_init(qb):
    return (jnp.full((qb, 1), NEG_MASK, F32), jnp.zeros((qb, 2 * LANES), F32))


def _softmax_finish(acc):
    return acc[:, :LANES] / jnp.maximum(acc[:, LANES:], 1e-30)


def _pair_tile(is_a, a, b):
    return jnp.where(is_a, a, b)


def _attn_heads(q_ref):
    qb, width = q_ref.shape
    is_a = lax.broadcasted_iota(jnp.int32, (qb, LANES), 1) < HEAD_DIM
    heads = []
    for t in range(width // LANES):
        qt = q_ref[:, t * LANES:(t + 1) * LANES].astype(F32) * (HEAD_DIM ** -0.5)
        heads.append((t, jnp.where(is_a, qt, 0.0).astype(BF16)))
        heads.append((t, jnp.where(is_a, 0.0, qt).astype(BF16)))
    return is_a, heads


def _expand(x, e):
    hi = x.astype(BF16)
    r1 = x - hi.astype(F32)
    mid = r1.astype(BF16)
    lo = (r1 - mid.astype(F32)).astype(BF16)
    return _dot(hi, e) + _dot(mid, e) + _dot(lo, e)


def _nsa_main_kernel(q_ref, ks_ref, vs_ref, kw_ref, vw_ref, nm_ref, eneg_ref, small_ref, egate_ref,
                     ocmp_ref, o_ref, win_ref, *acc_refs):
    qb = q_ref.shape[0]
    qi = pl.program_id(2)
    is_a, heads = _attn_heads(q_ref)
    nh = len(heads)
    rowi = lax.broadcasted_iota(jnp.int32, (qb, KEY_BLOCK), 0)
    coli = lax.broadcasted_iota(jnp.int32, (qb, KEY_BLOCK), 1)
    qsel = [jnp.concatenate([q, nm_ref[h % 2]], axis=1) for h, (_, q) in enumerate(heads)]
    ones_k = jnp.ones((KEY_BLOCK, LANES), BF16)

    def sel_chunk(c, ms, mask):
        off = pl.multiple_of(c * KEY_BLOCK, KEY_BLOCK)
        k = jnp.concatenate([ks_ref[pl.ds(off, KEY_BLOCK), :], eneg_ref[pl.ds(off, KEY_BLOCK), :]], axis=1)
        v = jnp.concatenate([vs_ref[pl.ds(off, KEY_BLOCK), :], ones_k], axis=1)
        out = []
        for h in range(nh):
            m_new, acc_new = _softmax_step(_dot_nt(qsel[h], k), mask, v, (ms[h], acc_refs[h][...]))
            out.append(m_new)
            acc_refs[h][...] = acc_new
        return tuple(out)

    for h in range(nh):
        acc_refs[h][...] = jnp.zeros((qb, 2 * LANES), F32)
    ms = tuple(jnp.full((qb, 1), NEG_MASK, F32) for _ in range(nh))
    ms = lax.fori_loop(0, qi, lambda c, ms: sel_chunk(c, ms, None), ms)
    sel_chunk(qi, ms, coli <= rowi)

    span = WINDOW + SUB
    rw = lax.broadcasted_iota(jnp.int32, (SUB, span), 0)
    cw = lax.broadcasted_iota(jnp.int32, (SUB, span), 1)
    ones_w = jnp.ones((span, LANES), BF16)
    is_a_slab = lax.broadcasted_iota(jnp.int32, (SUB, LANES), 1) < HEAD_DIM
    for r in range(qb // SUB):
        rows = slice(r * SUB, (r + 1) * SUB)
        t0 = qi * qb + r * SUB
        start = pl.multiple_of(jnp.maximum(t0 - WINDOW, 0), SUB)
        diff = (t0 - start) + rw - cw
        mask = (diff >= 0) & (diff < WINDOW)
        k = kw_ref[pl.ds(start, span), :]
        v = jnp.concatenate([vw_ref[pl.ds(start, span), :], ones_w], axis=1)
        for t in range(nh // 2):
            outs = []
            for half in range(2):
                s = jnp.where(mask, _dot_nt(heads[2 * t + half][1][rows], k), NEG_MASK)
                p = jnp.exp(s - jnp.max(s, axis=-1, keepdims=True))
                outs.append(_softmax_finish(_dot(p.astype(BF16), v)))
            win_ref[rows, t * LANES:(t + 1) * LANES] = _pair_tile(is_a_slab, outs[0], outs[1])

    sig = _sigmoid(small_ref[...])
    for t in range(nh // 2):
        lanes = slice(t * LANES, (t + 1) * LANES)
        sel_t = _pair_tile(is_a, _softmax_finish(acc_refs[2 * t][...]), _softmax_finish(acc_refs[2 * t + 1][...]))
        gates = _expand(sig, egate_ref[t])
        out = (gates[:, :LANES] * ocmp_ref[:, lanes] + gates[:, LANES:2 * LANES] * sel_t
               + gates[:, 2 * LANES:] * win_ref[:, lanes])
        o_ref[:, lanes] = out.astype(BF16)


def _nsa_main(main, nm, eneg, small, egate, ocmp, bsz, seq):
    t = main.shape[0]
    qb = Q_BLOCK
    nq = seq // qb
    ntile = N_NSA_HEADS // 2
    nstep = ntile // NSA_TILES
    w = NSA_TILES * LANES
    kv = lambda tile: pl.BlockSpec((seq, LANES), lambda b, j, i: (b, tile))
    return pl.pallas_call(
        _nsa_main_kernel,
        grid=(bsz, nstep, nq),
        in_specs=[pl.BlockSpec((qb, w), lambda b, j, i: (b * nq + i, T_QN // NSA_TILES + j)),
                  kv(T_KS), kv(T_VS), kv(T_KW), kv(T_VW),
                  pl.BlockSpec((None, N_NSA_KV, qb, LANES), lambda b, j, i: (b, 0, i, 0)),
                  pl.BlockSpec((seq, LANES), lambda b, j, i: (0, 0)),
                  pl.BlockSpec((qb, LANES), lambda b, j, i: (b * nq + i, 0)),
                  pl.BlockSpec((NSA_TILES, LANES, 3 * LANES), lambda b, j, i: (j, 0, 0)),
                  pl.BlockSpec((qb, w), lambda b, j, i: (b * nq + i, j))],
        out_specs=pl.BlockSpec((qb, w), lambda b, j, i: (b * nq + i, j)),
        out_shape=jax.ShapeDtypeStruct((t, ntile * LANES), BF16),
        scratch_shapes=[pltpu.VMEM((qb, w), F32)] + [pltpu.VMEM((qb, 2 * LANES), F32)] * (2 * NSA_TILES),
        compiler_params=_params(3),
        name="nsa_main",
    )(main, main, main, main, main, nm, eneg, small, egate, ocmp)


def _sb_kernel(q_ref, k_ref, v_ref, u_ref, o_ref, *acc_refs):
    qb = q_ref.shape[0]
    qi = pl.program_id(1)
    is_a, heads = _attn_heads(q_ref)
    nh = len(heads)
    rowi = lax.broadcasted_iota(jnp.int32, (qb, KEY_BLOCK), 0)
    coli = lax.broadcasted_iota(jnp.int32, (qb, KEY_BLOCK), 1)
    row_s = lax.broadcasted_iota(jnp.int32, (qb, SUB), 0)
    col_s = lax.broadcasted_iota(jnp.int32, (qb, SUB), 1)
    later = u_ref[...]
    nsub = KEY_BLOCK // SUB

    def chunk(c, tails, diag):
        off = pl.multiple_of(c * KEY_BLOCK, KEY_BLOCK)
        new = []
        for h, (t, q) in enumerate(heads):
            lanes = slice(t * LANES, (t + 1) * LANES)
            k = k_ref[pl.ds(off, KEY_BLOCK), lanes]
            v = v_ref[pl.ds(off, KEY_BLOCK), lanes]
            tail = tails[h]
            z = _dot_nt(q, k)
            log_beta = _log_sigmoid(z)
            log_1m = log_beta - z
            if diag:
                log_1m = jnp.where(coli < rowi, log_1m, 0.0)
            parts = [None] * nsub
            for b in reversed(range(nsub)):
                lo_col, hi_col = b * SUB, (b + 1) * SUB
                x = log_1m[:, lo_col:hi_col]
                excl = _dot(x.astype(BF16), later)
                a = jnp.exp(log_beta[:, lo_col:hi_col] + excl + tail)
                if diag:
                    a = jnp.where(col_s + lo_col < row_s, a, 0.0)
                parts[b] = a.astype(BF16)
                tail = tail + jnp.sum(x, axis=-1, keepdims=True)
            acc_refs[h][...] += _dot(jnp.concatenate(parts, axis=1), v)
            new.append(tail)
        return tuple(new)

    for h in range(nh):
        acc_refs[h][...] = jnp.zeros((qb, LANES), F32)
    tails = chunk(qi, tuple(jnp.zeros((qb, 1), F32) for _ in range(nh)), True)
    lax.fori_loop(0, qi, lambda i, tails: chunk(qi - 1 - i, tails, False), tails)
    for t in range(nh // 2):
        o_ref[:, t * LANES:(t + 1) * LANES] = _pair_tile(
            is_a, acc_refs[2 * t][...], acc_refs[2 * t + 1][...]).astype(BF16)


def _sb(main, later, bsz, seq):
    t = main.shape[0]
    qb = Q_BLOCK
    nq = seq // qb
    w = N_SB_HEADS * HEAD_DIM
    blk = lambda tile: tile * LANES // w
    return pl.pallas_call(
        _sb_kernel,
        grid=(bsz, nq),
        in_specs=[pl.BlockSpec((qb, w), lambda b, i: (b * nq + i, blk(T_QS))),
                  pl.BlockSpec((seq, w), lambda b, i: (b, blk(T_KSB))),
                  pl.BlockSpec((seq, w), lambda b, i: (b, blk(T_VSB))),
                  pl.BlockSpec((SUB, SUB), lambda b, i: (0, 0))],
        out_specs=pl.BlockSpec((qb, w), lambda b, i: (b * nq + i, 0)),
        out_shape=jax.ShapeDtypeStruct((t, w), BF16),
        scratch_shapes=[pltpu.VMEM((qb, LANES), F32)] * N_SB_HEADS,
        compiler_params=_params(2),
        name="sb",
    )(main, main, main, later)


def _fox_kernel(q_ref, k_ref, v_ref, qa_ref, ka_ref, o_ref, *state):
    qb = q_ref.shape[0]
    qi = pl.program_id(1)
    is_a, heads = _attn_heads(q_ref)
    nh = len(heads)
    rowi = lax.broadcasted_iota(jnp.int32, (qb, KEY_BLOCK), 0)
    coli = lax.broadcasted_iota(jnp.int32, (qb, KEY_BLOCK), 1)
    lane = lax.broadcasted_iota(jnp.int32, (qb, LANES), 1)
    qa = qa_ref[...]
    qaug = []
    for h, (_, q) in enumerate(heads):
        assert FL_LANE % N_FOX_HEADS == 0 and N_FOX_HEADS & (N_FOX_HEADS - 1) == 0
        mine = ((lane >= FL_LANE) & (lane < FL_LANE + FL_COPIES * N_FOX_HEADS)
                & ((lane & (N_FOX_HEADS - 1)) == h))
        qaug.append(jnp.concatenate([q, jnp.where(mine, qa, jnp.zeros_like(qa))], axis=1))
    ones_k = jnp.ones((KEY_BLOCK, LANES), BF16)

    m_refs, acc_refs = state[:nh], state[nh:]

    def chunk(c, ms, mask):
        off = pl.multiple_of(c * KEY_BLOCK, KEY_BLOCK)
        ka = ka_ref[pl.ds(off, KEY_BLOCK), :]
        out = []
        for h, (t, _) in enumerate(heads):
            lanes = slice(t * LANES, (t + 1) * LANES)
            k = jnp.concatenate([k_ref[pl.ds(off, KEY_BLOCK), lanes], ka], axis=1)
            v = jnp.concatenate([v_ref[pl.ds(off, KEY_BLOCK), lanes], ones_k], axis=1)
            m_new, acc_new = _softmax_step(_dot_nt(qaug[h], k), mask, v, (ms[h], acc_refs[h][...]))
            out.append(m_new)
            acc_refs[h][...] = acc_new
        return tuple(out)

    for h in range(nh):
        acc_refs[h][...] = jnp.zeros((qb, 2 * LANES), F32)

    ms = tuple(jnp.full((qb, 1), NEG_MASK, F32) for _ in range(nh))
    ms = lax.fori_loop(0, qi, lambda c, ms: chunk(c, ms, None), ms)
    chunk(qi, ms, coli <= rowi)
    for t in range(nh // 2):
        out = _pair_tile(is_a, _softmax_finish(acc_refs[2 * t][...]), _softmax_finish(acc_refs[2 * t + 1][...]))
        o_ref[:, t * LANES:(t + 1) * LANES] = out.astype(BF16)


def _fox(main, qa, ka, bsz, seq):
    t = main.shape[0]
    qb = Q_BLOCK
    nq = seq // qb
    w = N_FOX_HEADS * HEAD_DIM
    blk = lambda tile: tile * LANES // w
    return pl.pallas_call(
        _fox_kernel,
        grid=(bsz, nq),
        in_specs=[pl.BlockSpec((qb, w), lambda b, i: (b * nq + i, blk(T_QF))),
                  pl.BlockSpec((seq, w), lambda b, i: (b, blk(T_KF))),
                  pl.BlockSpec((seq, w), lambda b, i: (b, blk(T_VF))),
                  pl.BlockSpec((None, qb, LANES), lambda b, i: (b, i, 0)),
                  pl.BlockSpec((None, seq, LANES), lambda b, i: (b, 0, 0))],
        out_specs=pl.BlockSpec((qb, w), lambda b, i: (b * nq + i, 0)),
        out_shape=jax.ShapeDtypeStruct((t, w), BF16),
        scratch_shapes=([pltpu.VMEM((qb, 1), F32)] * N_FOX_HEADS
                        + [pltpu.VMEM((qb, 2 * LANES), F32)] * N_FOX_HEADS),
        compiler_params=_params(2),
        name="fox",
    )(main, main, main, qa, ka)


def _mem_kv_kernel(m_ref, g_ref, wk_ref, wv_ref, k_ref, v_ref):
    h = _rms(m_ref[...], g_ref[...]).astype(BF16)
    k_ref[...] = _dot(h, wk_ref[...]).astype(BF16)
    v_ref[...] = _dot(h, wv_ref[...]).astype(BF16)


def _mem_kv(mem, g, wk, wv):
    b, m, d = mem.shape
    mw = wk.shape[1]
    const = lambda i: (0, 0)
    return pl.pallas_call(
        _mem_kv_kernel,
        grid=(b,),
        in_specs=[pl.BlockSpec((None, m, d), lambda i: (i, 0, 0)),
                  pl.BlockSpec((1, d), const), pl.BlockSpec((d, mw), const), pl.BlockSpec((d, mw), const)],
        out_specs=[pl.BlockSpec((None, m, mw), lambda i: (i, 0, 0)),
                   pl.BlockSpec((None, m, mw), lambda i: (i, 0, 0))],
        out_shape=[jax.ShapeDtypeStruct((b, m, mw), BF16), jax.ShapeDtypeStruct((b, m, mw), BF16)],
        compiler_params=_params(1),
        name="mem_kv",
    )(mem, g, wk, wv)


def _cross_kernel(x_ref, on_ref, os_ref, of_ref, won_ref, wos_ref, wof_ref, g_ref, wq_ref,
                  km_ref, vm_ref, wmo_ref, o_ref):
    x1 = (x_ref[...] + _dot(on_ref[...], won_ref[...]) + _dot(os_ref[...], wos_ref[...])
          + _dot(of_ref[...], wof_ref[...]))
    h = _rms(x1, g_ref[...]).astype(BF16)
    q = _dot(h, wq_ref[...]) * (MEM_HEAD_DIM ** -0.5)
    tm = q.shape[0]
    lane = lax.broadcasted_iota(jnp.int32, (tm, LANES), 1)
    is_a = lane < MEM_HEAD_DIM
    tiles = []
    for p in range(q.shape[1] // LANES):
        qt = q[:, p * LANES:(p + 1) * LANES]
        k = km_ref[:, p * LANES:(p + 1) * LANES]
        v = vm_ref[:, p * LANES:(p + 1) * LANES]
        outs = []
        for half in range(2):
            qm = jnp.where(is_a if half == 0 else jnp.logical_not(is_a), qt, 0.0).astype(BF16)
            s = _dot_nt(qm, k)
            e = jnp.exp(s - jnp.max(s, axis=-1, keepdims=True))
            pr = e / jnp.sum(e, axis=-1, keepdims=True)
            outs.append(_dot(pr.astype(BF16), v))
        tiles.append(_pair_tile(is_a, outs[0], outs[1]).astype(BF16))
    attn = jnp.concatenate(tiles, axis=1)
    o_ref[...] = x1 + _dot(attn, wmo_ref[...])


def _cross(x2d, on, osb, ofx, won, wos, wof, g, wq, km, vm, wmo, seq, tm):
    t, d = x2d.shape
    mw = wq.shape[1]
    m = km.shape[1]
    spt = seq // tm
    const = lambda i: (0, 0)
    row = lambda w: pl.BlockSpec((tm, w), lambda i: (i, 0))
    return pl.pallas_call(
        _cross_kernel,
        grid=(t // tm,),
        in_specs=[row(d), row(on.shape[1]), row(osb.shape[1]), row(ofx.shape[1]),
                  pl.BlockSpec(won.shape, const), pl.BlockSpec(wos.shape, const),
                  pl.BlockSpec(wof.shape, const), pl.BlockSpec((1, d), const),
                  pl.BlockSpec((d, mw), const),
                  pl.BlockSpec((None, m, mw), lambda i: (i // spt, 0, 0)),
                  pl.BlockSpec((None, m, mw), lambda i: (i // spt, 0, 0)),
                  pl.BlockSpec((mw, d), const)],
        out_specs=row(d),
        out_shape=jax.ShapeDtypeStruct((t, d), F32),
        compiler_params=_params(1),
        name="cross",
    )(x2d, on, osb, ofx, won, wos, wof, g, wq, km, vm, wmo)


def _ffn_kernel(x_ref, g_ref, wup_ref, cw_ref, cb_ref, wd_ref, gout_ref, o_ref, tail_ref,
                *, tiles_per_seq, norm_out):
    i = pl.program_id(0)
    x = x_ref[...]
    h = _rms(x, g_ref[...]).astype(BF16)
    tm = x.shape[0]
    dff = wd_ref.shape[0]
    rowi = lax.broadcasted_iota(jnp.int32, (tm, dff), 0)

    @pl.when((i % tiles_per_seq) == 0)
    def _():
        tail_ref[...] = jnp.zeros_like(tail_ref)

    def conv(kind):
        cols = slice(kind * dff, (kind + 1) * dff)
        u = _dot(h, wup_ref[:, cols])
        prev = tail_ref[kind]
        u1 = jnp.where(rowi == 0, prev[7:8], pltpu.roll(u, 1, 0))
        u2 = jnp.where(rowi == 0, prev[6:7], jnp.where(rowi == 1, prev[7:8], pltpu.roll(u, 2, 0)))
        tail_ref[kind] = u[tm - 8:, :]
        return cw_ref[2:3, cols] * u + cw_ref[1:2, cols] * u1 + cw_ref[0:1, cols] * u2 + cb_ref[:, cols]

    gate = conv(0)
    val = conv(1)
    act = (gate * _sigmoid(gate) * val).astype(BF16)
    y = x + _dot(act, wd_ref[...])
    o_ref[...] = _rms(y, gout_ref[...]) if norm_out else y


def _ffn(x2d, g, wup, cw, cb, wdown, gout, seq, tm, norm_out):
    t, d = x2d.shape
    dff = wdown.shape[0]
    const = lambda i: (0, 0)
    once = dict(pipeline_mode=pl.Buffered(1))
    return pl.pallas_call(
        functools.partial(_ffn_kernel, tiles_per_seq=seq // tm, norm_out=norm_out),
        grid=(t // tm,),
        in_specs=[pl.BlockSpec((tm, d), lambda i: (i, 0)),
                  pl.BlockSpec((1, d), const),
                  pl.BlockSpec((d, 2 * dff), const, **once),
                  pl.BlockSpec((CONV_WIDTH, 2 * dff), const),
                  pl.BlockSpec((1, 2 * dff), const),
                  pl.BlockSpec((dff, d), const, **once),
                  pl.BlockSpec((1, d), const)],
        out_specs=pl.BlockSpec((tm, d), lambda i: (i, 0)),
        out_shape=jax.ShapeDtypeStruct((t, d), F32),
        scratch_shapes=[pltpu.VMEM((2, 8, dff), F32)],
        compiler_params=_params(1),
        name="ffn",
    )(x2d, g, wup, cw, cb, wdown, gout)


def _in_column_order():
    nq = N_NSA_HEADS * HEAD_DIM
    nkv = N_NSA_KV * HEAD_DIM
    ngate = 3 * N_NSA_HEADS
    sbw = N_SB_HEADS * HEAD_DIM
    fxw = N_FOX_HEADS * HEAD_DIM
    off = {}
    pos = 0
    for name, size in [("qn", nq), ("kc", nkv), ("vc", nkv), ("ks", nkv), ("vs", nkv), ("kw", nkv),
                       ("vw", nkv), ("gn", ngate), ("qs", sbw), ("ksb", sbw), ("vsb", sbw),
                       ("qf", fxw), ("kf", fxw), ("vf", fxw), ("fl", N_FOX_HEADS)]:
        off[name] = (pos, size)
        pos += size
    rng = lambda name: list(range(off[name][0], off[name][0] + off[name][1]))
    group = N_NSA_HEADS // N_NSA_KV
    cols = []
    for j in range(group):
        for g in range(N_NSA_KV):
            h = g * group + j
            cols += list(range(off["qn"][0] + h * HEAD_DIM, off["qn"][0] + (h + 1) * HEAD_DIM))
    for name in ("ks", "vs", "kw", "vw", "qs", "ksb", "vsb", "qf", "kf", "vf", "kc", "vc", "gn"):
        cols += rng(name)
    cols += rng("fl") * FL_COPIES
    cols += [-1] * (LANES - ngate - FL_COPIES * N_FOX_HEADS)
    return np.asarray(cols, np.int32), pos


def _nsa_out_rows():
    group = N_NSA_HEADS // N_NSA_KV
    rows = []
    for j in range(group):
        for g in range(N_NSA_KV):
            h = g * group + j
            rows += list(range(h * HEAD_DIM, (h + 1) * HEAD_DIM))
    return np.asarray(rows, np.int32)


def _rope_tables(seq):
    half = ROPE_DIM // 2
    inv = ROPE_THETA ** (-jnp.arange(half, dtype=F32) / half)
    ang = jnp.arange(seq).astype(F32)[:, None] * inv[None, :]
    cos, sin = jnp.cos(ang), jnp.sin(ang)
    ones = jnp.ones((seq, HEAD_DIM - ROPE_DIM), F32)
    zeros = jnp.zeros((seq, HEAD_DIM - half), F32)
    ctab = jnp.concatenate([cos, cos, ones], axis=1)
    s1 = jnp.concatenate([-sin, zeros], axis=1)
    s2 = jnp.concatenate([jnp.zeros((seq, half), F32), sin, jnp.zeros((seq, HEAD_DIM - ROPE_DIM), F32)], axis=1)
    rep = LANES // HEAD_DIM
    return tuple(jnp.tile(a, (1, rep)) for a in (ctab, s1, s2))


def _overlap_t(seq, ncp):
    n_cmp = (seq - CMP_BLOCK) // CMP_STRIDE + 1
    n_sel = seq // SEL_BLOCK
    starts = np.arange(n_cmp) * CMP_STRIDE
    sel_starts = np.arange(n_sel) * SEL_BLOCK
    ov = ((starts[:, None] < sel_starts[None, :] + SEL_BLOCK)
          & (starts[:, None] + CMP_BLOCK > sel_starts[None, :])).astype(np.float32)
    out = np.zeros((n_sel, ncp), np.float32)
    out[:, :n_cmp] = ov.T
    return out, n_cmp, n_sel


def _sel_bias_table(seq, n_sel):
    tab = np.zeros((seq, LANES), np.float32)
    tab[np.arange(seq), np.arange(seq) // SEL_BLOCK] = SEL_BIAS
    return tab


def _gate_expanders():
    group = N_NSA_HEADS // N_NSA_KV
    e = np.zeros((group, LANES, 3 * LANES), np.float32)
    for j in range(group):
        for g in range(N_NSA_KV):
            h = g * group + j
            for r in range(3):
                lo = r * LANES + g * HEAD_DIM
                e[j, 3 * h + r, lo:lo + HEAD_DIM] = 1.0
    return e


def _blockdiag(blocks):
    n = len(blocks)
    r, c = blocks[0].shape
    out = jnp.zeros((n * r, n * c), blocks[0].dtype)
    for i, blk in enumerate(blocks):
        out = out.at[i * r:(i + 1) * r, i * c:(i + 1) * c].set(blk)
    return out


def _compress_weights(pe_k, pe_v, wk1, wk2, wv1, wv2):
    hop = CMP_STRIDE
    lblk = CMP_BLOCK

    nblk = 2 * N_NSA_KV
    stacked = jnp.stack([wk1] * N_NSA_KV + [wv1] * N_NSA_KV, axis=1)
    bd = jnp.einsum('lkdc,kj->lkdjc', stacked, jnp.eye(nblk, dtype=stacked.dtype))
    bd = bd.reshape(lblk, nblk * HEAD_DIM, nblk * HEAD_DIM).astype(BF16)
    pe = jnp.concatenate([pe_k] * N_NSA_KV + [pe_v] * N_NSA_KV, axis=1).astype(F32)

    def first_layer(lo):
        return bd[lo:lo + hop].reshape(hop * nblk * HEAD_DIM, nblk * HEAD_DIM)

    def pe_row(lo):
        return pe[lo:lo + hop].reshape(1, hop * nblk * HEAD_DIM)

    assert lblk == 2 * hop
    w2k = _blockdiag([wk2] * N_NSA_KV).astype(BF16)
    w2vt = _blockdiag([wv2.T] * N_NSA_KV).astype(BF16)
    return pe_row(0), pe_row(hop), first_layer(0), first_layer(hop), w2k, w2vt


def kernel(x, mem, norm_mix, w_in, b_forget, cmp_pe_k, cmp_pe_v, cmp_wk1, cmp_wk2, cmp_wv1, cmp_wv2, w_out, norm_cross, norm_mem, w_mq, w_mk, w_mv, w_mo, norm_ffn, w_up, conv_w, conv_b, w_down, norm_final):
    bsz, seq, d = x.shape
    depth = w_in.shape[0]
    t = bsz * seq
    dff = w_down.shape[1]
    assert seq % 512 == 0 and d % LANES == 0

    cols, n_in = _in_column_order()
    assert n_in == w_in.shape[2]
    col_ok = jnp.asarray(cols >= 0)[None, :]
    col_src = jnp.asarray(np.maximum(cols, 0))
    out_rows = jnp.asarray(_nsa_out_rows())
    ctab, s1tab, s2tab = _rope_tables(seq)
    ncp = seq // CMP_STRIDE
    ovt_np, n_cmp, n_sel = _overlap_t(seq, ncp)
    ovt = jnp.asarray(ovt_np, BF16)
    eneg = jnp.asarray(_sel_bias_table(seq, n_sel), BF16)
    later = jnp.asarray(np.tril(np.ones((SUB, SUB), np.float32), -1), BF16)
    egate = jnp.asarray(_gate_expanders(), BF16)
    nsa_w = N_NSA_HEADS * HEAD_DIM
    sb_w = N_SB_HEADS * HEAD_DIM
    bf_row = jnp.zeros((depth, 1, LANES), F32).at[:, 0, FL_LANE:FL_LANE + FL_COPIES * N_FOX_HEADS].set(
        jnp.tile(b_forget, (1, FL_COPIES)))

    xs = x.reshape(t, d)
    for i in range(depth):
        w = jnp.where(col_ok, jnp.take(w_in[i], col_src, axis=1), 0.0).astype(BF16)
        main, kcv, small = _proj(xs, norm_mix[i][None, :], w, ctab, s1tab, s2tab, seq, ROW_TILE)

        fox_qa, fox_ka = _gates(small.reshape(bsz, seq, LANES), bf_row[i])

        pea, peb, wa, wb, w2k, w2vt = _compress_weights(
            cmp_pe_k[i], cmp_pe_v[i], cmp_wk1[i], cmp_wk2[i], cmp_wv1[i], cmp_wv2[i])
        kc, vct = _compress(kcv.reshape(bsz, ncp, CMP_STRIDE * 2 * LANES), pea, peb, wa, wb, w2k, w2vt)

        ocmp, nm = _nsa_cmp(main, kc, vct, ovt, bsz, seq, CMP_Q_BLOCK, n_cmp, n_sel)
        o_nsa = _nsa_main(main, nm, eneg, small, egate, ocmp, bsz, seq)
        o_sb = _sb(main, later, bsz, seq)
        o_fox = _fox(main, fox_qa, fox_ka, bsz, seq)

        km, vm = _mem_kv(mem, norm_mem[i][None, :], w_mk[i].astype(BF16), w_mv[i].astype(BF16))
        wo = w_out[i]
        xs = _cross(xs, o_nsa, o_sb, o_fox,
                    jnp.take(wo[:nsa_w], out_rows, axis=0).astype(BF16),
                    wo[nsa_w:nsa_w + sb_w].astype(BF16), wo[nsa_w + sb_w:].astype(BF16),
                    norm_cross[i][None, :], w_mq[i].astype(BF16), km, vm, w_mo[i].astype(BF16), seq, CROSS_TILE)
        xs = _ffn(xs, norm_ffn[i][None, :], w_up[i].astype(BF16), conv_w[i], conv_b[i][None, :],
                  w_down[i].astype(BF16), norm_final[None, :], seq, ROW_TILE, i == depth - 1)
    return xs.reshape(bsz, seq, d)
```

```python
import functools

import numpy as np
import jax
import jax.numpy as jnp
from jax import lax
from jax.experimental import pallas as pl
from jax.experimental.pallas import tpu as pltpu

N_NSA_HEADS = 8
N_NSA_KV = 2
N_SB_HEADS = 4
N_FOX_HEADS = 4
HEAD_DIM = 64
ROPE_DIM = 16
ROPE_THETA = 500000.0
CMP_BLOCK = 32
CMP_STRIDE = 16
SEL_BLOCK = 64
SEL_TOPK = 16
WINDOW = 512
MEM_HEAD_DIM = 64
CONV_WIDTH = 3
EPS = 1e-6

LANES = 128
Q_BLOCK = 512
KEY_BLOCK = 512
SUB = 256
WIN_SLAB = 128
FL_LANE = 3 * N_NSA_HEADS
FL_COPIES = 6
ROW_TILE = 512
CROSS_TILE = 1024
CMP_Q_BLOCK = 256
NSA_TILES = 4
NEG_MASK = -1e30
SEL_BIAS = -(2.0 ** 30)
VMEM_LIMIT = 48 * 1024 * 1024

F32 = jnp.float32
BF16 = jnp.bfloat16

T_QN, T_KS, T_VS, T_KW, T_VW = 0, 4, 5, 6, 7
T_QS, T_KSB, T_VSB = 8, 10, 12
T_QF, T_KF, T_VF = 14, 16, 18
N_MAIN_TILES = 20
ROPE_MAIN_TILES = (0, 1, 2, 3, T_KS, T_KW)

_NT = (((1,), (1,)), ((), ()))


def _params(n_grid):
    return pltpu.CompilerParams(dimension_semantics=("arbitrary",) * n_grid,
                                vmem_limit_bytes=VMEM_LIMIT)


def _rms(xf, g):
    return xf * lax.rsqrt(jnp.mean(xf * xf, axis=-1, keepdims=True) + EPS) * g


def _sigmoid(x):
    return 1.0 / (1.0 + jnp.exp(-x))


def _log_sigmoid(x):
    return jnp.minimum(x, 0.0) - jnp.log(1.0 + jnp.exp(-jnp.abs(x)))


def _dot(a, b):
    return jnp.dot(a, b, preferred_element_type=F32)


def _dot_nt(a, b):
    return lax.dot_general(a, b, _NT, preferred_element_type=F32)


def _split2(x):
    hi = x.astype(BF16)
    lo = (x - hi.astype(F32)).astype(BF16)
    return hi, lo


def _proj_kernel(x_ref, g_ref, w_ref, c_ref, s1_ref, s2_ref, main_ref, kcv_ref, small_ref):
    h = _rms(x_ref[...], g_ref[...]).astype(BF16)
    cos = c_ref[...]
    sin_lo = s1_ref[...]
    sin_hi = s2_ref[...]

    def rope(a):
        return a * cos + pltpu.roll(a, LANES - 8, 1) * sin_lo + pltpu.roll(a, 8, 1) * sin_hi

    for c in range(N_MAIN_TILES // 2):
        acc = _dot(h, w_ref[:, 2 * c * LANES:(2 * c + 2) * LANES])
        for k in range(2):
            t = 2 * c + k
            a = acc[:, k * LANES:(k + 1) * LANES]
            if t in ROPE_MAIN_TILES:
                a = rope(a)
            main_ref[:, t * LANES:(t + 1) * LANES] = a.astype(BF16)
    base = N_MAIN_TILES * LANES
    acc = _dot(h, w_ref[:, base:base + 2 * LANES])
    kcv_ref[:, :LANES] = rope(acc[:, :LANES])
    kcv_ref[:, LANES:] = acc[:, LANES:]
    small_ref[...] = _dot(h, w_ref[:, base + 2 * LANES:base + 3 * LANES])


def _proj(x2d, g, w, ctab, s1tab, s2tab, seq, tm):
    t = x2d.shape[0]
    d = x2d.shape[1]
    ncol = w.shape[1]
    spt = seq // tm
    tab = pl.BlockSpec((tm, LANES), lambda i: (i % spt, 0))
    return pl.pallas_call(
        _proj_kernel,
        grid=(t // tm,),
        in_specs=[pl.BlockSpec((tm, d), lambda i: (i, 0)),
                  pl.BlockSpec((1, d), lambda i: (0, 0)),
                  pl.BlockSpec((d, ncol), lambda i: (0, 0)),
                  tab, tab, tab],
        out_specs=[pl.BlockSpec((tm, N_MAIN_TILES * LANES), lambda i: (i, 0)),
                   pl.BlockSpec((tm, 2 * LANES), lambda i: (i, 0)),
                   pl.BlockSpec((tm, LANES), lambda i: (i, 0))],
        out_shape=[jax.ShapeDtypeStruct((t, N_MAIN_TILES * LANES), BF16),
                   jax.ShapeDtypeStruct((t, 2 * LANES), F32),
                   jax.ShapeDtypeStruct((t, LANES), F32)],
        compiler_params=_params(1),
        name="proj",
    )(x2d, g, w, ctab, s1tab, s2tab)


def _gates_kernel(s_ref, bf_ref, qa_ref, ka_ref):
    lf = _log_sigmoid(s_ref[...] + bf_ref[...])
    seq = lf.shape[0]
    row = lax.broadcasted_iota(jnp.int32, lf.shape, 0)
    sh = 1
    while sh < seq:
        lf = lf + jnp.where(row >= sh, pltpu.roll(lf, sh, 0), 0.0)
        sh *= 2
    hi = lf.astype(BF16).astype(F32)
    r1 = lf - hi
    mid = r1.astype(BF16).astype(F32)
    lo = (r1 - mid).astype(BF16).astype(F32)
    lane = lax.broadcasted_iota(jnp.int32, lf.shape, 1)

    def group(i):
        return (lane >= FL_LANE + i * N_FOX_HEADS) & (lane < FL_LANE + (i + 1) * N_FOX_HEADS)

    piece = jnp.where(group(0) | group(3), hi, jnp.where(group(1) | group(4), mid, lo))
    in_t = group(0) | group(1) | group(2)
    in_s = group(3) | group(4) | group(5)
    qa_ref[...] = jnp.where(in_t, piece, jnp.where(in_s, 1.0, 0.0)).astype(BF16)
    ka_ref[...] = jnp.where(in_t, 1.0, jnp.where(in_s, -piece, 0.0)).astype(BF16)


def _gates(small3, bf):
    b, seq, _ = small3.shape
    return pl.pallas_call(
        _gates_kernel,
        grid=(b,),
        in_specs=[pl.BlockSpec((None, seq, LANES), lambda i: (i, 0, 0)),
                  pl.BlockSpec((1, LANES), lambda i: (0, 0))],
        out_specs=[pl.BlockSpec((None, seq, LANES), lambda i: (i, 0, 0)),
                   pl.BlockSpec((None, seq, LANES), lambda i: (i, 0, 0))],
        out_shape=[jax.ShapeDtypeStruct((b, seq, LANES), BF16),
                   jax.ShapeDtypeStruct((b, seq, LANES), BF16)],
        compiler_params=_params(1),
        name="gates",
    )(small3, bf)


def _compress_kernel(r_ref, pea_ref, peb_ref, wa_ref, wb_ref, w2k_ref, w2vt_ref, kc_ref, vct_ref):
    r = r_ref[...]
    a = _dot((r + pea_ref[...]).astype(BF16), wa_ref[...])
    b = _dot((r + peb_ref[...]).astype(BF16), wb_ref[...])
    nrow = r.shape[0]
    hp = a + pltpu.roll(b, nrow - 1, 0)
    hid = (hp * _sigmoid(hp)).astype(BF16)
    kc_ref[...] = _dot(hid[:, :LANES], w2k_ref[...]).astype(BF16)
    vct_ref[...] = _dot_nt(w2vt_ref[...], hid[:, LANES:]).astype(BF16)


def _compress(r3, pea, peb, wa, wb, w2k, w2vt):
    b, nrow, width = r3.shape
    const = lambda i: (0, 0)
    return pl.pallas_call(
        _compress_kernel,
        grid=(b,),
        in_specs=[pl.BlockSpec((None, nrow, width), lambda i: (i, 0, 0)),
                  pl.BlockSpec((1, width), const), pl.BlockSpec((1, width), const),
                  pl.BlockSpec((width, 2 * LANES), const), pl.BlockSpec((width, 2 * LANES), const),
                  pl.BlockSpec((LANES, LANES), const), pl.BlockSpec((LANES, LANES), const)],
        out_specs=[pl.BlockSpec((None, nrow, LANES), lambda i: (i, 0, 0)),
                   pl.BlockSpec((None, LANES, nrow), lambda i: (i, 0, 0))],
        out_shape=[jax.ShapeDtypeStruct((b, nrow, LANES), BF16),
                   jax.ShapeDtypeStruct((b, LANES, nrow), BF16)],
        compiler_params=_params(1),
        name="compress",
    )(r3, pea, peb, wa, wb, w2k, w2vt)


def _nsa_cmp_kernel(q_ref, kc_ref, vct_ref, ovt_ref, ocmp_ref, nm_ref, *, qb, n_cmp, n_sel):
    qi = pl.program_id(1)
    ncp = kc_ref.shape[0]
    tq = qi * qb + lax.broadcasted_iota(jnp.int32, (ncp, qb), 1)
    nblk = lax.broadcasted_iota(jnp.int32, (ncp, qb), 0)
    cmask = (nblk * CMP_STRIDE + (CMP_BLOCK - 1) <= tq) & (nblk < n_cmp)
    row = lax.broadcasted_iota(jnp.int32, (LANES, qb), 0)
    lane = lax.broadcasted_iota(jnp.int32, (qb, LANES), 1)
    kc = kc_ref[...]
    vct = vct_ref[...]
    psum = [jnp.zeros((ncp, qb), F32), jnp.zeros((ncp, qb), F32)]
    for j in range(N_NSA_HEADS // 2):
        qt = q_ref[:, j * LANES:(j + 1) * LANES].astype(F32) * (HEAD_DIM ** -0.5)
        outs = []
        for half in range(2):
            qm = jnp.where(lane < HEAD_DIM if half == 0 else lane >= HEAD_DIM, qt, 0.0).astype(BF16)
            lt = _dot_nt(kc, qm)
            m = jnp.max(jnp.where(cmask, lt, NEG_MASK), axis=0, keepdims=True)
            p = jnp.where(cmask, jnp.exp(lt - m), 0.0)
            p = p / jnp.maximum(jnp.sum(p, axis=0, keepdims=True), 1e-30)
            psum[half] = psum[half] + p
            outs.append(_dot(vct, p.astype(BF16)))
        ot = jnp.where(row < HEAD_DIM, outs[0], outs[1])
        for s in range(qb // LANES):
            ocmp_ref[s * LANES:(s + 1) * LANES, j * LANES:(j + 1) * LANES] = (
                ot[:, s * LANES:(s + 1) * LANES].T)

    jrow = lax.broadcasted_iota(jnp.int32, (n_sel, qb), 0)
    tsel = qi * qb + lax.broadcasted_iota(jnp.int32, (n_sel, qb), 1)
    cur = tsel // SEL_BLOCK
    forced = (jrow == 0) | (jrow == cur) | (jrow == cur - 1)
    ovt = ovt_ref[...]
    for g in range(N_NSA_KV):
        hi, lo = _split2(psum[g])
        imp = _dot(ovt, hi) + _dot(ovt, lo)
        imp = jnp.where(jrow <= cur, jnp.where(forced, jnp.inf, imp), -jnp.inf)
        before = jnp.zeros((n_sel, qb), F32)
        for i in range(n_sel):
            ri = imp[i:i + 1, :]
            ahead = (ri > imp) | ((ri == imp) & (jrow > i))
            before = before + jnp.where(ahead, 1.0, 0.0)
        member = (before < min(SEL_TOPK, n_sel)) & (imp > -jnp.inf)
        not_member = jnp.where(member, 0.0, 1.0)
        padded = jnp.concatenate([not_member, jnp.zeros((LANES - n_sel, qb), F32)], axis=0)
        for s in range(qb // LANES):
            nm_ref[g, s * LANES:(s + 1) * LANES, :] = (
                padded[:, s * LANES:(s + 1) * LANES].T.astype(BF16))


def _nsa_cmp(main, kc, vct, ovt, bsz, seq, qb, n_cmp, n_sel):
    t = main.shape[0]
    nq = seq // qb
    ncp = kc.shape[1]
    kern = functools.partial(_nsa_cmp_kernel, qb=qb, n_cmp=n_cmp, n_sel=n_sel)
    return pl.pallas_call(
        kern,
        grid=(bsz, nq),
        in_specs=[pl.BlockSpec((qb, 4 * LANES), lambda b, i: (b * nq + i, 0)),
                  pl.BlockSpec((None, ncp, LANES), lambda b, i: (b, 0, 0)),
                  pl.BlockSpec((None, LANES, ncp), lambda b, i: (b, 0, 0)),
                  pl.BlockSpec((n_sel, ncp), lambda b, i: (0, 0))],
        out_specs=[pl.BlockSpec((qb, 4 * LANES), lambda b, i: (b * nq + i, 0)),
                   pl.BlockSpec((None, N_NSA_KV, qb, LANES), lambda b, i: (b, 0, i, 0))],
        out_shape=[jax.ShapeDtypeStruct((t, 4 * LANES), F32),
                   jax.ShapeDtypeStruct((bsz, N_NSA_KV, seq, LANES), BF16)],
        compiler_params=_params(2),
        name="nsa_cmp",
    )(main, kc, vct, ovt)


def _softmax_step(s, mask, vaug, carry):
    m, acc = carry
    if mask is not None:
        s = jnp.where(mask, s, NEG_MASK)
    m_new = jnp.maximum(m, jnp.max(s, axis=-1, keepdims=True))
    p = jnp.exp(s - m_new)
    acc = jnp.exp(m - m_new) * acc + _dot(p.astype(BF16), vaug)
    return m_new, acc


def _softmax_init(qb):
    return (jnp.full((qb, 1), NEG_MASK, F32), jnp.zeros((qb, 2 * LANES), F32))


def _softmax_finish(acc):
    return acc[:, :LANES] / jnp.maximum(acc[:, LANES:], 1e-30)


def _pair_tile(is_a, a, b):
    return jnp.where(is_a, a, b)


def _attn_heads(q_ref):
    qb, width = q_ref.shape
    is_a = lax.broadcasted_iota(jnp.int32, (qb, LANES), 1) < HEAD_DIM
    heads = []
    for t in range(width // LANES):
        qt = q_ref[:, t * LANES:(t + 1) * LANES].astype(F32) * (HEAD_DIM ** -0.5)
        heads.append((t, jnp.where(is_a, qt, 0.0).astype(BF16)))
        heads.append((t, jnp.where(is_a, 0.0, qt).astype(BF16)))
    return is_a, heads


def _expand(x, e):
    hi, lo = _split2(x)
    return _dot(hi, e) + _dot(lo, e)


def _nsa_main_kernel(q_ref, ks_ref, vs_ref, kw_ref, vw_ref, nm_ref, eneg_ref, small_ref, egate_ref,
                     ocmp_ref, o_ref, win_ref, *acc_refs):
    qb = q_ref.shape[0]
    qi = pl.program_id(2)
    is_a, heads = _attn_heads(q_ref)
    nh = len(heads)
    rowi = lax.broadcasted_iota(jnp.int32, (qb, KEY_BLOCK), 0)
    coli = lax.broadcasted_iota(jnp.int32, (qb, KEY_BLOCK), 1)
    qsel = [jnp.concatenate([q, nm_ref[h % 2]], axis=1) for h, (_, q) in enumerate(heads)]
    ones_k = jnp.ones((KEY_BLOCK, LANES), BF16)

    def sel_chunk(c, ms, mask):
        off = pl.multiple_of(c * KEY_BLOCK, KEY_BLOCK)
        k = jnp.concatenate([ks_ref[pl.ds(off, KEY_BLOCK), :], eneg_ref[pl.ds(off, KEY_BLOCK), :]], axis=1)
        v = jnp.concatenate([vs_ref[pl.ds(off, KEY_BLOCK), :], ones_k], axis=1)
        out = []
        for h in range(nh):
            m_new, acc_new = _softmax_step(_dot_nt(qsel[h], k), mask, v, (ms[h], acc_refs[h][...]))
            out.append(m_new)
            acc_refs[h][...] = acc_new
        return tuple(out)

    for h in range(nh):
        acc_refs[h][...] = jnp.zeros((qb, 2 * LANES), F32)
    ms = tuple(jnp.full((qb, 1), NEG_MASK, F32) for _ in range(nh))
    ms = lax.fori_loop(0, qi, lambda c, ms: sel_chunk(c, ms, None), ms)
    sel_chunk(qi, ms, coli <= rowi)

    span = WINDOW + WIN_SLAB
    rw = lax.broadcasted_iota(jnp.int32, (WIN_SLAB, span), 0)
    cw = lax.broadcasted_iota(jnp.int32, (WIN_SLAB, span), 1)
    ones_w = jnp.ones((span, LANES), BF16)
    is_a_slab = lax.broadcasted_iota(jnp.int32, (WIN_SLAB, LANES), 1) < HEAD_DIM
    for r in range(qb // WIN_SLAB):
        rows = slice(r * WIN_SLAB, (r + 1) * WIN_SLAB)
        t0 = qi * qb + r * WIN_SLAB
        start = pl.multiple_of(jnp.maximum(t0 - WINDOW, 0), WIN_SLAB)
        diff = (t0 - start) + rw - cw
        mask = (diff >= 0) & (diff < WINDOW)
        k = kw_ref[pl.ds(start, span), :]
        v = jnp.concatenate([vw_ref[pl.ds(start, span), :], ones_w], axis=1)
        for t in range(nh // 2):
            outs = []
            for half in range(2):
                s = jnp.where(mask, _dot_nt(heads[2 * t + half][1][rows], k), NEG_MASK)
                p = jnp.exp(s - jnp.max(s, axis=-1, keepdims=True))
                outs.append(_softmax_finish(_dot(p.astype(BF16), v)))
            win_ref[rows, t * LANES:(t + 1) * LANES] = _pair_tile(is_a_slab, outs[0], outs[1])

    sig = _sigmoid(small_ref[...])
    for t in range(nh // 2):
        lanes = slice(t * LANES, (t + 1) * LANES)
        sel_t = _pair_tile(is_a, _softmax_finish(acc_refs[2 * t][...]), _softmax_finish(acc_refs[2 * t + 1][...]))
        gates = _expand(sig, egate_ref[t])
        out = (gates[:, :LANES] * ocmp_ref[:, lanes] + gates[:, LANES:2 * LANES] * sel_t
               + gates[:, 2 * LANES:] * win_ref[:, lanes])
        o_ref[:, lanes] = out.astype(BF16)


def _nsa_main(main, nm, eneg, small, egate, ocmp, bsz, seq):
    t = main.shape[0]
    qb = Q_BLOCK
    nq = seq // qb
    ntile = N_NSA_HEADS // 2
    nstep = ntile // NSA_TILES
    w = NSA_TILES * LANES
    kv = lambda tile: pl.BlockSpec((seq, LANES), lambda b, j, i: (b, tile))
    return pl.pallas_call(
        _nsa_main_kernel,
        grid=(bsz, nstep, nq),
        in_specs=[pl.BlockSpec((qb, w), lambda b, j, i: (b * nq + i, T_QN // NSA_TILES + j)),
                  kv(T_KS), kv(T_VS), kv(T_KW), kv(T_VW),
                  pl.BlockSpec((None, N_NSA_KV, qb, LANES), lambda b, j, i: (b, 0, i, 0)),
                  pl.BlockSpec((seq, LANES), lambda b, j, i: (0, 0)),
                  pl.BlockSpec((qb, LANES), lambda b, j, i: (b * nq + i, 0)),
                  pl.BlockSpec((NSA_TILES, LANES, 3 * LANES), lambda b, j, i: (j, 0, 0)),
                  pl.BlockSpec((qb, w), lambda b, j, i: (b * nq + i, j))],
        out_specs=pl.BlockSpec((qb, w), lambda b, j, i: (b * nq + i, j)),
        out_shape=jax.ShapeDtypeStruct((t, ntile * LANES), BF16),
        scratch_shapes=[pltpu.VMEM((qb, w), F32)] + [pltpu.VMEM((qb, 2 * LANES), F32)] * (2 * NSA_TILES),
        compiler_params=_params(3),
        name="nsa_main",
    )(main, main, main, main, main, nm, eneg, small, egate, ocmp)


def _sb_kernel(q_ref, k_ref, v_ref, u_ref, o_ref, *acc_refs):
    qb = q_ref.shape[0]
    qi = pl.program_id(1)
    is_a, heads = _attn_heads(q_ref)
    nh = len(heads)
    rowi = lax.broadcasted_iota(jnp.int32, (qb, KEY_BLOCK), 0)
    coli = lax.broadcasted_iota(jnp.int32, (qb, KEY_BLOCK), 1)
    row_s = lax.broadcasted_iota(jnp.int32, (qb, SUB), 0)
    col_s = lax.broadcasted_iota(jnp.int32, (qb, SUB), 1)
    later = u_ref[...]
    nsub = KEY_BLOCK // SUB

    def chunk(c, tails, diag):
        off = pl.multiple_of(c * KEY_BLOCK, KEY_BLOCK)
        new = []
        for h, (t, q) in enumerate(heads):
            lanes = slice(t * LANES, (t + 1) * LANES)
            k = k_ref[pl.ds(off, KEY_BLOCK), lanes]
            v = v_ref[pl.ds(off, KEY_BLOCK), lanes]
            tail = tails[h]
            z = _dot_nt(q, k)
            log_beta = _log_sigmoid(z)
            log_1m = log_beta - z
            if diag:
                log_1m = jnp.where(coli < rowi, log_1m, 0.0)
            parts = [None] * nsub
            for b in reversed(range(nsub)):
                lo_col, hi_col = b * SUB, (b + 1) * SUB
                x = log_1m[:, lo_col:hi_col]
                excl = _dot(x.astype(BF16), later)
                a = jnp.exp(log_beta[:, lo_col:hi_col] + excl + tail)
                if diag:
                    a = jnp.where(col_s + lo_col < row_s, a, 0.0)
                parts[b] = a.astype(BF16)
                tail = tail + jnp.sum(x, axis=-1, keepdims=True)
            acc_refs[h][...] += _dot(jnp.concatenate(parts, axis=1), v)
            new.append(tail)
        return tuple(new)

    for h in range(nh):
        acc_refs[h][...] = jnp.zeros((qb, LANES), F32)
    tails = chunk(qi, tuple(jnp.zeros((qb, 1), F32) for _ in range(nh)), True)
    lax.fori_loop(0, qi, lambda i, tails: chunk(qi - 1 - i, tails, False), tails)
    for t in range(nh // 2):
        o_ref[:, t * LANES:(t + 1) * LANES] = _pair_tile(
            is_a, acc_refs[2 * t][...], acc_refs[2 * t + 1][...]).astype(BF16)


def _sb(main, later, bsz, seq):
    t = main.shape[0]
    qb = Q_BLOCK
    nq = seq // qb
    w = N_SB_HEADS * HEAD_DIM
    blk = lambda tile: tile * LANES // w
    return pl.pallas_call(
        _sb_kernel,
        grid=(bsz, nq),
        in_specs=[pl.BlockSpec((qb, w), lambda b, i: (b * nq + i, blk(T_QS))),
                  pl.BlockSpec((seq, w), lambda b, i: (b, blk(T_KSB))),
                  pl.BlockSpec((seq, w), lambda b, i: (b, blk(T_VSB))),
                  pl.BlockSpec((SUB, SUB), lambda b, i: (0, 0))],
        out_specs=pl.BlockSpec((qb, w), lambda b, i: (b * nq + i, 0)),
        out_shape=jax.ShapeDtypeStruct((t, w), BF16),
        scratch_shapes=[pltpu.VMEM((qb, LANES), F32)] * N_SB_HEADS,
        compiler_params=_params(2),
        name="sb",
    )(main, main, main, later)


def _fox_kernel(q_ref, k_ref, v_ref, qa_ref, ka_ref, o_ref, *state):
    qb = q_ref.shape[0]
    qi = pl.program_id(1)
    is_a, heads = _attn_heads(q_ref)
    nh = len(heads)
    rowi = lax.broadcasted_iota(jnp.int32, (qb, KEY_BLOCK), 0)
    coli = lax.broadcasted_iota(jnp.int32, (qb, KEY_BLOCK), 1)
    lane = lax.broadcasted_iota(jnp.int32, (qb, LANES), 1)
    qa = qa_ref[...]
    qaug = []
    for h, (_, q) in enumerate(heads):
        assert FL_LANE % N_FOX_HEADS == 0 and N_FOX_HEADS & (N_FOX_HEADS - 1) == 0
        mine = ((lane >= FL_LANE) & (lane < FL_LANE + FL_COPIES * N_FOX_HEADS)
                & ((lane & (N_FOX_HEADS - 1)) == h))
        qaug.append(jnp.concatenate([q, jnp.where(mine, qa, jnp.zeros_like(qa))], axis=1))
    ones_k = jnp.ones((KEY_BLOCK, LANES), BF16)

    m_refs, acc_refs = state[:nh], state[nh:]

    def chunk(c, ms, mask):
        off = pl.multiple_of(c * KEY_BLOCK, KEY_BLOCK)
        ka = ka_ref[pl.ds(off, KEY_BLOCK), :]
        out = []
        for h, (t, _) in enumerate(heads):
            lanes = slice(t * LANES, (t + 1) * LANES)
            k = jnp.concatenate([k_ref[pl.ds(off, KEY_BLOCK), lanes], ka], axis=1)
            v = jnp.concatenate([v_ref[pl.ds(off, KEY_BLOCK), lanes], ones_k], axis=1)
            m_new, acc_new = _softmax_step(_dot_nt(qaug[h], k), mask, v, (ms[h], acc_refs[h][...]))
            out.append(m_new)
            acc_refs[h][...] = acc_new
        return tuple(out)

    for h in range(nh):
        acc_refs[h][...] = jnp.zeros((qb, 2 * LANES), F32)

    ms = tuple(jnp.full((qb, 1), NEG_MASK, F32) for _ in range(nh))
    ms = lax.fori_loop(0, qi, lambda c, ms: chunk(c, ms, None), ms)
    chunk(qi, ms, coli <= rowi)
    for t in range(nh // 2):
        out = _pair_tile(is_a, _softmax_finish(acc_refs[2 * t][...]), _softmax_finish(acc_refs[2 * t + 1][...]))
        o_ref[:, t * LANES:(t + 1) * LANES] = out.astype(BF16)


def _fox(main, qa, ka, bsz, seq):
    t = main.shape[0]
    qb = Q_BLOCK
    nq = seq // qb
    w = N_FOX_HEADS * HEAD_DIM
    blk = lambda tile: tile * LANES // w
    return pl.pallas_call(
        _fox_kernel,
        grid=(bsz, nq),
        in_specs=[pl.BlockSpec((qb, w), lambda b, i: (b * nq + i, blk(T_QF))),
                  pl.BlockSpec((seq, w), lambda b, i: (b, blk(T_KF))),
                  pl.BlockSpec((seq, w), lambda b, i: (b, blk(T_VF))),
                  pl.BlockSpec((None, qb, LANES), lambda b, i: (b, i, 0)),
                  pl.BlockSpec((None, seq, LANES), lambda b, i: (b, 0, 0))],
        out_specs=pl.BlockSpec((qb, w), lambda b, i: (b * nq + i, 0)),
        out_shape=jax.ShapeDtypeStruct((t, w), BF16),
        scratch_shapes=([pltpu.VMEM((qb, 1), F32)] * N_FOX_HEADS
                        + [pltpu.VMEM((qb, 2 * LANES), F32)] * N_FOX_HEADS),
        compiler_params=_params(2),
        name="fox",
    )(main, main, main, qa, ka)


def _mem_kv_kernel(m_ref, g_ref, wk_ref, wv_ref, k_ref, v_ref):
    h = _rms(m_ref[...], g_ref[...]).astype(BF16)
    k_ref[...] = _dot(h, wk_ref[...]).astype(BF16)
    v_ref[...] = _dot(h, wv_ref[...]).astype(BF16)


def _mem_kv(mem, g, wk, wv):
    b, m, d = mem.shape
    mw = wk.shape[1]
    const = lambda i: (0, 0)
    return pl.pallas_call(
        _mem_kv_kernel,
        grid=(b,),
        in_specs=[pl.BlockSpec((None, m, d), lambda i: (i, 0, 0)),
                  pl.BlockSpec((1, d), const), pl.BlockSpec((d, mw), const), pl.BlockSpec((d, mw), const)],
        out_specs=[pl.BlockSpec((None, m, mw), lambda i: (i, 0, 0)),
                   pl.BlockSpec((None, m, mw), lambda i: (i, 0, 0))],
        out_shape=[jax.ShapeDtypeStruct((b, m, mw), BF16), jax.ShapeDtypeStruct((b, m, mw), BF16)],
        compiler_params=_params(1),
        name="mem_kv",
    )(mem, g, wk, wv)


def _cross_kernel(x_ref, on_ref, os_ref, of_ref, won_ref, wos_ref, wof_ref, g_ref, wq_ref,
                  km_ref, vm_ref, wmo_ref, o_ref):
    x1 = (x_ref[...] + _dot(on_ref[...], won_ref[...]) + _dot(os_ref[...], wos_ref[...])
          + _dot(of_ref[...], wof_ref[...]))
    h = _rms(x1, g_ref[...]).astype(BF16)
    q = _dot(h, wq_ref[...]) * (MEM_HEAD_DIM ** -0.5)
    tm = q.shape[0]
    lane = lax.broadcasted_iota(jnp.int32, (tm, LANES), 1)
    is_a = lane < MEM_HEAD_DIM
    tiles = []
    for p in range(q.shape[1] // LANES):
        qt = q[:, p * LANES:(p + 1) * LANES]
        k = km_ref[:, p * LANES:(p + 1) * LANES]
        v = vm_ref[:, p * LANES:(p + 1) * LANES]
        outs = []
        for half in range(2):
            qm = jnp.where(is_a if half == 0 else jnp.logical_not(is_a), qt, 0.0).astype(BF16)
            s = _dot_nt(qm, k)
            e = jnp.exp(s - jnp.max(s, axis=-1, keepdims=True))
            pr = e / jnp.sum(e, axis=-1, keepdims=True)
            outs.append(_dot(pr.astype(BF16), v))
        tiles.append(_pair_tile(is_a, outs[0], outs[1]).astype(BF16))
    attn = jnp.concatenate(tiles, axis=1)
    o_ref[...] = x1 + _dot(attn, wmo_ref[...])


def _cross(x2d, on, osb, ofx, won, wos, wof, g, wq, km, vm, wmo, seq, tm):
    t, d = x2d.shape
    mw = wq.shape[1]
    m = km.shape[1]
    spt = seq // tm
    const = lambda i: (0, 0)
    row = lambda w: pl.BlockSpec((tm, w), lambda i: (i, 0))
    return pl.pallas_call(
        _cross_kernel,
        grid=(t // tm,),
        in_specs=[row(d), row(on.shape[1]), row(osb.shape[1]), row(ofx.shape[1]),
                  pl.BlockSpec(won.shape, const), pl.BlockSpec(wos.shape, const),
                  pl.BlockSpec(wof.shape, const), pl.BlockSpec((1, d), const),
                  pl.BlockSpec((d, mw), const),
                  pl.BlockSpec((None, m, mw), lambda i: (i // spt, 0, 0)),
                  pl.BlockSpec((None, m, mw), lambda i: (i // spt, 0, 0)),
                  pl.BlockSpec((mw, d), const)],
        out_specs=row(d),
        out_shape=jax.ShapeDtypeStruct((t, d), F32),
        compiler_params=_params(1),
        name="cross",
    )(x2d, on, osb, ofx, won, wos, wof, g, wq, km, vm, wmo)


def _ffn_kernel(x_ref, g_ref, wup_ref, cw_ref, cb_ref, wd_ref, gout_ref, o_ref, tail_ref,
                *, tiles_per_seq, norm_out):
    i = pl.program_id(0)
    x = x_ref[...]
    h = _rms(x, g_ref[...]).astype(BF16)
    tm = x.shape[0]
    dff = wd_ref.shape[0]
    rowi = lax.broadcasted_iota(jnp.int32, (tm, dff), 0)

    @pl.when((i % tiles_per_seq) == 0)
    def _():
        tail_ref[...] = jnp.zeros_like(tail_ref)

    def conv(kind):
        cols = slice(kind * dff, (kind + 1) * dff)
        u = _dot(h, wup_ref[:, cols])
        prev = tail_ref[kind]
        u1 = jnp.where(rowi == 0, prev[7:8], pltpu.roll(u, 1, 0))
        u2 = jnp.where(rowi == 0, prev[6:7], jnp.where(rowi == 1, prev[7:8], pltpu.roll(u, 2, 0)))
        tail_ref[kind] = u[tm - 8:, :]
        return cw_ref[2:3, cols] * u + cw_ref[1:2, cols] * u1 + cw_ref[0:1, cols] * u2 + cb_ref[:, cols]

    gate = conv(0)
    val = conv(1)
    act = (gate * _sigmoid(gate) * val).astype(BF16)
    y = x + _dot(act, wd_ref[...])
    o_ref[...] = _rms(y, gout_ref[...]) if norm_out else y


def _ffn(x2d, g, wup, cw, cb, wdown, gout, seq, tm, norm_out):
    t, d = x2d.shape
    dff = wdown.shape[0]
    const = lambda i: (0, 0)
    once = dict(pipeline_mode=pl.Buffered(1))
    return pl.pallas_call(
        functools.partial(_ffn_kernel, tiles_per_seq=seq // tm, norm_out=norm_out),
        grid=(t // tm,),
        in_specs=[pl.BlockSpec((tm, d), lambda i: (i, 0)),
                  pl.BlockSpec((1, d), const),
                  pl.BlockSpec((d, 2 * dff), const, **once),
                  pl.BlockSpec((CONV_WIDTH, 2 * dff), const),
                  pl.BlockSpec((1, 2 * dff), const),
                  pl.BlockSpec((dff, d), const, **once),
                  pl.BlockSpec((1, d), const)],
        out_specs=pl.BlockSpec((tm, d), lambda i: (i, 0)),
        out_shape=jax.ShapeDtypeStruct((t, d), F32),
        scratch_shapes=[pltpu.VMEM((2, 8, dff), F32)],
        compiler_params=_params(1),
        name="ffn",
    )(x2d, g, wup, cw, cb, wdown, gout)


def _in_column_order():
    nq = N_NSA_HEADS * HEAD_DIM
    nkv = N_NSA_KV * HEAD_DIM
    ngate = 3 * N_NSA_HEADS
    sbw = N_SB_HEADS * HEAD_DIM
    fxw = N_FOX_HEADS * HEAD_DIM
    off = {}
    pos = 0
    for name, size in [("qn", nq), ("kc", nkv), ("vc", nkv), ("ks", nkv), ("vs", nkv), ("kw", nkv),
                       ("vw", nkv), ("gn", ngate), ("qs", sbw), ("ksb", sbw), ("vsb", sbw),
                       ("qf", fxw), ("kf", fxw), ("vf", fxw), ("fl", N_FOX_HEADS)]:
        off[name] = (pos, size)
        pos += size
    rng = lambda name: list(range(off[name][0], off[name][0] + off[name][1]))
    group = N_NSA_HEADS // N_NSA_KV
    cols = []
    for j in range(group):
        for g in range(N_NSA_KV):
            h = g * group + j
            cols += list(range(off["qn"][0] + h * HEAD_DIM, off["qn"][0] + (h + 1) * HEAD_DIM))
    for name in ("ks", "vs", "kw", "vw", "qs", "ksb", "vsb", "qf", "kf", "vf", "kc", "vc", "gn"):
        cols += rng(name)
    cols += rng("fl") * FL_COPIES
    cols += [-1] * (LANES - ngate - FL_COPIES * N_FOX_HEADS)
    return np.asarray(cols, np.int32), pos


def _nsa_out_rows():
    group = N_NSA_HEADS // N_NSA_KV
    rows = []
    for j in range(group):
        for g in range(N_NSA_KV):
            h = g * group + j
            rows += list(range(h * HEAD_DIM, (h + 1) * HEAD_DIM))
    return np.asarray(rows, np.int32)


def _rope_tables(seq):
    half = ROPE_DIM // 2
    inv = ROPE_THETA ** (-jnp.arange(half, dtype=F32) / half)
    ang = jnp.arange(seq).astype(F32)[:, None] * inv[None, :]
    cos, sin = jnp.cos(ang), jnp.sin(ang)
    ones = jnp.ones((seq, HEAD_DIM - ROPE_DIM), F32)
    zeros = jnp.zeros((seq, HEAD_DIM - half), F32)
    ctab = jnp.concatenate([cos, cos, ones], axis=1)
    s1 = jnp.concatenate([-sin, zeros], axis=1)
    s2 = jnp.concatenate([jnp.zeros((seq, half), F32), sin, jnp.zeros((seq, HEAD_DIM - ROPE_DIM), F32)], axis=1)
    rep = LANES // HEAD_DIM
    return tuple(jnp.tile(a, (1, rep)) for a in (ctab, s1, s2))


def _overlap_t(seq, ncp):
    n_cmp = (seq - CMP_BLOCK) // CMP_STRIDE + 1
    n_sel = seq // SEL_BLOCK
    starts = np.arange(n_cmp) * CMP_STRIDE
    sel_starts = np.arange(n_sel) * SEL_BLOCK
    ov = ((starts[:, None] < sel_starts[None, :] + SEL_BLOCK)
          & (starts[:, None] + CMP_BLOCK > sel_starts[None, :])).astype(np.float32)
    out = np.zeros((n_sel, ncp), np.float32)
    out[:, :n_cmp] = ov.T
    return out, n_cmp, n_sel


def _sel_bias_table(seq, n_sel):
    tab = np.zeros((seq, LANES), np.float32)
    tab[np.arange(seq), np.arange(seq) // SEL_BLOCK] = SEL_BIAS
    return tab


def _gate_expanders():
    group = N_NSA_HEADS // N_NSA_KV
    e = np.zeros((group, LANES, 3 * LANES), np.float32)
    for j in range(group):
        for g in range(N_NSA_KV):
            h = g * group + j
            for r in range(3):
                lo = r * LANES + g * HEAD_DIM
                e[j, 3 * h + r, lo:lo + HEAD_DIM] = 1.0
    return e


def _blockdiag(blocks):
    n = len(blocks)
    r, c = blocks[0].shape
    out = jnp.zeros((n * r, n * c), blocks[0].dtype)
    for i, blk in enumerate(blocks):
        out = out.at[i * r:(i + 1) * r, i * c:(i + 1) * c].set(blk)
    return out


def _compress_weights(pe_k, pe_v, wk1, wk2, wv1, wv2):
    hop = CMP_STRIDE
    lblk = CMP_BLOCK

    nblk = 2 * N_NSA_KV
    stacked = jnp.stack([wk1] * N_NSA_KV + [wv1] * N_NSA_KV, axis=1)
    bd = jnp.einsum('lkdc,kj->lkdjc', stacked, jnp.eye(nblk, dtype=stacked.dtype))
    bd = bd.reshape(lblk, nblk * HEAD_DIM, nblk * HEAD_DIM).astype(BF16)
    pe = jnp.concatenate([pe_k] * N_NSA_KV + [pe_v] * N_NSA_KV, axis=1).astype(F32)

    def first_layer(lo):
        return bd[lo:lo + hop].reshape(hop * nblk * HEAD_DIM, nblk * HEAD_DIM)

    def pe_row(lo):
        return pe[lo:lo + hop].reshape(1, hop * nblk * HEAD_DIM)

    assert lblk == 2 * hop
    w2k = _blockdiag([wk2] * N_NSA_KV).astype(BF16)
    w2vt = _blockdiag([wv2.T] * N_NSA_KV).astype(BF16)
    return pe_row(0), pe_row(hop), first_layer(0), first_layer(hop), w2k, w2vt


def kernel(x, mem, norm_mix, w_in, b_forget, cmp_pe_k, cmp_pe_v, cmp_wk1, cmp_wk2, cmp_wv1, cmp_wv2, w_out, norm_cross, norm_mem, w_mq, w_mk, w_mv, w_mo, norm_ffn, w_up, conv_w, conv_b, w_down, norm_final):
    bsz, seq, d = x.shape
    depth = w_in.shape[0]
    t = bsz * seq
    dff = w_down.shape[1]
    assert seq % 512 == 0 and d % LANES == 0

    cols, n_in = _in_column_order()
    assert n_in == w_in.shape[2]
    col_ok = jnp.asarray(cols >= 0)[None, :]
    col_src = jnp.asarray(np.maximum(cols, 0))
    out_rows = jnp.asarray(_nsa_out_rows())
    ctab, s1tab, s2tab = _rope_tables(seq)
    ncp = seq // CMP_STRIDE
    ovt_np, n_cmp, n_sel = _overlap_t(seq, ncp)
    ovt = jnp.asarray(ovt_np, BF16)
    eneg = jnp.asarray(_sel_bias_table(seq, n_sel), BF16)
    later = jnp.asarray(np.tril(np.ones((SUB, SUB), np.float32), -1), BF16)
    egate = jnp.asarray(_gate_expanders(), BF16)
    nsa_w = N_NSA_HEADS * HEAD_DIM
    sb_w = N_SB_HEADS * HEAD_DIM
    bf_row = jnp.zeros((depth, 1, LANES), F32).at[:, 0, FL_LANE:FL_LANE + FL_COPIES * N_FOX_HEADS].set(
        jnp.tile(b_forget, (1, FL_COPIES)))

    xs = x.reshape(t, d)
    for i in range(depth):
        w = jnp.where(col_ok, jnp.take(w_in[i], col_src, axis=1), 0.0).astype(BF16)
        main, kcv, small = _proj(xs, norm_mix[i][None, :], w, ctab, s1tab, s2tab, seq, ROW_TILE)

        fox_qa, fox_ka = _gates(small.reshape(bsz, seq, LANES), bf_row[i])

        pea, peb, wa, wb, w2k, w2vt = _compress_weights(
            cmp_pe_k[i], cmp_pe_v[i], cmp_wk1[i], cmp_wk2[i], cmp_wv1[i], cmp_wv2[i])
        kc, vct = _compress(kcv.reshape(bsz, ncp, CMP_STRIDE * 2 * LANES), pea, peb, wa, wb, w2k, w2vt)

        ocmp, nm = _nsa_cmp(main, kc, vct, ovt, bsz, seq, CMP_Q_BLOCK, n_cmp, n_sel)
        o_nsa = _nsa_main(main, nm, eneg, small, egate, ocmp, bsz, seq)
        o_sb = _sb(main, later, bsz, seq)
        o_fox = _fox(main, fox_qa, fox_ka, bsz, seq)

        km, vm = _mem_kv(mem, norm_mem[i][None, :], w_mk[i].astype(BF16), w_mv[i].astype(BF16))
        wo = w_out[i]
        xs = _cross(xs, o_nsa, o_sb, o_fox,
                    jnp.take(wo[:nsa_w], out_rows, axis=0).astype(BF16),
                    wo[nsa_w:nsa_w + sb_w].astype(BF16), wo[nsa_w + sb_w:].astype(BF16),
                    norm_cross[i][None, :], w_mq[i].astype(BF16), km, vm, w_mo[i].astype(BF16), seq, CROSS_TILE)
        xs = _ffn(xs, norm_ffn[i][None, :], w_up[i].astype(BF16), conv_w[i], conv_b[i][None, :],
                  w_down[i].astype(BF16), norm_final[None, :], seq, ROW_TILE, i == depth - 1)
    return xs.reshape(bsz, seq, d)
```

```python
import functools

import numpy as np
import jax
import jax.numpy as jnp
from jax import lax
from jax.experimental import pallas as pl
from jax.experimental.pallas import tpu as pltpu

N_NSA_HEADS = 8
N_NSA_KV = 2
N_SB_HEADS = 4
N_FOX_HEADS = 4
HEAD_DIM = 64
ROPE_DIM = 16
ROPE_THETA = 500000.0
CMP_BLOCK = 32
CMP_STRIDE = 16
SEL_BLOCK = 64
SEL_TOPK = 16
WINDOW = 512
MEM_HEAD_DIM = 64
CONV_WIDTH = 3
EPS = 1e-6

LANES = 128
Q_BLOCK = 512
KEY_BLOCK = 512
SUB = 256
WIN_SLAB = 128
FL_LANE = 3 * N_NSA_HEADS
FL_COPIES = 6
ROW_TILE = 512
CROSS_TILE = 1024
CMP_Q_BLOCK = 256
NSA_TILES = 4
NEG_MASK = -1e30
DENOM_FLOOR = 1e-30
SEL_BIAS = -(2.0 ** 30)
VMEM_LIMIT = 48 * 1024 * 1024

F32 = jnp.float32
BF16 = jnp.bfloat16

T_QN, T_KS, T_VS, T_KW, T_VW = 0, 4, 5, 6, 7
T_QS, T_KSB, T_VSB = 8, 10, 12
T_QF, T_KF, T_VF = 14, 16, 18
N_MAIN_TILES = 20
ROPE_MAIN_TILES = (0, 1, 2, 3, T_KS, T_KW)

_NT = (((1,), (1,)), ((), ()))


def _params(n_grid):
    return pltpu.CompilerParams(dimension_semantics=("arbitrary",) * n_grid,
                                vmem_limit_bytes=VMEM_LIMIT)


def _rms(xf, g):
    return xf * lax.rsqrt(jnp.mean(xf * xf, axis=-1, keepdims=True) + EPS) * g


def _sigmoid(x):
    return 1.0 / (1.0 + jnp.exp(-x))


def _log_sigmoid(x):
    return jnp.minimum(x, 0.0) - jnp.log(1.0 + jnp.exp(-jnp.abs(x)))


def _dot(a, b):
    return jnp.dot(a, b, preferred_element_type=F32)


def _dot_nt(a, b):
    return lax.dot_general(a, b, _NT, preferred_element_type=F32)


def _split2(x):
    hi = x.astype(BF16)
    lo = (x - hi.astype(F32)).astype(BF16)
    return hi, lo


def _proj_kernel(x_ref, g_ref, w_ref, c_ref, s1_ref, s2_ref, main_ref, kcv_ref, small_ref):
    h = _rms(x_ref[...], g_ref[...]).astype(BF16)
    cos = c_ref[...]
    sin_lo = s1_ref[...]
    sin_hi = s2_ref[...]

    def rope(a):
        return a * cos + pltpu.roll(a, LANES - 8, 1) * sin_lo + pltpu.roll(a, 8, 1) * sin_hi

    for c in range(N_MAIN_TILES // 2):
        acc = _dot(h, w_ref[:, 2 * c * LANES:(2 * c + 2) * LANES])
        for k in range(2):
            t = 2 * c + k
            a = acc[:, k * LANES:(k + 1) * LANES]
            if t in ROPE_MAIN_TILES:
                a = rope(a)
            main_ref[:, t * LANES:(t + 1) * LANES] = a.astype(BF16)
    base = N_MAIN_TILES * LANES
    acc = _dot(h, w_ref[:, base:base + 2 * LANES])
    kcv_ref[:, :LANES] = rope(acc[:, :LANES])
    kcv_ref[:, LANES:] = acc[:, LANES:]
    small_ref[...] = _dot(h, w_ref[:, base + 2 * LANES:base + 3 * LANES])


def _proj(x2d, g, w, ctab, s1tab, s2tab, seq, tm):
    t = x2d.shape[0]
    d = x2d.shape[1]
    ncol = w.shape[1]
    spt = seq // tm
    tab = pl.BlockSpec((tm, LANES), lambda i: (i % spt, 0))
    return pl.pallas_call(
        _proj_kernel,
        grid=(t // tm,),
        in_specs=[pl.BlockSpec((tm, d), lambda i: (i, 0)),
                  pl.BlockSpec((1, d), lambda i: (0, 0)),
                  pl.BlockSpec((d, ncol), lambda i: (0, 0)),
                  tab, tab, tab],
        out_specs=[pl.BlockSpec((tm, N_MAIN_TILES * LANES), lambda i: (i, 0)),
                   pl.BlockSpec((tm, 2 * LANES), lambda i: (i, 0)),
                   pl.BlockSpec((tm, LANES), lambda i: (i, 0))],
        out_shape=[jax.ShapeDtypeStruct((t, N_MAIN_TILES * LANES), BF16),
                   jax.ShapeDtypeStruct((t, 2 * LANES), F32),
                   jax.ShapeDtypeStruct((t, LANES), F32)],
        compiler_params=_params(1),
        name="proj",
    )(x2d, g, w, ctab, s1tab, s2tab)


def _gates_kernel(s_ref, bf_ref, qa_ref, ka_ref):
    lf = _log_sigmoid(s_ref[...] + bf_ref[...])
    seq = lf.shape[0]
    row = lax.broadcasted_iota(jnp.int32, lf.shape, 0)
    sh = 1
    while sh < seq:
        lf = lf + jnp.where(row >= sh, pltpu.roll(lf, sh, 0), 0.0)
        sh *= 2
    hi = lf.astype(BF16).astype(F32)
    r1 = lf - hi
    mid = r1.astype(BF16).astype(F32)
    lo = (r1 - mid).astype(BF16).astype(F32)
    lane = lax.broadcasted_iota(jnp.int32, lf.shape, 1)

    def group(i):
        return (lane >= FL_LANE + i * N_FOX_HEADS) & (lane < FL_LANE + (i + 1) * N_FOX_HEADS)

    piece = jnp.where(group(0) | group(3), hi, jnp.where(group(1) | group(4), mid, lo))
    in_t = group(0) | group(1) | group(2)
    in_s = group(3) | group(4) | group(5)
    qa_ref[...] = jnp.where(in_t, piece, jnp.where(in_s, 1.0, 0.0)).astype(BF16)
    ka_ref[...] = jnp.where(in_t, 1.0, jnp.where(in_s, -piece, 0.0)).astype(BF16)


def _gates(small3, bf):
    b, seq, _ = small3.shape
    return pl.pallas_call(
        _gates_kernel,
        grid=(b,),
        in_specs=[pl.BlockSpec((None, seq, LANES), lambda i: (i, 0, 0)),
                  pl.BlockSpec((1, LANES), lambda i: (0, 0))],
        out_specs=[pl.BlockSpec((None, seq, LANES), lambda i: (i, 0, 0)),
                   pl.BlockSpec((None, seq, LANES), lambda i: (i, 0, 0))],
        out_shape=[jax.ShapeDtypeStruct((b, seq, LANES), BF16),
                   jax.ShapeDtypeStruct((b, seq, LANES), BF16)],
        compiler_params=_params(1),
        name="gates",
    )(small3, bf)


def _compress_kernel(r_ref, pea_ref, peb_ref, wa_ref, wb_ref, w2k_ref, w2vt_ref, kc_ref, vct_ref):
    r = r_ref[...]
    a = _dot((r + pea_ref[...]).astype(BF16), wa_ref[...])
    b = _dot((r + peb_ref[...]).astype(BF16), wb_ref[...])
    nrow = r.shape[0]
    hp = a + pltpu.roll(b, nrow - 1, 0)
    hid = (hp * _sigmoid(hp)).astype(BF16)
    kc_ref[...] = _dot(hid[:, :LANES], w2k_ref[...]).astype(BF16)
    vct_ref[...] = _dot_nt(w2vt_ref[...], hid[:, LANES:]).astype(BF16)


def _compress(r3, pea, peb, wa, wb, w2k, w2vt):
    b, nrow, width = r3.shape
    const = lambda i: (0, 0)
    return pl.pallas_call(
        _compress_kernel,
        grid=(b,),
        in_specs=[pl.BlockSpec((None, nrow, width), lambda i: (i, 0, 0)),
                  pl.BlockSpec((1, width), const), pl.BlockSpec((1, width), const),
                  pl.BlockSpec((width, 2 * LANES), const), pl.BlockSpec((width, 2 * LANES), const),
                  pl.BlockSpec((LANES, LANES), const), pl.BlockSpec((LANES, LANES), const)],
        out_specs=[pl.BlockSpec((None, nrow, LANES), lambda i: (i, 0, 0)),
                   pl.BlockSpec((None, LANES, nrow), lambda i: (i, 0, 0))],
        out_shape=[jax.ShapeDtypeStruct((b, nrow, LANES), BF16),
                   jax.ShapeDtypeStruct((b, LANES, nrow), BF16)],
        compiler_params=_params(1),
        name="compress",
    )(r3, pea, peb, wa, wb, w2k, w2vt)


def _nsa_cmp_kernel(q_ref, kc_ref, vct_ref, ovt_ref, ocmp_ref, nm_ref, *, qb, n_cmp, n_sel):
    qi = pl.program_id(1)
    ncp = kc_ref.shape[0]
    tq = qi * qb + lax.broadcasted_iota(jnp.int32, (ncp, qb), 1)
    nblk = lax.broadcasted_iota(jnp.int32, (ncp, qb), 0)
    cmask = (nblk * CMP_STRIDE + (CMP_BLOCK - 1) <= tq) & (nblk < n_cmp)
    row = lax.broadcasted_iota(jnp.int32, (LANES, qb), 0)
    lane = lax.broadcasted_iota(jnp.int32, (qb, LANES), 1)
    kc = kc_ref[...]
    vct = vct_ref[...]
    psum = [jnp.zeros((ncp, qb), F32), jnp.zeros((ncp, qb), F32)]
    for j in range(N_NSA_HEADS // 2):
        qt = q_ref[:, j * LANES:(j + 1) * LANES].astype(F32) * (HEAD_DIM ** -0.5)
        outs = []
        for half in range(2):
            qm = jnp.where(lane < HEAD_DIM if half == 0 else lane >= HEAD_DIM, qt, 0.0).astype(BF16)
            lt = _dot_nt(kc, qm)
            m = jnp.max(jnp.where(cmask, lt, NEG_MASK), axis=0, keepdims=True)
            p = jnp.where(cmask, jnp.exp(lt - m), 0.0)
            p = p / jnp.maximum(jnp.sum(p, axis=0, keepdims=True), DENOM_FLOOR)
            psum[half] = psum[half] + p
            outs.append(_dot(vct, p.astype(BF16)))
        ot = jnp.where(row < HEAD_DIM, outs[0], outs[1])
        for s in range(qb // LANES):
            ocmp_ref[s * LANES:(s + 1) * LANES, j * LANES:(j + 1) * LANES] = (
                ot[:, s * LANES:(s + 1) * LANES].T)

    jrow = lax.broadcasted_iota(jnp.int32, (n_sel, qb), 0)
    tsel = qi * qb + lax.broadcasted_iota(jnp.int32, (n_sel, qb), 1)
    cur = tsel // SEL_BLOCK
    forced = (jrow == 0) | (jrow == cur) | (jrow == cur - 1)
    ovt = ovt_ref[...]
    for g in range(N_NSA_KV):
        hi, lo = _split2(psum[g])
        imp = _dot(ovt, hi) + _dot(ovt, lo)
        imp = jnp.where(jrow <= cur, jnp.where(forced, jnp.inf, imp), -jnp.inf)
        before = jnp.zeros((n_sel, qb), F32)
        for i in range(n_sel):
            ri = imp[i:i + 1, :]
            ahead = (ri > imp) | ((ri == imp) & (jrow > i))
            before = before + jnp.where(ahead, 1.0, 0.0)
        member = (before < min(SEL_TOPK, n_sel)) & (imp > -jnp.inf)
        not_member = jnp.where(member, 0.0, 1.0)
        padded = jnp.concatenate([not_member, jnp.zeros((LANES - n_sel, qb), F32)], axis=0)
        for s in range(qb // LANES):
            nm_ref[g, s * LANES:(s + 1) * LANES, :] = (
                padded[:, s * LANES:(s + 1) * LANES].T.astype(BF16))


def _nsa_cmp(main, kc, vct, ovt, bsz, seq, qb, n_cmp, n_sel):
    t = main.shape[0]
    nq = seq // qb
    ncp = kc.shape[1]
    kern = functools.partial(_nsa_cmp_kernel, qb=qb, n_cmp=n_cmp, n_sel=n_sel)
    return pl.pallas_call(
        kern,
        grid=(bsz, nq),
        in_specs=[pl.BlockSpec((qb, 4 * LANES), lambda b, i: (b * nq + i, 0)),
                  pl.BlockSpec((None, ncp, LANES), lambda b, i: (b, 0, 0)),
                  pl.BlockSpec((None, LANES, ncp), lambda b, i: (b, 0, 0)),
                  pl.BlockSpec((n_sel, ncp), lambda b, i: (0, 0))],
        out_specs=[pl.BlockSpec((qb, 4 * LANES), lambda b, i: (b * nq + i, 0)),
                   pl.BlockSpec((None, N_NSA_KV, qb, LANES), lambda b, i: (b, 0, i, 0))],
        out_shape=[jax.ShapeDtypeStruct((t, 4 * LANES), F32),
                   jax.ShapeDtypeStruct((bsz, N_NSA_KV, seq, LANES), BF16)],
        compiler_params=_params(2),
        name="nsa_cmp",
    )(main, kc, vct, ovt)


def _softmax_step(s, mask, vaug, carry):
    m, acc = carry
    if mask is not None:
        s = jnp.where(mask, s, NEG_MASK)
    m_new = jnp.maximum(m, jnp.max(s, axis=-1, keepdims=True))
    p = jnp.exp(s - m_new)
    acc = jnp.exp(m - m_new) * acc + _dot(p.astype(BF16), vaug)
    return m_new, acc


def _softmax_finish(acc):
    return acc[:, :LANES] / jnp.maximum(acc[:, LANES:], DENOM_FLOOR)


def _pair_tile(is_a, a, b):
    return jnp.where(is_a, a, b)


def _attn_heads(q_ref):
    qb, width = q_ref.shape
    is_a = lax.broadcasted_iota(jnp.int32, (qb, LANES), 1) < HEAD_DIM
    heads = []
    for t in range(width // LANES):
        qt = q_ref[:, t * LANES:(t + 1) * LANES].astype(F32) * (HEAD_DIM ** -0.5)
        heads.append((t, jnp.where(is_a, qt, 0.0).astype(BF16)))
        heads.append((t, jnp.where(is_a, 0.0, qt).astype(BF16)))
    return is_a, heads


def _expand(x, e):
    hi, lo = _split2(x)
    return _dot(hi, e) + _dot(lo, e)


def _nsa_main_kernel(q_ref, ks_ref, vs_ref, kw_ref, vw_ref, nm_ref, eneg_ref, small_ref, egate_ref,
                     ocmp_ref, o_ref, win_ref, *acc_refs):
    qb = q_ref.shape[0]
    qi = pl.program_id(2)
    is_a, heads = _attn_heads(q_ref)
    nh = len(heads)
    rowi = lax.broadcasted_iota(jnp.int32, (qb, KEY_BLOCK), 0)
    coli = lax.broadcasted_iota(jnp.int32, (qb, KEY_BLOCK), 1)
    qsel = [jnp.concatenate([q, nm_ref[h % 2]], axis=1) for h, (_, q) in enumerate(heads)]
    ones_k = jnp.ones((KEY_BLOCK, LANES), BF16)

    def sel_chunk(c, ms, mask):
        off = pl.multiple_of(c * KEY_BLOCK, KEY_BLOCK)
        k = jnp.concatenate([ks_ref[pl.ds(off, KEY_BLOCK), :], eneg_ref[pl.ds(off, KEY_BLOCK), :]], axis=1)
        v = jnp.concatenate([vs_ref[pl.ds(off, KEY_BLOCK), :], ones_k], axis=1)
        out = []
        for h in range(nh):
            m_new, acc_new = _softmax_step(_dot_nt(qsel[h], k), mask, v, (ms[h], acc_refs[h][...]))
            out.append(m_new)
            acc_refs[h][...] = acc_new
        return tuple(out)

    for h in range(nh):
        acc_refs[h][...] = jnp.zeros((qb, 2 * LANES), F32)
    ms = tuple(jnp.full((qb, 1), NEG_MASK, F32) for _ in range(nh))
    ms = lax.fori_loop(0, qi, lambda c, ms: sel_chunk(c, ms, None), ms)
    sel_chunk(qi, ms, coli <= rowi)

    span = WINDOW + WIN_SLAB
    rw = lax.broadcasted_iota(jnp.int32, (WIN_SLAB, span), 0)
    cw = lax.broadcasted_iota(jnp.int32, (WIN_SLAB, span), 1)
    ones_w = jnp.ones((span, LANES), BF16)
    is_a_slab = lax.broadcasted_iota(jnp.int32, (WIN_SLAB, LANES), 1) < HEAD_DIM
    for r in range(qb // WIN_SLAB):
        rows = slice(r * WIN_SLAB, (r + 1) * WIN_SLAB)
        t0 = qi * qb + r * WIN_SLAB
        start = pl.multiple_of(jnp.maximum(t0 - WINDOW, 0), WIN_SLAB)
        diff = (t0 - start) + rw - cw
        mask = (diff >= 0) & (diff < WINDOW)
        k = kw_ref[pl.ds(start, span), :]
        v = jnp.concatenate([vw_ref[pl.ds(start, span), :], ones_w], axis=1)
        for t in range(nh // 2):
            outs = []
            for half in range(2):
                s = jnp.where(mask, _dot_nt(heads[2 * t + half][1][rows], k), NEG_MASK)
                p = jnp.exp(s - jnp.max(s, axis=-1, keepdims=True))
                outs.append(_softmax_finish(_dot(p.astype(BF16), v)))
            win_ref[rows, t * LANES:(t + 1) * LANES] = _pair_tile(is_a_slab, outs[0], outs[1])

    sig = _sigmoid(small_ref[...])
    for t in range(nh // 2):
        lanes = slice(t * LANES, (t + 1) * LANES)
        sel_t = _pair_tile(is_a, _softmax_finish(acc_refs[2 * t][...]), _softmax_finish(acc_refs[2 * t + 1][...]))
        gates = _expand(sig, egate_ref[t])
        out = (gates[:, :LANES] * ocmp_ref[:, lanes] + gates[:, LANES:2 * LANES] * sel_t
               + gates[:, 2 * LANES:] * win_ref[:, lanes])
        o_ref[:, lanes] = out.astype(BF16)


def _nsa_main(main, nm, eneg, small, egate, ocmp, bsz, seq):
    t = main.shape[0]
    qb = Q_BLOCK
    nq = seq // qb
    ntile = N_NSA_HEADS // 2
    nstep = ntile // NSA_TILES
    w = NSA_TILES * LANES
    kv = lambda tile: pl.BlockSpec((seq, LANES), lambda b, j, i: (b, tile))
    return pl.pallas_call(
        _nsa_main_kernel,
        grid=(bsz, nstep, nq),
        in_specs=[pl.BlockSpec((qb, w), lambda b, j, i: (b * nq + i, T_QN // NSA_TILES + j)),
                  kv(T_KS), kv(T_VS), kv(T_KW), kv(T_VW),
                  pl.BlockSpec((None, N_NSA_KV, qb, LANES), lambda b, j, i: (b, 0, i, 0)),
                  pl.BlockSpec((seq, LANES), lambda b, j, i: (0, 0)),
                  pl.BlockSpec((qb, LANES), lambda b, j, i: (b * nq + i, 0)),
                  pl.BlockSpec((NSA_TILES, LANES, 3 * LANES), lambda b, j, i: (j, 0, 0)),
                  pl.BlockSpec((qb, w), lambda b, j, i: (b * nq + i, j))],
        out_specs=pl.BlockSpec((qb, w), lambda b, j, i: (b * nq + i, j)),
        out_shape=jax.ShapeDtypeStruct((t, ntile * LANES), BF16),
        scratch_shapes=[pltpu.VMEM((qb, w), F32)] + [pltpu.VMEM((qb, 2 * LANES), F32)] * (2 * NSA_TILES),
        compiler_params=_params(3),
        name="nsa_main",
    )(main, main, main, main, main, nm, eneg, small, egate, ocmp)


def _sb_kernel(q_ref, k_ref, v_ref, u_ref, o_ref, *acc_refs):
    qb = q_ref.shape[0]
    qi = pl.program_id(1)
    is_a, heads = _attn_heads(q_ref)
    nh = len(heads)
    rowi = lax.broadcasted_iota(jnp.int32, (qb, KEY_BLOCK), 0)
    coli = lax.broadcasted_iota(jnp.int32, (qb, KEY_BLOCK), 1)
    row_s = lax.broadcasted_iota(jnp.int32, (qb, SUB), 0)
    col_s = lax.broadcasted_iota(jnp.int32, (qb, SUB), 1)
    later = u_ref[...]
    nsub = KEY_BLOCK // SUB

    def chunk(c, tails, diag):
        off = pl.multiple_of(c * KEY_BLOCK, KEY_BLOCK)
        new = []
        for h, (t, q) in enumerate(heads):
            lanes = slice(t * LANES, (t + 1) * LANES)
            k = k_ref[pl.ds(off, KEY_BLOCK), lanes]
            v = v_ref[pl.ds(off, KEY_BLOCK), lanes]
            tail = tails[h]
            z = _dot_nt(q, k)
            log_beta = _log_sigmoid(z)
            log_1m = log_beta - z
            if diag:
                log_1m = jnp.where(coli < rowi, log_1m, 0.0)
            parts = [None] * nsub
            for b in reversed(range(nsub)):
                lo_col, hi_col = b * SUB, (b + 1) * SUB
                x = log_1m[:, lo_col:hi_col]
                excl = _dot(x.astype(BF16), later)
                a = jnp.exp(log_beta[:, lo_col:hi_col] + excl + tail)
                if diag:
                    a = jnp.where(col_s + lo_col < row_s, a, 0.0)
                parts[b] = a.astype(BF16)
                tail = tail + jnp.sum(x, axis=-1, keepdims=True)
            acc_refs[h][...] += _dot(jnp.concatenate(parts, axis=1), v)
            new.append(tail)
        return tuple(new)

    for h in range(nh):
        acc_refs[h][...] = jnp.zeros((qb, LANES), F32)
    tails = chunk(qi, tuple(jnp.zeros((qb, 1), F32) for _ in range(nh)), True)
    lax.fori_loop(0, qi, lambda i, tails: chunk(qi - 1 - i, tails, False), tails)
    for t in range(nh // 2):
        o_ref[:, t * LANES:(t + 1) * LANES] = _pair_tile(
            is_a, acc_refs[2 * t][...], acc_refs[2 * t + 1][...]).astype(BF16)


def _sb(main, later, bsz, seq):
    t = main.shape[0]
    qb = Q_BLOCK
    nq = seq // qb
    w = N_SB_HEADS * HEAD_DIM
    blk = lambda tile: tile * LANES // w
    return pl.pallas_call(
        _sb_kernel,
        grid=(bsz, nq),
        in_specs=[pl.BlockSpec((qb, w), lambda b, i: (b * nq + i, blk(T_QS))),
                  pl.BlockSpec((seq, w), lambda b, i: (b, blk(T_KSB))),
                  pl.BlockSpec((seq, w), lambda b, i: (b, blk(T_VSB))),
                  pl.BlockSpec((SUB, SUB), lambda b, i: (0, 0))],
        out_specs=pl.BlockSpec((qb, w), lambda b, i: (b * nq + i, 0)),
        out_shape=jax.ShapeDtypeStruct((t, w), BF16),
        scratch_shapes=[pltpu.VMEM((qb, LANES), F32)] * N_SB_HEADS,
        compiler_params=_params(2),
        name="sb",
    )(main, main, main, later)


def _fox_kernel(q_ref, k_ref, v_ref, qa_ref, ka_ref, o_ref, *state):
    qb = q_ref.shape[0]
    qi = pl.program_id(1)
    is_a, heads = _attn_heads(q_ref)
    nh = len(heads)
    rowi = lax.broadcasted_iota(jnp.int32, (qb, KEY_BLOCK), 0)
    coli = lax.broadcasted_iota(jnp.int32, (qb, KEY_BLOCK), 1)
    lane = lax.broadcasted_iota(jnp.int32, (qb, LANES), 1)
    qa = qa_ref[...]
    qaug = []
    for h, (_, q) in enumerate(heads):
        assert FL_LANE % N_FOX_HEADS == 0 and N_FOX_HEADS & (N_FOX_HEADS - 1) == 0
        mine = ((lane >= FL_LANE) & (lane < FL_LANE + FL_COPIES * N_FOX_HEADS)
                & ((lane & (N_FOX_HEADS - 1)) == h))
        qaug.append(jnp.concatenate([q, jnp.where(mine, qa, jnp.zeros_like(qa))], axis=1))
    ones_k = jnp.ones((KEY_BLOCK, LANES), BF16)

    m_refs, acc_refs = state[:nh], state[nh:]

    def chunk(c, ms, mask):
        off = pl.multiple_of(c * KEY_BLOCK, KEY_BLOCK)
        ka = ka_ref[pl.ds(off, KEY_BLOCK), :]
        out = []
        for h, (t, _) in enumerate(heads):
            lanes = slice(t * LANES, (t + 1) * LANES)
            k = jnp.concatenate([k_ref[pl.ds(off, KEY_BLOCK), lanes], ka], axis=1)
            v = jnp.concatenate([v_ref[pl.ds(off, KEY_BLOCK), lanes], ones_k], axis=1)
            m_new, acc_new = _softmax_step(_dot_nt(qaug[h], k), mask, v, (ms[h], acc_refs[h][...]))
            out.append(m_new)
            acc_refs[h][...] = acc_new
        return tuple(out)

    for h in range(nh):
        acc_refs[h][...] = jnp.zeros((qb, 2 * LANES), F32)

    ms = tuple(jnp.full((qb, 1), NEG_MASK, F32) for _ in range(nh))
    ms = lax.fori_loop(0, qi, lambda c, ms: chunk(c, ms, None), ms)
    chunk(qi, ms, coli <= rowi)
    for t in range(nh // 2):
        out = _pair_tile(is_a, _softmax_finish(acc_refs[2 * t][...]), _softmax_finish(acc_refs[2 * t + 1][...]))
        o_ref[:, t * LANES:(t + 1) * LANES] = out.astype(BF16)


def _fox(main, qa, ka, bsz, seq):
    t = main.shape[0]
    qb = Q_BLOCK
    nq = seq // qb
    w = N_FOX_HEADS * HEAD_DIM
    blk = lambda tile: tile * LANES // w
    return pl.pallas_call(
        _fox_kernel,
        grid=(bsz, nq),
        in_specs=[pl.BlockSpec((qb, w), lambda b, i: (b * nq + i, blk(T_QF))),
                  pl.BlockSpec((seq, w), lambda b, i: (b, blk(T_KF))),
                  pl.BlockSpec((seq, w), lambda b, i: (b, blk(T_VF))),
                  pl.BlockSpec((None, qb, LANES), lambda b, i: (b, i, 0)),
                  pl.BlockSpec((None, seq, LANES), lambda b, i: (b, 0, 0))],
        out_specs=pl.BlockSpec((qb, w), lambda b, i: (b * nq + i, 0)),
        out_shape=jax.ShapeDtypeStruct((t, w), BF16),
        scratch_shapes=([pltpu.VMEM((qb, 1), F32)] * N_FOX_HEADS
                        + [pltpu.VMEM((qb, 2 * LANES), F32)] * N_FOX_HEADS),
        compiler_params=_params(2),
        name="fox",
    )(main, main, main, qa, ka)


def _mem_kv_kernel(m_ref, g_ref, wk_ref, wv_ref, k_ref, v_ref):
    h = _rms(m_ref[...], g_ref[...]).astype(BF16)
    k_ref[...] = _dot(h, wk_ref[...]).astype(BF16)
    v_ref[...] = _dot(h, wv_ref[...]).astype(BF16)


def _mem_kv(mem, g, wk, wv):
    b, m, d = mem.shape
    mw = wk.shape[1]
    const = lambda i: (0, 0)
    return pl.pallas_call(
        _mem_kv_kernel,
        grid=(b,),
        in_specs=[pl.BlockSpec((None, m, d), lambda i: (i, 0, 0)),
                  pl.BlockSpec((1, d), const), pl.BlockSpec((d, mw), const), pl.BlockSpec((d, mw), const)],
        out_specs=[pl.BlockSpec((None, m, mw), lambda i: (i, 0, 0)),
                   pl.BlockSpec((None, m, mw), lambda i: (i, 0, 0))],
        out_shape=[jax.ShapeDtypeStruct((b, m, mw), BF16), jax.ShapeDtypeStruct((b, m, mw), BF16)],
        compiler_params=_params(1),
        name="mem_kv",
    )(mem, g, wk, wv)


def _cross_kernel(x_ref, on_ref, os_ref, of_ref, won_ref, wos_ref, wof_ref, g_ref, wq_ref,
                  km_ref, vm_ref, wmo_ref, o_ref):
    x1 = (x_ref[...] + _dot(on_ref[...], won_ref[...]) + _dot(os_ref[...], wos_ref[...])
          + _dot(of_ref[...], wof_ref[...]))
    h = _rms(x1, g_ref[...]).astype(BF16)
    q = _dot(h, wq_ref[...]) * (MEM_HEAD_DIM ** -0.5)
    tm = q.shape[0]
    lane = lax.broadcasted_iota(jnp.int32, (tm, LANES), 1)
    is_a = lane < MEM_HEAD_DIM
    tiles = []
    for p in range(q.shape[1] // LANES):
        qt = q[:, p * LANES:(p + 1) * LANES]
        k = km_ref[:, p * LANES:(p + 1) * LANES]
        v = vm_ref[:, p * LANES:(p + 1) * LANES]
        outs = []
        for half in range(2):
            qm = jnp.where(is_a if half == 0 else jnp.logical_not(is_a), qt, 0.0).astype(BF16)
            s = _dot_nt(qm, k)
            e = jnp.exp(s - jnp.max(s, axis=-1, keepdims=True))
            pr = e / jnp.sum(e, axis=-1, keepdims=True)
            outs.append(_dot(pr.astype(BF16), v))
        tiles.append(_pair_tile(is_a, outs[0], outs[1]).astype(BF16))
    attn = jnp.concatenate(tiles, axis=1)
    o_ref[...] = x1 + _dot(attn, wmo_ref[...])


def _cross(x2d, on, osb, ofx, won, wos, wof, g, wq, km, vm, wmo, seq, tm):
    t, d = x2d.shape
    mw = wq.shape[1]
    m = km.shape[1]
    spt = seq // tm
    const = lambda i: (0, 0)
    row = lambda w: pl.BlockSpec((tm, w), lambda i: (i, 0))
    return pl.pallas_call(
        _cross_kernel,
        grid=(t // tm,),
        in_specs=[row(d), row(on.shape[1]), row(osb.shape[1]), row(ofx.shape[1]),
                  pl.BlockSpec(won.shape, const), pl.BlockSpec(wos.shape, const),
                  pl.BlockSpec(wof.shape, const), pl.BlockSpec((1, d), const),
                  pl.BlockSpec((d, mw), const),
                  pl.BlockSpec((None, m, mw), lambda i: (i // spt, 0, 0)),
                  pl.BlockSpec((None, m, mw), lambda i: (i // spt, 0, 0)),
                  pl.BlockSpec((mw, d), const)],
        out_specs=row(d),
        out_shape=jax.ShapeDtypeStruct((t, d), F32),
        compiler_params=_params(1),
        name="cross",
    )(x2d, on, osb, ofx, won, wos, wof, g, wq, km, vm, wmo)


def _ffn_kernel(x_ref, g_ref, wup_ref, cw_ref, cb_ref, wd_ref, gout_ref, o_ref, tail_ref,
                *, tiles_per_seq, norm_out):
    i = pl.program_id(0)
    x = x_ref[...]
    h = _rms(x, g_ref[...]).astype(BF16)
    tm = x.shape[0]
    dff = wd_ref.shape[0]
    rowi = lax.broadcasted_iota(jnp.int32, (tm, dff), 0)

    @pl.when((i % tiles_per_seq) == 0)
    def _():
        tail_ref[...] = jnp.zeros_like(tail_ref)

    def conv(kind):
        cols = slice(kind * dff, (kind + 1) * dff)
        u = _dot(h, wup_ref[:, cols])
        prev = tail_ref[kind]
        u1 = jnp.where(rowi == 0, prev[7:8], pltpu.roll(u, 1, 0))
        u2 = jnp.where(rowi == 0, prev[6:7], jnp.where(rowi == 1, prev[7:8], pltpu.roll(u, 2, 0)))
        tail_ref[kind] = u[tm - 8:, :]
        return cw_ref[2:3, cols] * u + cw_ref[1:2, cols] * u1 + cw_ref[0:1, cols] * u2 + cb_ref[:, cols]

    gate = conv(0)
    val = conv(1)
    act = (gate * _sigmoid(gate) * val).astype(BF16)
    y = x + _dot(act, wd_ref[...])
    o_ref[...] = _rms(y, gout_ref[...]) if norm_out else y


def _ffn(x2d, g, wup, cw, cb, wdown, gout, seq, tm, norm_out):
    t, d = x2d.shape
    dff = wdown.shape[0]
    const = lambda i: (0, 0)
    once = dict(pipeline_mode=pl.Buffered(1))
    return pl.pallas_call(
        functools.partial(_ffn_kernel, tiles_per_seq=seq // tm, norm_out=norm_out),
        grid=(t // tm,),
        in_specs=[pl.BlockSpec((tm, d), lambda i: (i, 0)),
                  pl.BlockSpec((1, d), const),
                  pl.BlockSpec((d, 2 * dff), const, **once),
                  pl.BlockSpec((CONV_WIDTH, 2 * dff), const),
                  pl.BlockSpec((1, 2 * dff), const),
                  pl.BlockSpec((dff, d), const, **once),
                  pl.BlockSpec((1, d), const)],
        out_specs=pl.BlockSpec((tm, d), lambda i: (i, 0)),
        out_shape=jax.ShapeDtypeStruct((t, d), F32),
        scratch_shapes=[pltpu.VMEM((2, 8, dff), F32)],
        compiler_params=_params(1),
        name="ffn",
    )(x2d, g, wup, cw, cb, wdown, gout)


def _in_column_order():
    nq = N_NSA_HEADS * HEAD_DIM
    nkv = N_NSA_KV * HEAD_DIM
    ngate = 3 * N_NSA_HEADS
    sbw = N_SB_HEADS * HEAD_DIM
    fxw = N_FOX_HEADS * HEAD_DIM
    off = {}
    pos = 0
    for name, size in [("qn", nq), ("kc", nkv), ("vc", nkv), ("ks", nkv), ("vs", nkv), ("kw", nkv),
                       ("vw", nkv), ("gn", ngate), ("qs", sbw), ("ksb", sbw), ("vsb", sbw),
                       ("qf", fxw), ("kf", fxw), ("vf", fxw), ("fl", N_FOX_HEADS)]:
        off[name] = (pos, size)
        pos += size
    rng = lambda name: list(range(off[name][0], off[name][0] + off[name][1]))
    group = N_NSA_HEADS // N_NSA_KV
    cols = []
    for j in range(group):
        for g in range(N_NSA_KV):
            h = g * group + j
            cols += list(range(off["qn"][0] + h * HEAD_DIM, off["qn"][0] + (h + 1) * HEAD_DIM))
    for name in ("ks", "vs", "kw", "vw", "qs", "ksb", "vsb", "qf", "kf", "vf", "kc", "vc", "gn"):
        cols += rng(name)
    cols += rng("fl") * FL_COPIES
    cols += [-1] * (LANES - ngate - FL_COPIES * N_FOX_HEADS)
    return np.asarray(cols, np.int32), pos


def _nsa_out_rows():
    group = N_NSA_HEADS // N_NSA_KV
    rows = []
    for j in range(group):
        for g in range(N_NSA_KV):
            h = g * group + j
            rows += list(range(h * HEAD_DIM, (h + 1) * HEAD_DIM))
    return np.asarray(rows, np.int32)


def _rope_tables(seq):
    half = ROPE_DIM // 2
    inv = ROPE_THETA ** (-jnp.arange(half, dtype=F32) / half)
    ang = jnp.arange(seq).astype(F32)[:, None] * inv[None, :]
    cos, sin = jnp.cos(ang), jnp.sin(ang)
    ones = jnp.ones((seq, HEAD_DIM - ROPE_DIM), F32)
    zeros = jnp.zeros((seq, HEAD_DIM - half), F32)
    ctab = jnp.concatenate([cos, cos, ones], axis=1)
    s1 = jnp.concatenate([-sin, zeros], axis=1)
    s2 = jnp.concatenate([jnp.zeros((seq, half), F32), sin, jnp.zeros((seq, HEAD_DIM - ROPE_DIM), F32)], axis=1)
    rep = LANES // HEAD_DIM
    return tuple(jnp.tile(a, (1, rep)) for a in (ctab, s1, s2))


def _overlap_t(seq, ncp):
    n_cmp = (seq - CMP_BLOCK) // CMP_STRIDE + 1
    n_sel = seq // SEL_BLOCK
    starts = np.arange(n_cmp) * CMP_STRIDE
    sel_starts = np.arange(n_sel) * SEL_BLOCK
    ov = ((starts[:, None] < sel_starts[None, :] + SEL_BLOCK)
          & (starts[:, None] + CMP_BLOCK > sel_starts[None, :])).astype(np.float32)
    out = np.zeros((n_sel, ncp), np.float32)
    out[:, :n_cmp] = ov.T
    return out, n_cmp, n_sel


def _sel_bias_table(seq, n_sel):
    tab = np.zeros((seq, LANES), np.float32)
    tab[np.arange(seq), np.arange(seq) // SEL_BLOCK] = SEL_BIAS
    return tab


def _gate_expanders():
    group = N_NSA_HEADS // N_NSA_KV
    e = np.zeros((group, LANES, 3 * LANES), np.float32)
    for j in range(group):
        for g in range(N_NSA_KV):
            h = g * group + j
            for r in range(3):
                lo = r * LANES + g * HEAD_DIM
                e[j, 3 * h + r, lo:lo + HEAD_DIM] = 1.0
    return e


def _blockdiag(blocks):
    n = len(blocks)
    r, c = blocks[0].shape
    out = jnp.zeros((n * r, n * c), blocks[0].dtype)
    for i, blk in enumerate(blocks):
        out = out.at[i * r:(i + 1) * r, i * c:(i + 1) * c].set(blk)
    return out


def _compress_weights(pe_k, pe_v, wk1, wk2, wv1, wv2):
    hop = CMP_STRIDE
    lblk = CMP_BLOCK

    nblk = 2 * N_NSA_KV
    stacked = jnp.stack([wk1] * N_NSA_KV + [wv1] * N_NSA_KV, axis=1)
    bd = jnp.einsum('lkdc,kj->lkdjc', stacked, jnp.eye(nblk, dtype=stacked.dtype))
    bd = bd.reshape(lblk, nblk * HEAD_DIM, nblk * HEAD_DIM).astype(BF16)
    pe = jnp.concatenate([pe_k] * N_NSA_KV + [pe_v] * N_NSA_KV, axis=1).astype(F32)

    def first_layer(lo):
        return bd[lo:lo + hop].reshape(hop * nblk * HEAD_DIM, nblk * HEAD_DIM)

    def pe_row(lo):
        return pe[lo:lo + hop].reshape(1, hop * nblk * HEAD_DIM)

    assert lblk == 2 * hop
    w2k = _blockdiag([wk2] * N_NSA_KV).astype(BF16)
    w2vt = _blockdiag([wv2.T] * N_NSA_KV).astype(BF16)
    return pe_row(0), pe_row(hop), first_layer(0), first_layer(hop), w2k, w2vt


def kernel(x, mem, norm_mix, w_in, b_forget, cmp_pe_k, cmp_pe_v, cmp_wk1, cmp_wk2, cmp_wv1, cmp_wv2, w_out, norm_cross, norm_mem, w_mq, w_mk, w_mv, w_mo, norm_ffn, w_up, conv_w, conv_b, w_down, norm_final):
    bsz, seq, d = x.shape
    depth = w_in.shape[0]
    t = bsz * seq
    dff = w_down.shape[1]
    assert seq % Q_BLOCK == 0 and seq % CROSS_TILE == 0 and seq >= WINDOW + WIN_SLAB and d % LANES == 0

    cols, n_in = _in_column_order()
    assert n_in == w_in.shape[2]
    col_ok = jnp.asarray(cols >= 0)[None, :]
    col_src = jnp.asarray(np.maximum(cols, 0))
    out_rows = jnp.asarray(_nsa_out_rows())
    ctab, s1tab, s2tab = _rope_tables(seq)
    ncp = seq // CMP_STRIDE
    ovt_np, n_cmp, n_sel = _overlap_t(seq, ncp)
    ovt = jnp.asarray(ovt_np, BF16)
    eneg = jnp.asarray(_sel_bias_table(seq, n_sel), BF16)
    later = jnp.asarray(np.tril(np.ones((SUB, SUB), np.float32), -1), BF16)
    egate = jnp.asarray(_gate_expanders(), BF16)
    nsa_w = N_NSA_HEADS * HEAD_DIM
    sb_w = N_SB_HEADS * HEAD_DIM
    bf_row = jnp.zeros((depth, 1, LANES), F32).at[:, 0, FL_LANE:FL_LANE + FL_COPIES * N_FOX_HEADS].set(
        jnp.tile(b_forget, (1, FL_COPIES)))

    xs = x.reshape(t, d)
    for i in range(depth):
        w = jnp.where(col_ok, jnp.take(w_in[i], col_src, axis=1), 0.0).astype(BF16)
        main, kcv, small = _proj(xs, norm_mix[i][None, :], w, ctab, s1tab, s2tab, seq, ROW_TILE)

        fox_qa, fox_ka = _gates(small.reshape(bsz, seq, LANES), bf_row[i])

        pea, peb, wa, wb, w2k, w2vt = _compress_weights(
            cmp_pe_k[i], cmp_pe_v[i], cmp_wk1[i], cmp_wk2[i], cmp_wv1[i], cmp_wv2[i])
        kc, vct = _compress(kcv.reshape(bsz, ncp, CMP_STRIDE * 2 * LANES), pea, peb, wa, wb, w2k, w2vt)

        ocmp, nm = _nsa_cmp(main, kc, vct, ovt, bsz, seq, CMP_Q_BLOCK, n_cmp, n_sel)
        o_nsa = _nsa_main(main, nm, eneg, small, egate, ocmp, bsz, seq)
        o_sb = _sb(main, later, bsz, seq)
        o_fox = _fox(main, fox_qa, fox_ka, bsz, seq)

        km, vm = _mem_kv(mem, norm_mem[i][None, :], w_mk[i].astype(BF16), w_mv[i].astype(BF16))
        wo = w_out[i]
        xs = _cross(xs, o_nsa, o_sb, o_fox,
                    jnp.take(wo[:nsa_w], out_rows, axis=0).astype(BF16),
                    wo[nsa_w:nsa_w + sb_w].astype(BF16), wo[nsa_w + sb_w:].astype(BF16),
                    norm_cross[i][None, :], w_mq[i].astype(BF16), km, vm, w_mo[i].astype(BF16), seq, CROSS_TILE)
        xs = _ffn(xs, norm_ffn[i][None, :], w_up[i].astype(BF16), conv_w[i], conv_b[i][None, :],
                  w_down[i].astype(BF16), norm_final[None, :], seq, ROW_TILE, i == depth - 1)
    return xs.reshape(bsz, seq, d)
```
